```python
import jax
import jax.numpy as jnp
from jax import lax
import numpy as np

D_MODEL = 1024
BATCH = 4
SEQ = 8192
DEPTH = 1

HEAD_DIM = 64
ROPE_THETA = 10000.0
RMS_EPS = 1e-6
Q_BLOCK = 128

A_HEADS = 8
A_KV_HEADS = 2
IDX_HEADS = 4
IDX_DIM = 64
TOPK_MAX = 256

B_PATTERNS = ((128, 1), (512, 4), (2048, 16))
B_N_GROUPS = 3
B_HEADS = 8

MEM_LEN = 256
M_HEADS = 4
M_HEAD_DIM = 128

BRANCH_WIDTH = 512
N_BRANCHES = 3

MOE_GROUPS = 4
EXPERTS_PER_GROUP = 4
N_EXPERTS = 16
MOE_TOPK = 2
EXPERT_HIDDEN = 512

IN_SPLITS = (512, 128, 128, 256, 64, 4, 1536, 1536, 1536, 512)
IN_WIDTH = 6212

kernel_name = 'hybrid_dsa_dilated_memory_hmoe'


def rmsnorm(x, g):
    xf = x.astype(jnp.float32)
    y = xf * lax.rsqrt(jnp.mean(xf * xf, axis=-1, keepdims=True) + RMS_EPS)
    return (y * g.astype(jnp.float32)).astype(x.dtype)


def rope(x):
    s, dh = x.shape[1], x.shape[-1]
    half = dh // 2
    inv = ROPE_THETA ** (-jnp.arange(half, dtype=jnp.float32) / half)
    ang = jnp.arange(s, dtype=jnp.float32)[:, None] * inv[None, :]
    cos = jnp.cos(ang)[None, :, None, :]
    sin = jnp.sin(ang)[None, :, None, :]
    xf = x.astype(jnp.float32)
    x1, x2 = xf[..., :half], xf[..., half:]
    return jnp.concatenate([x1 * cos - x2 * sin, x2 * cos + x1 * sin], axis=-1).astype(x.dtype)


def split_cols(a, sizes):
    out, off = [], 0
    for n in sizes:
        out.append(a[..., off:off + n])
        off += n
    return out


def sweep_query_blocks(fn, seq_len):
    out = lax.map(fn, jnp.arange(seq_len // Q_BLOCK))
    nb, b = out.shape[0], out.shape[1]
    return jnp.moveaxis(out, 0, 1).reshape(b, nb * Q_BLOCK, *out.shape[3:])


def dsa_attention(q, k, v, q_idx, k_idx, w_idx):
    b, s, h, dh = q.shape
    kvh = k.shape[2]
    g = h // kvh
    topk = min(TOPK_MAX, s // 4)
    key_pos = jnp.arange(s)
    local = jnp.arange(Q_BLOCK)
    k_idx_f = k_idx.astype(jnp.float32)

    def block(i):
        q0 = i * Q_BLOCK
        qpos = q0 + local
        qi = lax.dynamic_slice_in_dim(q_idx, q0, Q_BLOCK, axis=1).astype(jnp.float32)
        wi = lax.dynamic_slice_in_dim(w_idx, q0, Q_BLOCK, axis=1).astype(jnp.float32)
        dots = jnp.einsum('bqhd,bsd->bqhs', qi, k_idx_f) * (IDX_DIM ** -0.5)
        score = jnp.einsum('bqhs,bqh->bqs', jax.nn.relu(dots), wi) * (IDX_HEADS ** -0.5)
        causal = key_pos[None, :] <= qpos[:, None]
        score = jnp.where(causal[None], score, -jnp.inf)
        _, sel = lax.top_k(score, topk)
        valid = sel <= qpos[None, :, None]
        ks = jax.vmap(lambda kk, ii: kk[ii])(k, sel)
        vs = jax.vmap(lambda vv, ii: vv[ii])(v, sel)
        qb = lax.dynamic_slice_in_dim(q, q0, Q_BLOCK, axis=1).reshape(b, Q_BLOCK, kvh, g, dh)
        logits = jnp.einsum('bqkgd,bqjkd->bqkgj', qb, ks).astype(jnp.float32) * (dh ** -0.5)
        logits = jnp.where(valid[:, :, None, None, :], logits, -jnp.inf)
        p = jax.nn.softmax(logits, axis=-1)
        o = jnp.einsum('bqkgj,bqjkd->bqkgd', p.astype(vs.dtype), vs)
        return o.reshape(b, Q_BLOCK, h * dh)

    return sweep_query_blocks(block, s)


def dilated_attention(q, k, v):
    b, s, ng, h, dh = q.shape
    local = jnp.arange(Q_BLOCK)
    qs = [q[:, :, gi] for gi in range(ng)]
    kp = [jnp.pad(k[:, :, gi], ((0, 0), (w, 0), (0, 0), (0, 0))) for gi, (w, _) in enumerate(B_PATTERNS)]
    vp = [jnp.pad(v[:, :, gi], ((0, 0), (w, 0), (0, 0), (0, 0))) for gi, (w, _) in enumerate(B_PATTERNS)]

    def block(i):
        q0 = i * Q_BLOCK
        outs, lses = [], []
        for gi, (window, dil) in enumerate(B_PATTERNS):
            steps = jnp.arange(window // dil + 1)
            back = local[:, None] - steps[None, :] * dil
            valid = (q0 + back) >= 0
            rel = window + back
            ks = lax.dynamic_slice_in_dim(kp[gi], q0, window + Q_BLOCK, axis=1)[:, rel]
            vs = lax.dynamic_slice_in_dim(vp[gi], q0, window + Q_BLOCK, axis=1)[:, rel]
            qg = lax.dynamic_slice_in_dim(qs[gi], q0, Q_BLOCK, axis=1)
            logits = jnp.einsum('bqhd,bqjhd->bqhj', qg, ks).astype(jnp.float32) * (dh ** -0.5)
            logits = jnp.where(valid[None, :, None, :], logits, -jnp.inf)
            lse = jax.nn.logsumexp(logits, axis=-1)
            p = jnp.exp(logits - lse[..., None])
            outs.append(jnp.einsum('bqhj,bqjhd->bqhd', p.astype(vs.dtype), vs))
            lses.append(lse)
        alpha = jax.nn.softmax(jnp.stack(lses, axis=2), axis=2)
        o = jnp.einsum('bqgh,bqghd->bqhd', alpha.astype(outs[0].dtype), jnp.stack(outs, axis=2))
        return o.reshape(b, Q_BLOCK, h * dh)

    return sweep_query_blocks(block, s)


def memory_attention(qm, mem_n, w_mem_kv):
    b, m, _ = mem_n.shape
    km, vm = split_cols(mem_n @ w_mem_kv, (M_HEADS * M_HEAD_DIM, M_HEADS * M_HEAD_DIM))
    km = km.reshape(b, m, M_HEADS, M_HEAD_DIM)
    vm = vm.reshape(b, m, M_HEADS, M_HEAD_DIM)
    logits = jnp.einsum('bshd,bmhd->bhsm', qm, km).astype(jnp.float32) * (M_HEAD_DIM ** -0.5)
    p = jax.nn.softmax(logits, axis=-1)
    o = jnp.einsum('bhsm,bmhd->bshd', p.astype(vm.dtype), vm)
    return o.reshape(b, qm.shape[1], M_HEADS * M_HEAD_DIM)


def hier_moe(xn, w_group, b_group, w_sub, b_sub, w1, w3, w2):
    b, s, d = xn.shape
    t = xn.reshape(b * s, d)
    gl = (t @ w_group).astype(jnp.float32) + b_group.astype(jnp.float32)
    gp = jax.nn.softmax(gl, axis=-1)
    g_w, g_sel = lax.top_k(gp, 1)
    sl = jnp.einsum('td,gde->tge', t, w_sub).astype(jnp.float32) + b_sub.astype(jnp.float32)
    sl = jnp.take_along_axis(sl, g_sel[:, :, None], axis=1)[:, 0]
    sp = jax.nn.softmax(sl, axis=-1)
    top_p, top_e = lax.top_k(sp, MOE_TOPK)
    top_p = top_p / jnp.sum(top_p, axis=-1, keepdims=True)
    expert_id = g_sel * EXPERTS_PER_GROUP + top_e
    combine = jnp.einsum('tk,tke->te', g_w * top_p, jax.nn.one_hot(expert_id, N_EXPERTS, dtype=jnp.float32))
    out = jnp.zeros((b * s, d), jnp.float32)
    for e in range(N_EXPERTS):
        hid = jax.nn.silu(t @ w1[e]) * (t @ w3[e])
        out = out + combine[:, e:e + 1] * (hid @ w2[e]).astype(jnp.float32)
    return out.astype(xn.dtype).reshape(b, s, d)


def setup_inputs(seed: int = 0) -> dict:
    key = jax.random.key(seed)
    ks = jax.random.split(key, 19)
    f32 = jnp.float32
    L, D = DEPTH, D_MODEL

    def dense(k, shape, fan_in):
        return jax.random.normal(k, shape, f32) * (fan_in ** -0.5)

    def gain(k, shape):
        return 1.0 + 0.02 * jax.random.normal(k, shape, f32)

    def small(k, shape, scale):
        return scale * jax.random.normal(k, shape, f32)

    return {
        'x': jax.random.normal(ks[0], (BATCH, SEQ, D), f32),
        'mem': jax.random.normal(ks[1], (BATCH, MEM_LEN, D), f32),
        'g_mix': gain(ks[2], (L, D)),
        'g_mem': gain(ks[3], (L, D)),
        'w_in': dense(ks[4], (L, D, IN_WIDTH), D),
        'w_mem_kv': dense(ks[5], (L, D, 2 * M_HEADS * M_HEAD_DIM), D),
        'w_gate': dense(ks[6], (L, D, N_BRANCHES * D), D),
        'b_gate': small(ks[7], (L, N_BRANCHES * D), 0.1),
        'w_branch': dense(ks[8], (L, N_BRANCHES, BRANCH_WIDTH, D), BRANCH_WIDTH),
        'w_out': dense(ks[9], (L, D, D), D),
        'g_ffn': gain(ks[10], (L, D)),
        'w_group': dense(ks[11], (L, D, MOE_GROUPS), D),
        'b_group': small(ks[12], (L, MOE_GROUPS), 0.01),
        'w_sub': dense(ks[13], (L, MOE_GROUPS, D, EXPERTS_PER_GROUP), D),
        'b_sub': small(ks[14], (L, MOE_GROUPS, EXPERTS_PER_GROUP), 0.01),
        'w1': dense(ks[15], (L, N_EXPERTS, D, EXPERT_HIDDEN), D),
        'w3': dense(ks[16], (L, N_EXPERTS, D, EXPERT_HIDDEN), D),
        'w2': dense(ks[17], (L, N_EXPERTS, EXPERT_HIDDEN, D), EXPERT_HIDDEN),
        'g_final': gain(ks[18], (D,)),
    }


def reference(x, mem, g_mix, g_mem, w_in, w_mem_kv, w_gate, b_gate, w_branch, w_out, g_ffn,
              w_group, b_group, w_sub, b_sub, w1, w3, w2, g_final):
    b, s, d = x.shape
    h = x
    for l in range(DEPTH):
        n = rmsnorm(h, g_mix[l])
        aq, ak, av, iq, ik, iw, bq, bk, bv, mq = split_cols(n @ w_in[l], IN_SPLITS)
        aq = rope(aq.reshape(b, s, A_HEADS, HEAD_DIM))
        ak = rope(ak.reshape(b, s, A_KV_HEADS, HEAD_DIM))
        av = av.reshape(b, s, A_KV_HEADS, HEAD_DIM)
        iq = rope(iq.reshape(b, s, IDX_HEADS, IDX_DIM))
        ik = rope(ik.reshape(b, s, 1, IDX_DIM))[:, :, 0]
        y_a = dsa_attention(aq, ak, av, iq, ik, iw)
        bq = rope(bq.reshape(b, s, B_N_GROUPS * B_HEADS, HEAD_DIM)).reshape(b, s, B_N_GROUPS, B_HEADS, HEAD_DIM)
        bk = rope(bk.reshape(b, s, B_N_GROUPS * B_HEADS, HEAD_DIM)).reshape(b, s, B_N_GROUPS, B_HEADS, HEAD_DIM)
        bv = bv.reshape(b, s, B_N_GROUPS, B_HEADS, HEAD_DIM)
        y_b = dilated_attention(bq, bk, bv)
        y_m = memory_attention(mq.reshape(b, s, M_HEADS, M_HEAD_DIM), rmsnorm(mem, g_mem[l]), w_mem_kv[l])
        gates = jax.nn.sigmoid((n @ w_gate[l] + b_gate[l]).astype(jnp.float32)).astype(n.dtype)
        gates = gates.reshape(b, s, N_BRANCHES, d)
        merged = (gates[:, :, 0] * (y_a @ w_branch[l, 0])
                  + gates[:, :, 1] * (y_b @ w_branch[l, 1])
                  + gates[:, :, 2] * (y_m @ w_branch[l, 2]))
        h = h + merged @ w_out[l]
        h = h + hier_moe(rmsnorm(h, g_ffn[l]), w_group[l], b_group[l], w_sub[l], b_sub[l], w1[l], w3[l], w2[l])
    return rmsnorm(h, g_final)
```

```python
import functools
import math

import jax
import jax.numpy as jnp
from jax import lax
from jax.experimental import pallas as pl
from jax.experimental.pallas import tpu as pltpu

HEAD_DIM = 64
ROPE_THETA = 10000.0
RMS_EPS = 1e-6
A_HEADS = 8
A_KV_HEADS = 2
IDX_HEADS = 4
TOPK_MAX = 256
B_PATTERNS = ((128, 1), (512, 4), (2048, 16))
B_HEADS = 8
M_HEADS = 4
M_HEAD_DIM = 128
MOE_GROUPS = 4
EXPERTS_PER_GROUP = 4
N_EXPERTS = 16
IN_SPLITS = (512, 128, 128, 256, 64, 4, 1536, 1536, 1536, 512)

LANES = 128
PROJ_TILE = 512
VMEM_LIMIT = 56 * 1024 * 1024

T_AQ, T_IQ, T_BQ, T_BK, T_MIX, T_BV, T_MQ = 0, 2, 3, 6, 9, 10, 13
N_PROJ_TILES = 14
N_ROPE_TILES = 9
P_COLS = N_PROJ_TILES * PROJ_TILE
SLAB_AK = T_MIX * 4 + 0
SLAB_IK = T_MIX * 4 + 1
SLAB_AV = T_MIX * 4 + 2
SLAB_IW = T_MIX * 4 + 3

INT_MIN = -(2 ** 31)
NEG_BIG = -1e30

R_SUB0 = MOE_GROUPS


def _cparams(sem):
    return pltpu.CompilerParams(dimension_semantics=sem, vmem_limit_bytes=VMEM_LIMIT)


def _rms(xf, g):
    return xf * lax.rsqrt(jnp.mean(xf * xf, axis=-1, keepdims=True) + RMS_EPS) * g


def _rope_slab(y, cos, sin_signed, first_half):
    partner = jnp.where(first_half, pltpu.roll(y, 96, 1), pltpu.roll(y, 32, 1))
    return y * cos + partner * sin_signed


def _in_proj_kernel(x_ref, g_ref, w_ref, cos_ref, sin_ref, o_ref, n_ref, *, rows):
    j = pl.program_id(1)
    tm = x_ref.shape[0]

    @pl.when(j == 0)
    def _():
        n_ref[...] = _rms(x_ref[...], g_ref[...]).astype(jnp.bfloat16)

    lane = lax.broadcasted_iota(jnp.int32, (rows, LANES), 1)
    first_half = (lane & (HEAD_DIM - 1)) < (HEAD_DIM // 2)

    def epilogue(slab_modes):
        for r in range(tm // rows):
            rs = pl.ds(r * rows, rows)
            acc = jnp.dot(n_ref[rs, :], w_ref[...], preferred_element_type=jnp.float32)
            cos = cos_ref[rs, :]
            sin = sin_ref[rs, :]
            for s in range(PROJ_TILE // LANES):
                y = acc[:, s * LANES:(s + 1) * LANES]
                if slab_modes[s] == "rope":
                    y = _rope_slab(y, cos, sin, first_half)
                elif slab_modes[s] == "hi_lo":
                    resid = y - y.astype(jnp.bfloat16).astype(jnp.float32)
                    y = jnp.where(lane < IDX_HEADS, y, resid)
                o_ref[rs, s * LANES:(s + 1) * LANES] = y.astype(o_ref.dtype)

    @pl.when(j < N_ROPE_TILES)
    def _():
        epilogue(("rope",) * 4)

    @pl.when(j == T_MIX)
    def _():
        epilogue(("rope", "rope", "plain", "hi_lo"))

    @pl.when(j > T_MIX)
    def _():
        epilogue(("plain",) * 4)


def _in_proj(x2, g_mix, wp, cos_t, sin_t, seq):
    t, d = x2.shape
    tm = min(1024, seq)
    nseq = seq // tm
    return pl.pallas_call(
        functools.partial(_in_proj_kernel, rows=256),
        grid=(t // tm, N_PROJ_TILES),
        in_specs=[
            pl.BlockSpec((tm, d), lambda i, j: (i, 0)),
            pl.BlockSpec((1, d), lambda i, j: (0, 0)),
            pl.BlockSpec((d, PROJ_TILE), lambda i, j: (0, j)),
            pl.BlockSpec((tm, LANES), lambda i, j: (i % nseq, 0)),
            pl.BlockSpec((tm, LANES), lambda i, j: (i % nseq, 0)),
        ],
        out_specs=pl.BlockSpec((tm, PROJ_TILE), lambda i, j: (i, j)),
        out_shape=jax.ShapeDtypeStruct((t, P_COLS), jnp.bfloat16),
        scratch_shapes=[pltpu.VMEM((tm, d), jnp.bfloat16)],
        compiler_params=_cparams(("parallel", "arbitrary")),
        name="in_proj",
    )(x2, g_mix, wp, cos_t, sin_t)


def _mem_kv_kernel(m_ref, g_ref, w_ref, o_ref):
    n = _rms(m_ref[0], g_ref[...]).astype(jnp.bfloat16)
    o_ref[0] = jnp.dot(n, w_ref[...], preferred_element_type=jnp.float32).astype(o_ref.dtype)


def _mem_kv(mem, g_mem, w_kv):
    b, m, d = mem.shape
    n = w_kv.shape[1]
    return pl.pallas_call(
        _mem_kv_kernel,
        grid=(b,),
        in_specs=[
            pl.BlockSpec((1, m, d), lambda i: (i, 0, 0)),
            pl.BlockSpec((1, d), lambda i: (0, 0)),
            pl.BlockSpec((d, n), lambda i: (0, 0)),
        ],
        out_specs=pl.BlockSpec((1, m, n), lambda i: (i, 0, 0)),
        out_shape=jax.ShapeDtypeStruct((b, m, n), jnp.bfloat16),
        compiler_params=_cparams(("parallel",)),
        name="mem_kv",
    )(mem, g_mem, w_kv)


def _sortable_key(s):
    bits = lax.bitcast_convert_type(s, jnp.int32)
    bits = jnp.where(bits == INT_MIN, 0, bits)
    return jnp.where(bits < 0, bits ^ jnp.int32(0x7FFFFFFF), bits)


def _dsa_kernel(aq_ref, iq_ref, iw_ref, ik_ref, ak_ref, av_ref, tri_ref, o_ref,
                keys_ref, qs_ref, thr_ref, need_ref, off_ref, m_ref, l_ref, acc_ref, p_ref,
                *, tq, kc, topk, search_rows):
    i = pl.program_id(1)
    q0 = i * tq
    nchunks = (q0 + tq + kc - 1) // kc
    nslab = kc // LANES
    group = A_HEADS // A_KV_HEADS

    iw = iw_ref[0].astype(jnp.float32)
    w_heads = [iw[:, h:h + 1] + iw[:, IDX_HEADS + h:IDX_HEADS + h + 1] for h in range(IDX_HEADS)]
    qpos = q0 + lax.broadcasted_iota(jnp.int32, (tq, kc), 0)
    kiota = lax.broadcasted_iota(jnp.int32, (tq, kc), 1)

    def score_chunk(c, carry):
        k0 = pl.multiple_of(c * kc, kc)
        ikc = ik_ref[0, pl.ds(k0, kc), :]
        s = jnp.zeros((tq, kc), jnp.float32)
        for h in range(IDX_HEADS):
            d = lax.dot_general(iq_ref[0, :, h * LANES:(h + 1) * LANES], ikc,
                                (((1,), (1,)), ((), ())), preferred_element_type=jnp.float32)
            s = s + jnp.maximum(d, 0.0) * w_heads[h]
        key = _sortable_key(s)
        keys_ref[c] = jnp.where(kiota + k0 <= qpos, key, jnp.int32(INT_MIN))
        return carry

    lax.fori_loop(0, nchunks, score_chunk, 0)

    def count_ge(rs, cand_b, strict):
        def body(c, acc):
            for s in range(nslab):
                kk = keys_ref[c, rs, s * LANES:(s + 1) * LANES]
                hit = (kk > cand_b) if strict else (kk >= cand_b)
                acc = acc + jnp.where(hit, 1.0, 0.0)
            return acc
        acc = lax.fori_loop(0, nchunks, body, jnp.zeros((search_rows, LANES), jnp.float32))
        return jnp.sum(acc, axis=-1, keepdims=True)

    for rg in range(tq // search_rows):
        rs = pl.ds(rg * search_rows, search_rows)

        def bit_step(it, prefix):
            cand = prefix + lax.shift_left(jnp.int32(1), 31 - it)
            cnt = count_ge(rs, jnp.broadcast_to(cand, (search_rows, LANES)), False)
            return jnp.where(cnt >= float(topk), cand, prefix)

        thr = lax.fori_loop(0, 32, bit_step, jnp.full((search_rows, 1), INT_MIN, jnp.int32))
        thr_b = jnp.broadcast_to(thr, (search_rows, LANES))
        c_gt = count_ge(rs, thr_b, True)
        need = jnp.where(thr == INT_MIN, 0.0, float(topk) - c_gt)
        thr_ref[rs, :] = thr_b
        need_ref[rs, :] = jnp.broadcast_to(need, (search_rows, LANES))

    for j in range(A_KV_HEADS):
        for g in range(group):
            h = j * group + g
            qs_ref[j, g * tq:(g + 1) * tq, :] = aq_ref[0, :, h * LANES:(h + 1) * LANES]
    off_ref[...] = jnp.zeros_like(off_ref)
    m_ref[...] = jnp.full(m_ref.shape, NEG_BIG, jnp.float32)
    l_ref[...] = jnp.zeros_like(l_ref)
    acc_ref[...] = jnp.zeros_like(acc_ref)
    half = tri_ref.shape[0]

    def attn_chunk(c, carry):
        k0 = pl.multiple_of(c * kc, kc)
        akc = ak_ref[0, pl.ds(k0, kc), :]
        avc = av_ref[0, pl.ds(k0, kc), :]
        thr_b = thr_ref[...]
        need_b = need_ref[...]
        bias = []
        off = off_ref[...]
        for hh in range(kc // half):
            eq_parts = []
            for s in range(half // LANES):
                kk = keys_ref[c, :, hh * half + s * LANES: hh * half + (s + 1) * LANES]
                eq_parts.append(jnp.where(kk == thr_b, 1.0, 0.0).astype(jnp.bfloat16))
            eqf = jnp.concatenate(eq_parts, axis=1)
            pre = jnp.dot(eqf, tri_ref[...], preferred_element_type=jnp.float32)
            for s in range(half // LANES):
                kk = keys_ref[c, :, hh * half + s * LANES: hh * half + (s + 1) * LANES]
                rank = pre[:, s * LANES:(s + 1) * LANES] + off
                tie_ok = (kk == thr_b) & (rank <= need_b)
                sel = (kk > thr_b) | tie_ok
                bias.append(jnp.where(sel, 0.0, NEG_BIG))
            off = off + jnp.sum(eqf.astype(jnp.float32), axis=-1, keepdims=True)
        off_ref[...] = off

        for j in range(A_KV_HEADS):
            logits = lax.dot_general(qs_ref[j], akc, (((1,), (1,)), ((), ())),
                                     preferred_element_type=jnp.float32)
            alphas = []
            for g in range(group):
                h = j * group + g
                lg = [logits[g * tq:(g + 1) * tq, s * LANES:(s + 1) * LANES] + bias[s]
                      for s in range(nslab)]
                m_old = m_ref[h]
                m_cur = functools.reduce(jnp.maximum, lg)
                m_new = jnp.maximum(m_old, jnp.max(m_cur, axis=-1, keepdims=True))
                alpha = jnp.exp(m_old - m_new)
                psum = jnp.zeros((tq, LANES), jnp.float32)
                for s in range(nslab):
                    p = jnp.exp(lg[s] - m_new)
                    psum = psum + p
                    p_ref[g * tq:(g + 1) * tq, s * LANES:(s + 1) * LANES] = p.astype(jnp.bfloat16)
                l_ref[h] = alpha * l_ref[h] + jnp.sum(psum, axis=-1, keepdims=True)
                m_ref[h] = m_new
                alphas.append(alpha)
            pv = jnp.dot(p_ref[...], avc, preferred_element_type=jnp.float32)
            for g in range(group):
                h = j * group + g
                acc_ref[h] = alphas[g] * acc_ref[h] + pv[g * tq:(g + 1) * tq]
        return carry

    lax.fori_loop(0, nchunks, attn_chunk, 0)

    lane = lax.broadcasted_iota(jnp.int32, (tq, LANES), 1)
    for h in range(A_HEADS):
        j = h // group
        own = (lane >= j * HEAD_DIM) & (lane < (j + 1) * HEAD_DIM)
        out = acc_ref[h] / l_ref[h]
        o_ref[0, :, h * LANES:(h + 1) * LANES] = jnp.where(own, out, 0.0).astype(o_ref.dtype)


def _dsa(p3, tri):
    b, s, _ = p3.shape
    tq = min(256, s)
    kc = min(512, s)
    topk = min(TOPK_MAX, s // 4)
    kern = functools.partial(_dsa_kernel, tq=tq, kc=kc, topk=topk, search_rows=min(128, tq))
    wide = A_HEADS * LANES
    return pl.pallas_call(
        kern,
        grid=(b, s // tq),
        in_specs=[
            pl.BlockSpec((1, tq, wide), lambda bi, i: (bi, i, T_AQ)),
            pl.BlockSpec((1, tq, PROJ_TILE), lambda bi, i: (bi, i, T_IQ)),
            pl.BlockSpec((1, tq, LANES), lambda bi, i: (bi, i, SLAB_IW)),
            pl.BlockSpec((1, s, LANES), lambda bi, i: (bi, 0, SLAB_IK)),
            pl.BlockSpec((1, s, LANES), lambda bi, i: (bi, 0, SLAB_AK)),
            pl.BlockSpec((1, s, LANES), lambda bi, i: (bi, 0, SLAB_AV)),
            pl.BlockSpec(tri.shape, lambda bi, i: (0, 0)),
        ],
        out_specs=pl.BlockSpec((1, tq, wide), lambda bi, i: (bi, i, 0)),
        out_shape=jax.ShapeDtypeStruct((b, s, wide), jnp.bfloat16),
        scratch_shapes=[
            pltpu.VMEM((s // kc, tq, kc), jnp.int32),
            pltpu.VMEM((A_KV_HEADS, 4 * tq, LANES), jnp.bfloat16),
            pltpu.VMEM((tq, LANES), jnp.int32),
            pltpu.VMEM((tq, LANES), jnp.float32),
            pltpu.VMEM((tq, LANES), jnp.float32),
            pltpu.VMEM((A_HEADS, tq, LANES), jnp.float32),
            pltpu.VMEM((A_HEADS, tq, LANES), jnp.float32),
            pltpu.VMEM((A_HEADS, tq, LANES), jnp.float32),
            pltpu.VMEM((4 * tq, kc), jnp.bfloat16),
        ],
        compiler_params=_cparams(("parallel", "arbitrary")),
        name="dsa",
    )(p3, p3, p3, p3, p3, p3, tri)


def _dil_kernel(q_ref, kp_ref, kc_ref, vp_ref, vc_ref, o_ref, lse_ref, *, tu):
    u = pl.program_id(2)
    qq = lax.broadcasted_iota(jnp.int32, (tu, 2 * tu), 0)
    kk = lax.broadcasted_iota(jnp.int32, (tu, 2 * tu), 1)
    first_key = jnp.where(u > 0, 0, tu)
    ok = (kk >= qq) & (kk <= qq + tu) & (kk >= first_key)
    lane = lax.broadcasted_iota(jnp.int32, (tu, LANES), 1)
    low = lane < HEAD_DIM
    lse_tile = jnp.zeros((tu, LANES), jnp.float32)
    for pr in range(B_HEADS // 2):
        sl = slice(pr * LANES, (pr + 1) * LANES)
        q = q_ref[0, :, sl]
        k = jnp.concatenate([kp_ref[0, :, sl], kc_ref[0, :, sl]], axis=0)
        v = jnp.concatenate([vp_ref[0, :, sl], vc_ref[0, :, sl]], axis=0)
        outs = []
        for par in range(2):
            mine = low if par == 0 else jnp.logical_not(low)
            qm = jnp.where(mine, q, jnp.zeros_like(q))
            lg = lax.dot_general(qm, k, (((1,), (1,)), ((), ())), preferred_element_type=jnp.float32)
            lg = jnp.where(ok, lg, NEG_BIG)
            m = jnp.max(lg, axis=-1, keepdims=True)
            p = jnp.exp(lg - m)
            l = jnp.sum(p, axis=-1, keepdims=True)
            o = jnp.dot(p.astype(jnp.bfloat16), v, preferred_element_type=jnp.float32) / l
            outs.append(o)
            lse_tile = jnp.where(lane == 2 * pr + par, m + jnp.log(l), lse_tile)
        o_ref[0, :, sl] = jnp.where(low, outs[0], outs[1]).astype(o_ref.dtype)
    lse_ref[0] = lse_tile


def _dilated(p3, gi):
    b, s, pc = p3.shape
    window, dil = B_PATTERNS[gi]
    tu = window // dil
    su = s // dil
    nt = pc // PROJ_TILE
    pv = p3.reshape(b, su, dil * pc)
    width = B_HEADS * HEAD_DIM
    qmap = lambda bi, r, u: (bi, u, r * nt + T_BQ + gi)
    kcur = lambda bi, r, u: (bi, u, r * nt + T_BK + gi)
    kprev = lambda bi, r, u: (bi, jnp.maximum(u - 1, 0), r * nt + T_BK + gi)
    vcur = lambda bi, r, u: (bi, u, r * nt + T_BV + gi)
    vprev = lambda bi, r, u: (bi, jnp.maximum(u - 1, 0), r * nt + T_BV + gi)
    blk = (1, tu, width)
    out, lse = pl.pallas_call(
        functools.partial(_dil_kernel, tu=tu),
        grid=(b, dil, su // tu),
        in_specs=[pl.BlockSpec(blk, qmap), pl.BlockSpec(blk, kprev), pl.BlockSpec(blk, kcur),
                  pl.BlockSpec(blk, vprev), pl.BlockSpec(blk, vcur)],
        out_specs=[pl.BlockSpec(blk, lambda bi, r, u: (bi, u, r)),
                   pl.BlockSpec((1, tu, LANES), lambda bi, r, u: (bi, u, r))],
        out_shape=[jax.ShapeDtypeStruct((b, su, dil * width), jnp.bfloat16),
                   jax.ShapeDtypeStruct((b, su, dil * LANES), jnp.float32)],
        compiler_params=_cparams(("parallel", "parallel", "arbitrary")),
        name=f"dilated{gi}",
    )(pv, pv, pv, pv, pv)
    return out.reshape(b * s, width), lse.reshape(b * s, LANES)


def _mem_attn_kernel(q_ref, k_ref, v_ref, o_ref):
    scale = M_HEAD_DIM ** -0.5
    for h in range(M_HEADS):
        sl = slice(h * M_HEAD_DIM, (h + 1) * M_HEAD_DIM)
        lg = lax.dot_general(q_ref[0, :, sl], k_ref[0, :, sl], (((1,), (1,)), ((), ())),
                             preferred_element_type=jnp.float32) * scale
        m = jnp.max(lg, axis=-1, keepdims=True)
        p = jnp.exp(lg - m)
        l = jnp.sum(p, axis=-1, keepdims=True)
        o = jnp.dot(p.astype(jnp.bfloat16), v_ref[0, :, sl], preferred_element_type=jnp.float32)
        o_ref[0, :, sl] = (o / l).astype(o_ref.dtype)


def _mem_attn(p3, kv):
    b, s, _ = p3.shape
    m = kv.shape[1]
    width = M_HEADS * M_HEAD_DIM
    tq = min(512, s)
    return pl.pallas_call(
        _mem_attn_kernel,
        grid=(b, s // tq),
        in_specs=[
            pl.BlockSpec((1, tq, width), lambda bi, i: (bi, i, T_MQ)),
            pl.BlockSpec((1, m, width), lambda bi, i: (bi, 0, 0)),
            pl.BlockSpec((1, m, width), lambda bi, i: (bi, 0, 1)),
        ],
        out_specs=pl.BlockSpec((1, tq, width), lambda bi, i: (bi, i, 0)),
        out_shape=jax.ShapeDtypeStruct((b, s, width), jnp.bfloat16),
        compiler_params=_cparams(("parallel", "parallel")),
        name="mem_attn",
    )(p3, kv, kv)


def _split_bf16(v):
    hi = v.astype(jnp.bfloat16)
    lo = (v - hi.astype(jnp.float32)).astype(jnp.bfloat16)
    return hi, lo


def _merge_kernel(x_ref, gmix_ref, wg_ref, bg_ref, ya_ref, wa_ref, ob0_ref, ob1_ref, ob2_ref,
                  ls0_ref, ls1_ref, ls2_ref, exp_ref, wb_ref, ym_ref, wm_ref, wo_ref, gffn_ref,
                  wrh_ref, wrl_ref, h_ref, xn_ref, rl_ref):
    d = x_ref.shape[1]
    x = x_ref[...]
    n = _rms(x, gmix_ref[...]).astype(jnp.bfloat16)

    def gate(k):
        z = jnp.dot(n, wg_ref[:, k * d:(k + 1) * d], preferred_element_type=jnp.float32)
        return jax.nn.sigmoid(z + bg_ref[:, k * d:(k + 1) * d])

    merged = gate(0) * jnp.dot(ya_ref[...], wa_ref[...], preferred_element_type=jnp.float32)

    ls = [ls0_ref[...], ls1_ref[...], ls2_ref[...]]
    mx = jnp.maximum(jnp.maximum(ls[0], ls[1]), ls[2])
    es = [jnp.exp(v - mx) for v in ls]
    inv = 1.0 / (es[0] + es[1] + es[2])
    yb = jnp.zeros((x.shape[0], ob0_ref.shape[1]), jnp.float32)
    for e, ob in zip(es, (ob0_ref, ob1_ref, ob2_ref)):
        hi, lo = _split_bf16(e * inv)
        a = (jnp.dot(hi, exp_ref[...], preferred_element_type=jnp.float32)
             + jnp.dot(lo, exp_ref[...], preferred_element_type=jnp.float32))
        yb = yb + a * ob[...].astype(jnp.float32)
    merged = merged + gate(1) * jnp.dot(yb.astype(jnp.bfloat16), wb_ref[...],
                                        preferred_element_type=jnp.float32)
    merged = merged + gate(2) * jnp.dot(ym_ref[...], wm_ref[...], preferred_element_type=jnp.float32)

    h = x + jnp.dot(merged.astype(jnp.bfloat16), wo_ref[...], preferred_element_type=jnp.float32)
    h_ref[...] = h
    xn = _rms(h, gffn_ref[...])
    xn_ref[...] = xn.astype(jnp.bfloat16)
    hi, lo = _split_bf16(xn)
    rl_ref[...] = (jnp.dot(hi, wrh_ref[...], preferred_element_type=jnp.float32)
                   + jnp.dot(lo, wrh_ref[...], preferred_element_type=jnp.float32)
                   + jnp.dot(hi, wrl_ref[...], preferred_element_type=jnp.float32))


def _merge(x2, g_mix, wg, bg, ya, wa, obs, lss, expand, wb, ym, wm, wo, g_ffn, wrh, wrl):
    t, d = x2.shape
    tm = min(256, t)
    row = lambda w: pl.BlockSpec((tm, w), lambda i: (i, 0))
    full = lambda a: pl.BlockSpec(a.shape, lambda i: (0, 0))
    return pl.pallas_call(
        _merge_kernel,
        grid=(t // tm,),
        in_specs=[row(d), full(g_mix), full(wg), full(bg), row(ya.shape[1]), full(wa),
                  row(obs[0].shape[1]), row(obs[1].shape[1]), row(obs[2].shape[1]),
                  row(LANES), row(LANES), row(LANES), full(expand), full(wb),
                  row(ym.shape[1]), full(wm), full(wo), full(g_ffn), full(wrh), full(wrl)],
        out_specs=[row(d), row(d), row(LANES)],
        out_shape=[jax.ShapeDtypeStruct((t, d), jnp.float32),
                   jax.ShapeDtypeStruct((t, d), jnp.bfloat16),
                   jax.ShapeDtypeStruct((t, LANES), jnp.float32)],
        compiler_params=_cparams(("parallel",)),
        name="merge",
    )(x2, g_mix, wg, bg, ya, wa, obs[0], obs[1], obs[2], lss[0], lss[1], lss[2], expand, wb,
      ym, wm, wo, g_ffn, wrh, wrl)


def _first_lane_where(cond, lane):
    return jnp.min(jnp.where(cond, lane, float(LANES)), axis=-1, keepdims=True)


def _route(logits, rb):
    z = logits + rb
    lane = lax.broadcasted_iota(jnp.int32, z.shape, 1).astype(jnp.float32)
    is_g = lane < MOE_GROUPS
    zg = jnp.where(is_g, z, -jnp.inf)
    mg = jnp.max(zg, axis=-1, keepdims=True)
    eg = jnp.exp(zg - mg)
    gp = eg / jnp.sum(eg, axis=-1, keepdims=True)
    g_w = jnp.max(gp, axis=-1, keepdims=True)
    g_sel = _first_lane_where(is_g & (gp == g_w), lane)
    lo = R_SUB0 + g_sel * EXPERTS_PER_GROUP
    in_grp = (lane >= lo) & (lane < lo + EXPERTS_PER_GROUP)
    zs = jnp.where(in_grp, z, -jnp.inf)
    ms = jnp.max(zs, axis=-1, keepdims=True)
    es = jnp.exp(zs - ms)
    sp = es / jnp.sum(es, axis=-1, keepdims=True)
    p1 = jnp.max(sp, axis=-1, keepdims=True)
    i1 = _first_lane_where(in_grp & (sp == p1), lane)
    rest = in_grp & (lane != i1)
    sp2 = jnp.where(rest, sp, -1.0)
    p2 = jnp.max(sp2, axis=-1, keepdims=True)
    i2 = _first_lane_where(rest & (sp2 == p2), lane)
    tot = p1 + p2
    return jnp.where(lane == i1, g_w * (p1 / tot), jnp.where(lane == i2, g_w * (p2 / tot), 0.0))


def _moe_kernel(xn_ref, rl_ref, rb_ref, h_ref, w1_ref, w3_ref, w2_ref, gf_ref, o_ref,
                comb_ref, acc_ref):
    e = pl.program_id(1)

    @pl.when(e == 0)
    def _():
        comb_ref[...] = _route(rl_ref[...], rb_ref[...])
        acc_ref[...] = jnp.zeros_like(acc_ref)

    x = xn_ref[...]
    a = jnp.dot(x, w1_ref[0], preferred_element_type=jnp.float32)
    g = jnp.dot(x, w3_ref[0], preferred_element_type=jnp.float32)
    lane = lax.broadcasted_iota(jnp.int32, comb_ref.shape, 1)
    c = jnp.sum(jnp.where(lane == R_SUB0 + e, comb_ref[...], 0.0), axis=-1, keepdims=True)
    hid = (jax.nn.silu(a) * g).astype(jnp.bfloat16)
    acc_ref[...] += c * jnp.dot(hid, w2_ref[0], preferred_element_type=jnp.float32)

    @pl.when(e == N_EXPERTS - 1)
    def _():
        o_ref[...] = _rms(h_ref[...] + acc_ref[...], gf_ref[...])


def _moe(xn, rl, rb, h, w1, w3, w2, g_final):
    t, d = xn.shape
    hid = w1.shape[2]
    tm = min(1024, t)
    return pl.pallas_call(
        _moe_kernel,
        grid=(t // tm, N_EXPERTS),
        in_specs=[
            pl.BlockSpec((tm, d), lambda i, e: (i, 0)),
            pl.BlockSpec((tm, LANES), lambda i, e: (i, 0)),
            pl.BlockSpec((1, LANES), lambda i, e: (0, 0)),
            pl.BlockSpec((tm, d), lambda i, e: (i, 0)),
            pl.BlockSpec((1, d, hid), lambda i, e: (e, 0, 0)),
            pl.BlockSpec((1, d, hid), lambda i, e: (e, 0, 0)),
            pl.BlockSpec((1, hid, d), lambda i, e: (e, 0, 0)),
            pl.BlockSpec((1, d), lambda i, e: (0, 0)),
        ],
        out_specs=pl.BlockSpec((tm, d), lambda i, e: (i, 0)),
        out_shape=jax.ShapeDtypeStruct((t, d), jnp.float32),
        scratch_shapes=[pltpu.VMEM((tm, LANES), jnp.float32), pltpu.VMEM((tm, d), jnp.float32)],
        compiler_params=_cparams(("parallel", "arbitrary")),
        name="moe",
    )(xn, rl, rb, h, w1, w3, w2, g_final)


def _pad_heads(w, n_heads, offset):
    d = w.shape[0]
    w = w.reshape(d, n_heads, HEAD_DIM)
    out = jnp.zeros((d, n_heads, LANES), w.dtype)
    for h in range(n_heads):
        out = out.at[:, h, offset(h):offset(h) + HEAD_DIM].set(w[:, h])
    return out.reshape(d, n_heads * LANES)


def _pack_w_in(w_in):
    d = w_in.shape[0]
    parts, off = [], 0
    for n in IN_SPLITS:
        parts.append(w_in[:, off:off + n])
        off += n
    aq, ak, av, iq, ik, iw, bq, bk, bv, mq = parts
    qscale = HEAD_DIM ** -0.5
    group = A_HEADS // A_KV_HEADS
    aq_x = _pad_heads(aq * qscale, A_HEADS, lambda h: (h // group) * HEAD_DIM)
    iq_x = _pad_heads(iq * qscale, IDX_HEADS, lambda h: 0)
    ik_x = jnp.pad(ik, ((0, 0), (0, LANES - ik.shape[1])))
    iw2 = jnp.concatenate([iw, iw], axis=1) * (IDX_HEADS ** -0.5)
    iw_x = jnp.pad(iw2, ((0, 0), (0, LANES - iw2.shape[1])))
    return jnp.concatenate([aq_x, iq_x, bq * qscale, bk, ak, ik_x, av, iw_x, bv, mq], axis=1)


def _rope_tables(seq):
    half = HEAD_DIM // 2
    inv = ROPE_THETA ** (-jnp.arange(half, dtype=jnp.float32) / half)
    ang = jnp.arange(seq, dtype=jnp.float32)[:, None] * inv[None, :]
    cos = jnp.tile(jnp.cos(ang), (1, LANES // half))
    sign = jnp.tile(jnp.concatenate([-jnp.ones((half,), jnp.float32), jnp.ones((half,), jnp.float32)]),
                    LANES // HEAD_DIM)
    sin = jnp.tile(jnp.sin(ang), (1, LANES // half)) * sign[None, :]
    return cos, sin


def _layer(h3, mem, g_mix, g_mem, w_in, w_mem_kv, w_gate, b_gate, w_branch, w_out, g_ffn,
           w_group, b_group, w_sub, b_sub, w1, w3, w2, g_out):
    b, s, d = h3.shape
    bf = jnp.bfloat16
    x2 = h3.reshape(b * s, d)
    cos_t, sin_t = _rope_tables(s)

    p = _in_proj(x2, g_mix[None, :], _pack_w_in(w_in).astype(bf), cos_t, sin_t, s)
    p3 = p.reshape(b, s, P_COLS)

    kv = _mem_kv(mem, g_mem[None, :], w_mem_kv.astype(bf))

    half = 256
    tri = (jnp.arange(half)[:, None] <= jnp.arange(half)[None, :]).astype(bf)
    ya = _dsa(p3, tri).reshape(b * s, A_HEADS * LANES)

    obs, lss = [], []
    for gi in range(len(B_PATTERNS)):
        o, l = _dilated(p3, gi)
        obs.append(o)
        lss.append(l)

    ym = _mem_attn(p3, kv).reshape(b * s, M_HEADS * M_HEAD_DIM)

    group = A_HEADS // A_KV_HEADS
    wa = _pad_heads(w_branch[0].T, A_HEADS, lambda h: (h // group) * HEAD_DIM).T.astype(bf)
    expand = (jnp.arange(LANES)[:, None] == (jnp.arange(B_HEADS * HEAD_DIM)[None, :] // HEAD_DIM)).astype(bf)
    w_route = jnp.concatenate([w_group, jnp.moveaxis(w_sub, 0, 1).reshape(d, N_EXPERTS)], axis=1)
    w_route = jnp.pad(w_route, ((0, 0), (0, LANES - w_route.shape[1])))
    wrh = w_route.astype(bf)
    wrl = (w_route - wrh.astype(jnp.float32)).astype(bf)
    r_bias = jnp.pad(jnp.concatenate([b_group, b_sub.reshape(-1)]), (0, LANES - MOE_GROUPS - N_EXPERTS))

    h, xn, rl = _merge(x2, g_mix[None, :], w_gate.astype(bf), b_gate[None, :], ya, wa, obs, lss, expand,
                       w_branch[1].astype(bf), ym, w_branch[2].astype(bf), w_out.astype(bf),
                       g_ffn[None, :], wrh, wrl)
    out = _moe(xn, rl, r_bias[None, :], h, w1.astype(bf), w3.astype(bf), w2.astype(bf), g_out[None, :])
    return out.reshape(b, s, d)


def kernel(x, mem, g_mix, g_mem, w_in, w_mem_kv, w_gate, b_gate, w_branch, w_out, g_ffn,
           w_group, b_group, w_sub, b_sub, w1, w3, w2, g_final):
    depth = g_mix.shape[0]
    assert depth == 1, "the final rmsnorm is fused into the single layer's last kernel"
    return _layer(x, mem, g_mix[0], g_mem[0], w_in[0], w_mem_kv[0], w_gate[0], b_gate[0], w_branch[0],
                  w_out[0], g_ffn[0], w_group[0], b_group[0], w_sub[0], b_sub[0], w1[0], w3[0], w2[0],
                  g_final)
```

```python
import functools
import math

import jax
import jax.numpy as jnp
from jax import lax
from jax.experimental import pallas as pl
from jax.experimental.pallas import tpu as pltpu

HEAD_DIM = 64
ROPE_THETA = 10000.0
RMS_EPS = 1e-6
A_HEADS = 8
A_KV_HEADS = 2
IDX_HEADS = 4
TOPK_MAX = 256
B_PATTERNS = ((128, 1), (512, 4), (2048, 16))
B_HEADS = 8
M_HEADS = 4
M_HEAD_DIM = 128
MOE_GROUPS = 4
EXPERTS_PER_GROUP = 4
N_EXPERTS = 16
IN_SPLITS = (512, 128, 128, 256, 64, 4, 1536, 1536, 1536, 512)

LANES = 128
PROJ_TILE = 512
SLABS = PROJ_TILE // LANES
VMEM_LIMIT = 56 * 1024 * 1024

T_AQ, T_IQ, T_BQ, T_BK, T_MIX, T_BV, T_MQ = 0, 2, 3, 4, 5, 6, 7
P_MODES = (("rope",) * 4,) * 5 + (("rope", "rope", "plain", "hi_lo"),) + (("plain",) * 4,) * 2
SLAB_AK = T_MIX * SLABS + 0
SLAB_IK = T_MIX * SLABS + 1
SLAB_AV = T_MIX * SLABS + 2
SLAB_IW = T_MIX * SLABS + 3
D_MODES = (("rope",) * 4, ("rope",) * 4, ("plain",) * 4)

INT_MIN = -(2 ** 31)
NEG_BIG = -1e30

R_SUB0 = MOE_GROUPS


def _cparams(sem):
    return pltpu.CompilerParams(dimension_semantics=sem, vmem_limit_bytes=VMEM_LIMIT)


def _rms(xf, g):
    return xf * lax.rsqrt(jnp.mean(xf * xf, axis=-1, keepdims=True) + RMS_EPS) * g


def _rope_slab(y, cos, sin_signed, first_half):
    partner = jnp.where(first_half, pltpu.roll(y, 96, 1), pltpu.roll(y, 32, 1))
    return y * cos + partner * sin_signed


def _in_proj_kernel(x_ref, g_ref, w_ref, cos_ref, sin_ref, o_ref, n_ref, *stage, modes, dil, rows):
    j = pl.program_id(2)
    tm = x_ref.shape[1]

    @pl.when(j == 0)
    def _():
        n_ref[...] = _rms(x_ref[0], g_ref[...]).astype(jnp.bfloat16)

    lane = lax.broadcasted_iota(jnp.int32, (rows, LANES), 1)
    first_half = (lane & (HEAD_DIM - 1)) < (HEAD_DIM // 2)

    def epilogue(slab_modes):
        for r in range(tm // rows):
            rs = pl.ds(r * rows, rows)
            acc = jnp.dot(n_ref[rs, :], w_ref[...], preferred_element_type=jnp.float32)
            cos = cos_ref[rs, :]
            sin = sin_ref[rs, :]
            for s in range(SLABS):
                y = acc[:, s * LANES:(s + 1) * LANES]
                if slab_modes[s] == "rope":
                    y = _rope_slab(y, cos, sin, first_half)
                elif slab_modes[s] == "hi_lo":
                    resid = y - y.astype(jnp.bfloat16).astype(jnp.float32)
                    y = jnp.where(lane < IDX_HEADS, y, resid)
                if dil == 1:
                    o_ref[0, 0, rs, s * LANES:(s + 1) * LANES] = y.astype(o_ref.dtype)
                else:
                    stage[0][s, rs, :] = y
        if dil > 1:
            for r in range(dil):
                for s in range(SLABS):
                    o_ref[0, r, :, s * LANES:(s + 1) * LANES] = (
                        stage[0][s, pl.ds(r, tm // dil, stride=dil), :].astype(o_ref.dtype))

    for pattern in sorted(set(modes)):
        tiles = [t for t, m in enumerate(modes) if m == pattern]
        cond = functools.reduce(jnp.logical_or, [j == t for t in tiles])
        pl.when(cond)(functools.partial(epilogue, pattern))


def _in_proj(x3, g_mix, wp, cos_t, sin_t, modes, dil):
    b, s, d = x3.shape
    tm = min(1024, s)
    ntiles = len(modes)
    scratch = [pltpu.VMEM((tm, d), jnp.bfloat16)]
    if dil > 1:
        scratch.append(pltpu.VMEM((SLABS, tm, LANES), jnp.float32))
    return pl.pallas_call(
        functools.partial(_in_proj_kernel, modes=modes, dil=dil, rows=256),
        grid=(b, s // tm, ntiles),
        in_specs=[
            pl.BlockSpec((1, tm, d), lambda bi, i, j: (bi, i, 0)),
            pl.BlockSpec((1, d), lambda bi, i, j: (0, 0)),
            pl.BlockSpec((d, PROJ_TILE), lambda bi, i, j: (0, j)),
            pl.BlockSpec((tm, LANES), lambda bi, i, j: (i, 0)),
            pl.BlockSpec((tm, LANES), lambda bi, i, j: (i, 0)),
        ],
        out_specs=pl.BlockSpec((1, dil, tm // dil, PROJ_TILE), lambda bi, i, j: (bi, 0, i, j)),
        out_shape=jax.ShapeDtypeStruct((b, dil, s // dil, ntiles * PROJ_TILE), jnp.bfloat16),
        scratch_shapes=scratch,
        compiler_params=_cparams(("parallel", "parallel", "arbitrary")),
        name=f"in_proj_d{dil}",
    )(x3, g_mix, wp, cos_t, sin_t)


def _mem_kv_kernel(m_ref, g_ref, w_ref, o_ref):
    n = _rms(m_ref[0], g_ref[...]).astype(jnp.bfloat16)
    o_ref[0] = jnp.dot(n, w_ref[...], preferred_element_type=jnp.float32).astype(o_ref.dtype)


def _mem_kv(mem, g_mem, w_kv):
    b, m, d = mem.shape
    n = w_kv.shape[1]
    return pl.pallas_call(
        _mem_kv_kernel,
        grid=(b,),
        in_specs=[
            pl.BlockSpec((1, m, d), lambda i: (i, 0, 0)),
            pl.BlockSpec((1, d), lambda i: (0, 0)),
            pl.BlockSpec((d, n), lambda i: (0, 0)),
        ],
        out_specs=pl.BlockSpec((1, m, n), lambda i: (i, 0, 0)),
        out_shape=jax.ShapeDtypeStruct((b, m, n), jnp.bfloat16),
        compiler_params=_cparams(("parallel",)),
        name="mem_kv",
    )(mem, g_mem, w_kv)


def _key_to_float(key):
    bits = jnp.where(key < 0, key ^ jnp.int32(0x7FFFFFFF), key)
    f = lax.bitcast_convert_type(bits, jnp.float32)
    return jnp.where((key < 0) & (f != f), -jnp.inf, f)


def _dsa_kernel(aq_ref, iq_ref, iw_ref, ik_ref, ak_ref, av_ref, tri_ref, o_ref,
                sc_ref, lists_ref, qs_ref, thr_ref, need_ref, off_ref, m_ref, l_ref, acc_ref, p_ref,
                *, tq, kc, topk, depth, ins_rows, bis_rows):
    i = pl.program_id(1)
    q0 = i * tq
    nchunks = (q0 + tq + kc - 1) // kc
    nslab = kc // LANES
    group = A_HEADS // A_KV_HEADS

    iw = iw_ref[0].astype(jnp.float32)
    w_heads = [iw[:, h:h + 1] + iw[:, IDX_HEADS + h:IDX_HEADS + h + 1] for h in range(IDX_HEADS)]
    qpos = q0 + lax.broadcasted_iota(jnp.int32, (tq, kc), 0)
    kiota = lax.broadcasted_iota(jnp.int32, (tq, kc), 1)

    def score_chunk(c, carry):
        k0 = pl.multiple_of(c * kc, kc)
        ikc = ik_ref[0, pl.ds(k0, kc), :]
        s = jnp.zeros((tq, kc), jnp.float32)
        for h in range(IDX_HEADS):
            d = lax.dot_general(iq_ref[0, :, h * LANES:(h + 1) * LANES], ikc,
                                (((1,), (1,)), ((), ())), preferred_element_type=jnp.float32)
            s = s + jnp.maximum(d, 0.0) * w_heads[h]
        sc_ref[c] = jnp.where(kiota + k0 <= qpos, s, -jnp.inf)
        return carry

    lax.fori_loop(0, nchunks, score_chunk, 0)

    def insert_rows(rp, carry):
        r0 = pl.multiple_of(rp * ins_rows, ins_rows)

        def chunk_body(c, tops):
            tops = list(tops)
            for s in range(nslab):
                x = sc_ref[c, pl.ds(r0, ins_rows), s * LANES:(s + 1) * LANES]
                for j in range(depth):
                    hi = jnp.maximum(tops[j], x)
                    x = jnp.minimum(tops[j], x)
                    tops[j] = hi
            return tuple(tops)

        empty = tuple(jnp.full((ins_rows, LANES), -jnp.inf, jnp.float32) for _ in range(depth))
        tops = lax.fori_loop(0, nchunks, chunk_body, empty)
        for j in range(depth):
            lists_ref[pl.ds(r0, ins_rows), j * LANES:(j + 1) * LANES] = tops[j]
        return carry

    lax.fori_loop(0, tq // ins_rows, insert_rows, 0)

    def bisect(count_ge):
        def bit_step(it, prefix):
            cand = prefix + lax.shift_left(jnp.int32(1), 31 - it)
            return jnp.where(count_ge(_key_to_float(cand)) >= float(topk), cand, prefix)
        return bit_step

    def finish(rs, thr, c_gt):
        thr_ref[rs, :] = thr
        need_ref[rs, :] = jnp.where(thr == -jnp.inf, 0.0, float(topk) - c_gt)

    def list_count(rs, cand, strict):
        acc = jnp.zeros((bis_rows, LANES), jnp.float32)
        for j in range(depth):
            v = lists_ref[rs, j * LANES:(j + 1) * LANES]
            acc = acc + jnp.where((v > cand) if strict else (v >= cand), 1.0, 0.0)
        return jnp.sum(acc, axis=-1, keepdims=True)

    nblk = tq // bis_rows
    blocks = [pl.ds(rb * bis_rows, bis_rows) for rb in range(nblk)]

    def list_bit_step(it, prefixes):
        return tuple(bisect(functools.partial(list_count, rs, strict=False))(it, pre)
                     for rs, pre in zip(blocks, prefixes))

    start = tuple(jnp.full((bis_rows, LANES), INT_MIN, jnp.int32) for _ in range(nblk))
    keys = lax.fori_loop(0, 32, list_bit_step, start)
    overflow = jnp.zeros((bis_rows, LANES), jnp.float32)
    for rs, key in zip(blocks, keys):
        thr = _key_to_float(key)
        finish(rs, thr, list_count(rs, thr, True))
        last = lists_ref[rs, (depth - 1) * LANES:depth * LANES]
        overflow = jnp.maximum(overflow, jnp.where(last > thr, 1.0, 0.0))

    @pl.when(jnp.max(overflow) > 0.0)
    def _():
        def full_count(rs, cand, strict):
            def body(c, acc):
                for s in range(nslab):
                    v = sc_ref[c, rs, s * LANES:(s + 1) * LANES]
                    acc = acc + jnp.where((v > cand) if strict else (v >= cand), 1.0, 0.0)
                return acc
            acc = lax.fori_loop(0, nchunks, body, jnp.zeros((bis_rows, LANES), jnp.float32))
            return jnp.sum(acc, axis=-1, keepdims=True)

        for rs in blocks:
            key = lax.fori_loop(0, 32, bisect(functools.partial(full_count, rs, strict=False)),
                                jnp.full((bis_rows, LANES), INT_MIN, jnp.int32))
            thr = _key_to_float(key)
            finish(rs, thr, full_count(rs, thr, True))

    for j in range(A_KV_HEADS):
        for g in range(group):
            h = j * group + g
            qs_ref[j, g * tq:(g + 1) * tq, :] = aq_ref[0, :, h * LANES:(h + 1) * LANES]
    off_ref[...] = jnp.zeros_like(off_ref)
    m_ref[...] = jnp.full(m_ref.shape, NEG_BIG, jnp.float32)
    l_ref[...] = jnp.zeros_like(l_ref)
    acc_ref[...] = jnp.zeros_like(acc_ref)
    half = tri_ref.shape[0]

    def attn_chunk(c, carry):
        k0 = pl.multiple_of(c * kc, kc)
        akc = ak_ref[0, pl.ds(k0, kc), :]
        avc = av_ref[0, pl.ds(k0, kc), :]
        thr_b = thr_ref[...]
        need_b = need_ref[...]
        bias = []
        off = off_ref[...]
        for hh in range(kc // half):
            eq_parts = []
            for s in range(half // LANES):
                v = sc_ref[c, :, hh * half + s * LANES: hh * half + (s + 1) * LANES]
                eq_parts.append(jnp.where(v == thr_b, 1.0, 0.0).astype(jnp.bfloat16))
            eqf = jnp.concatenate(eq_parts, axis=1)
            pre = jnp.dot(eqf, tri_ref[...], preferred_element_type=jnp.float32)
            for s in range(half // LANES):
                v = sc_ref[c, :, hh * half + s * LANES: hh * half + (s + 1) * LANES]
                rank = pre[:, s * LANES:(s + 1) * LANES] + off
                tie_ok = (v == thr_b) & (rank <= need_b)
                sel = (v > thr_b) | tie_ok
                bias.append(jnp.where(sel, 0.0, NEG_BIG))
            off = off + jnp.sum(eqf.astype(jnp.float32), axis=-1, keepdims=True)
        off_ref[...] = off

        for j in range(A_KV_HEADS):
            logits = lax.dot_general(qs_ref[j], akc, (((1,), (1,)), ((), ())),
                                     preferred_element_type=jnp.float32)
            alphas = []
            for g in range(group):
                h = j * group + g
                lg = [logits[g * tq:(g + 1) * tq, s * LANES:(s + 1) * LANES] + bias[s]
                      for s in range(nslab)]
                m_old = m_ref[h]
                m_cur = functools.reduce(jnp.maximum, lg)
                m_new = jnp.maximum(m_old, jnp.max(m_cur, axis=-1, keepdims=True))
                alpha = jnp.exp2(m_old - m_new)
                psum = jnp.zeros((tq, LANES), jnp.float32)
                for s in range(nslab):
                    p = jnp.exp2(lg[s] - m_new)
                    psum = psum + p
                    p_ref[g * tq:(g + 1) * tq, s * LANES:(s + 1) * LANES] = p.astype(jnp.bfloat16)
                l_ref[h] = alpha * l_ref[h] + jnp.sum(psum, axis=-1, keepdims=True)
                m_ref[h] = m_new
                alphas.append(alpha)
            pv = jnp.dot(p_ref[...], avc, preferred_element_type=jnp.float32)
            for g in range(group):
                h = j * group + g
                acc_ref[h] = alphas[g] * acc_ref[h] + pv[g * tq:(g + 1) * tq]
        return carry

    lax.fori_loop(0, nchunks, attn_chunk, 0)

    lane = lax.broadcasted_iota(jnp.int32, (tq, LANES), 1)
    for h in range(A_HEADS):
        j = h // group
        own = (lane >= j * HEAD_DIM) & (lane < (j + 1) * HEAD_DIM)
        out = acc_ref[h] / l_ref[h]
        o_ref[0, :, h * LANES:(h + 1) * LANES] = jnp.where(own, out, 0.0).astype(o_ref.dtype)


def _dsa(p3, tri):
    b, s, _ = p3.shape
    tq = min(256, s)
    kc = min(512, s)
    topk = min(TOPK_MAX, s // 4)
    depth = 12
    kern = functools.partial(_dsa_kernel, tq=tq, kc=kc, topk=topk, depth=depth,
                             ins_rows=16, bis_rows=min(64, tq))
    wide = A_HEADS * LANES
    return pl.pallas_call(
        kern,
        grid=(b, s // tq),
        in_specs=[
            pl.BlockSpec((1, tq, wide), lambda bi, i: (bi, i, T_AQ)),
            pl.BlockSpec((1, tq, PROJ_TILE), lambda bi, i: (bi, i, T_IQ)),
            pl.BlockSpec((1, tq, LANES), lambda bi, i: (bi, i, SLAB_IW)),
            pl.BlockSpec((1, s, LANES), lambda bi, i: (bi, 0, SLAB_IK)),
            pl.BlockSpec((1, s, LANES), lambda bi, i: (bi, 0, SLAB_AK)),
            pl.BlockSpec((1, s, LANES), lambda bi, i: (bi, 0, SLAB_AV)),
            pl.BlockSpec(tri.shape, lambda bi, i: (0, 0)),
        ],
        out_specs=pl.BlockSpec((1, tq, wide), lambda bi, i: (bi, i, 0)),
        out_shape=jax.ShapeDtypeStruct((b, s, wide), jnp.bfloat16),
        scratch_shapes=[
            pltpu.VMEM((s // kc, tq, kc), jnp.float32),
            pltpu.VMEM((tq, depth * LANES), jnp.float32),
            pltpu.VMEM((A_KV_HEADS, 4 * tq, LANES), jnp.bfloat16),
            pltpu.VMEM((tq, LANES), jnp.float32),
            pltpu.VMEM((tq, LANES), jnp.float32),
            pltpu.VMEM((tq, LANES), jnp.float32),
            pltpu.VMEM((A_HEADS, tq, LANES), jnp.float32),
            pltpu.VMEM((A_HEADS, tq, LANES), jnp.float32),
            pltpu.VMEM((A_HEADS, tq, LANES), jnp.float32),
            pltpu.VMEM((4 * tq, kc), jnp.bfloat16),
        ],
        compiler_params=_cparams(("parallel", "arbitrary")),
        name="dsa",
    )(p3, p3, p3, p3, p3, p3, tri)


def _dil_kernel(q_ref, kp_ref, kc_ref, vp_ref, vc_ref, o_ref, lse_ref, *, tu, w):
    u = pl.program_id(2)
    qq = lax.broadcasted_iota(jnp.int32, (tu, w + tu), 0)
    kk = lax.broadcasted_iota(jnp.int32, (tu, w + tu), 1)
    first_key = jnp.where(u > 0, 0, w)
    ok = (kk >= qq) & (kk <= qq + w) & (kk >= first_key)
    lane = lax.broadcasted_iota(jnp.int32, (tu, LANES), 1)
    low = lane < HEAD_DIM
    lse_tile = jnp.zeros((tu, LANES), jnp.float32)
    for pr in range(B_HEADS // 2):
        sl = slice(pr * LANES, (pr + 1) * LANES)
        q = q_ref[0, 0, :, sl]
        k = jnp.concatenate([kp_ref[0, 0, :, sl], kc_ref[0, 0, :, sl]], axis=0)
        v = jnp.concatenate([vp_ref[0, 0, :, sl], vc_ref[0, 0, :, sl]], axis=0)
        outs = []
        for par in range(2):
            mine = low if par == 0 else jnp.logical_not(low)
            qm = jnp.where(mine, q, jnp.zeros_like(q))
            lg = lax.dot_general(qm, k, (((1,), (1,)), ((), ())), preferred_element_type=jnp.float32)
            lg = jnp.where(ok, lg, NEG_BIG)
            m = jnp.max(lg, axis=-1, keepdims=True)
            p = jnp.exp(lg - m)
            l = jnp.sum(p, axis=-1, keepdims=True)
            o = jnp.dot(p.astype(jnp.bfloat16), v, preferred_element_type=jnp.float32) / l
            outs.append(o)
            lse_tile = jnp.where(lane == 2 * pr + par, m + jnp.log(l), lse_tile)
        o_ref[0, 0, :, sl] = jnp.where(low, outs[0], outs[1]).astype(o_ref.dtype)
    lse_ref[0, 0] = lse_tile


def _dilated(src, tq_tile, tk_tile, tv_tile, gi):
    b, dil, su, _ = src.shape
    window, d2 = B_PATTERNS[gi]
    assert d2 == dil
    w = window // dil
    tu = min(256, su)
    ratio = tu // w
    width = B_HEADS * HEAD_DIM
    cur = lambda t: pl.BlockSpec((1, 1, tu, width), lambda bi, r, u: (bi, r, u, t))
    prev = lambda t: pl.BlockSpec((1, 1, w, width),
                                  lambda bi, r, u: (bi, r, jnp.maximum(u * ratio - 1, 0), t))
    return pl.pallas_call(
        functools.partial(_dil_kernel, tu=tu, w=w),
        grid=(b, dil, su // tu),
        in_specs=[cur(tq_tile), prev(tk_tile), cur(tk_tile), prev(tv_tile), cur(tv_tile)],
        out_specs=[pl.BlockSpec((1, 1, tu, width), lambda bi, r, u: (bi, r, u, 0)),
                   pl.BlockSpec((1, 1, tu, LANES), lambda bi, r, u: (bi, r, u, 0))],
        out_shape=[jax.ShapeDtypeStruct((b, dil, su, width), jnp.bfloat16),
                   jax.ShapeDtypeStruct((b, dil, su, LANES), jnp.float32)],
        compiler_params=_cparams(("parallel", "parallel", "arbitrary")),
        name=f"dilated{gi}",
    )(src, src, src, src, src)


def _mem_attn_kernel(q_ref, k_ref, v_ref, o_ref):
    scale = M_HEAD_DIM ** -0.5
    for h in range(M_HEADS):
        sl = slice(h * M_HEAD_DIM, (h + 1) * M_HEAD_DIM)
        lg = lax.dot_general(q_ref[0, :, sl], k_ref[0, :, sl], (((1,), (1,)), ((), ())),
                             preferred_element_type=jnp.float32) * scale
        m = jnp.max(lg, axis=-1, keepdims=True)
        p = jnp.exp(lg - m)
        l = jnp.sum(p, axis=-1, keepdims=True)
        o = jnp.dot(p.astype(jnp.bfloat16), v_ref[0, :, sl], preferred_element_type=jnp.float32)
        o_ref[0, :, sl] = (o / l).astype(o_ref.dtype)


def _mem_attn(p3, kv):
    b, s, _ = p3.shape
    m = kv.shape[1]
    width = M_HEADS * M_HEAD_DIM
    tq = min(512, s)
    return pl.pallas_call(
        _mem_attn_kernel,
        grid=(b, s // tq),
        in_specs=[
            pl.BlockSpec((1, tq, width), lambda bi, i: (bi, i, T_MQ)),
            pl.BlockSpec((1, m, width), lambda bi, i: (bi, 0, 0)),
            pl.BlockSpec((1, m, width), lambda bi, i: (bi, 0, 1)),
        ],
        out_specs=pl.BlockSpec((1, tq, width), lambda bi, i: (bi, i, 0)),
        out_shape=jax.ShapeDtypeStruct((b, s, width), jnp.bfloat16),
        compiler_params=_cparams(("parallel", "parallel")),
        name="mem_attn",
    )(p3, kv, kv)


def _split_bf16(v):
    hi = v.astype(jnp.bfloat16)
    lo = (v - hi.astype(jnp.float32)).astype(jnp.bfloat16)
    return hi, lo


def _merge_kernel(x_ref, gmix_ref, wg_ref, bg_ref, ya_ref, wa_ref, ob0_ref, ob1_ref, ob2_ref,
                  ls0_ref, ls1_ref, ls2_ref, exp_ref, wb_ref, ym_ref, wm_ref, wo_ref, gffn_ref,
                  wrh_ref, wrl_ref, h_ref, xn_ref, rl_ref, ls_scr, ob_scr):
    d = x_ref.shape[2]
    tm = x_ref.shape[1]
    x = x_ref[0]
    n = _rms(x, gmix_ref[...]).astype(jnp.bfloat16)

    def gate(k):
        z = jnp.dot(n, wg_ref[:, k * d:(k + 1) * d], preferred_element_type=jnp.float32)
        return jax.nn.sigmoid(z + bg_ref[:, k * d:(k + 1) * d])

    merged = gate(0) * jnp.dot(ya_ref[0], wa_ref[...], preferred_element_type=jnp.float32)

    ob_refs = (ob0_ref, ob1_ref, ob2_ref)
    ls = []
    for gi, ls_ref in enumerate((ls0_ref, ls1_ref, ls2_ref)):
        dil = ls_ref.shape[1]
        if dil == 1:
            ls.append(ls_ref[0, 0])
        else:
            for r in range(dil):
                ls_scr[gi - 1, pl.ds(r, tm // dil, stride=dil), :] = ls_ref[0, r]
            ls.append(ls_scr[gi - 1])
    mx = jnp.maximum(jnp.maximum(ls[0], ls[1]), ls[2])
    es = [jnp.exp(v - mx) for v in ls]
    inv = 1.0 / (es[0] + es[1] + es[2])
    yb = [jnp.zeros((tm, LANES), jnp.float32) for _ in range(SLABS)]
    for e, ob in zip(es, ob_refs):
        dil = ob.shape[1]
        hi, lo = _split_bf16(e * inv)
        a = (jnp.dot(hi, exp_ref[...], preferred_element_type=jnp.float32)
             + jnp.dot(lo, exp_ref[...], preferred_element_type=jnp.float32))
        for s in range(SLABS):
            sl = slice(s * LANES, (s + 1) * LANES)
            if dil == 1:
                o = ob[0, 0, :, sl].astype(jnp.float32)
            else:
                for r in range(dil):
                    ob_scr[s, pl.ds(r, tm // dil, stride=dil), :] = ob[0, r, :, sl].astype(jnp.float32)
                o = ob_scr[s]
            yb[s] = yb[s] + a[:, sl] * o
    yb = jnp.concatenate(yb, axis=1)
    merged = merged + gate(1) * jnp.dot(yb.astype(jnp.bfloat16), wb_ref[...],
                                        preferred_element_type=jnp.float32)
    merged = merged + gate(2) * jnp.dot(ym_ref[0], wm_ref[...], preferred_element_type=jnp.float32)

    h = x + jnp.dot(merged.astype(jnp.bfloat16), wo_ref[...], preferred_element_type=jnp.float32)
    h_ref[0] = h
    xn = _rms(h, gffn_ref[...])
    xn_ref[0] = xn.astype(jnp.bfloat16)
    hi, lo = _split_bf16(xn)
    rl_ref[0] = (jnp.dot(hi, wrh_ref[...], preferred_element_type=jnp.float32)
                 + jnp.dot(lo, wrh_ref[...], preferred_element_type=jnp.float32)
                 + jnp.dot(hi, wrl_ref[...], preferred_element_type=jnp.float32))


def _merge(x3, g_mix, wg, bg, ya, wa, obs, lss, expand, wb, ym, wm, wo, g_ffn, wrh, wrl):
    b, s, d = x3.shape
    tm = min(256, s)
    row = lambda w: pl.BlockSpec((1, tm, w), lambda bi, i: (bi, i, 0))
    full = lambda a: pl.BlockSpec(a.shape, lambda bi, i: (0,) * a.ndim)

    def dil_spec(a):
        dil, width = a.shape[1], a.shape[3]
        return pl.BlockSpec((1, dil, tm // dil, width), lambda bi, i: (bi, 0, i, 0))

    return pl.pallas_call(
        _merge_kernel,
        grid=(b, s // tm),
        in_specs=[row(d), full(g_mix), full(wg), full(bg), row(ya.shape[2]), full(wa),
                  dil_spec(obs[0]), dil_spec(obs[1]), dil_spec(obs[2]),
                  dil_spec(lss[0]), dil_spec(lss[1]), dil_spec(lss[2]), full(expand), full(wb),
                  row(ym.shape[2]), full(wm), full(wo), full(g_ffn), full(wrh), full(wrl)],
        out_specs=[row(d), row(d), row(LANES)],
        out_shape=[jax.ShapeDtypeStruct((b, s, d), jnp.float32),
                   jax.ShapeDtypeStruct((b, s, d), jnp.bfloat16),
                   jax.ShapeDtypeStruct((b, s, LANES), jnp.float32)],
        scratch_shapes=[pltpu.VMEM((2, tm, LANES), jnp.float32),
                        pltpu.VMEM((SLABS, tm, LANES), jnp.float32)],
        compiler_params=_cparams(("parallel", "parallel")),
        name="merge",
    )(x3, g_mix, wg, bg, ya, wa, obs[0], obs[1], obs[2], lss[0], lss[1], lss[2], expand, wb,
      ym, wm, wo, g_ffn, wrh, wrl)


def _first_lane_where(cond, lane):
    return jnp.min(jnp.where(cond, lane, float(LANES)), axis=-1, keepdims=True)


def _route(logits, rb):
    z = logits + rb
    lane = lax.broadcasted_iota(jnp.int32, z.shape, 1).astype(jnp.float32)
    is_g = lane < MOE_GROUPS
    zg = jnp.where(is_g, z, -jnp.inf)
    mg = jnp.max(zg, axis=-1, keepdims=True)
    eg = jnp.exp(zg - mg)
    gp = eg / jnp.sum(eg, axis=-1, keepdims=True)
    g_w = jnp.max(gp, axis=-1, keepdims=True)
    g_sel = _first_lane_where(is_g & (gp == g_w), lane)
    lo = R_SUB0 + g_sel * EXPERTS_PER_GROUP
    in_grp = (lane >= lo) & (lane < lo + EXPERTS_PER_GROUP)
    zs = jnp.where(in_grp, z, -jnp.inf)
    ms = jnp.max(zs, axis=-1, keepdims=True)
    es = jnp.exp(zs - ms)
    sp = es / jnp.sum(es, axis=-1, keepdims=True)
    p1 = jnp.max(sp, axis=-1, keepdims=True)
    i1 = _first_lane_where(in_grp & (sp == p1), lane)
    rest = in_grp & (lane != i1)
    sp2 = jnp.where(rest, sp, -1.0)
    p2 = jnp.max(sp2, axis=-1, keepdims=True)
    i2 = _first_lane_where(rest & (sp2 == p2), lane)
    tot = p1 + p2
    return jnp.where(lane == i1, g_w * (p1 / tot), jnp.where(lane == i2, g_w * (p2 / tot), 0.0))


def _moe_kernel(xn_ref, rl_ref, rb_ref, h_ref, w1_ref, w3_ref, w2_ref, gf_ref, o_ref,
                comb_ref, acc_ref):
    e = pl.program_id(1)

    @pl.when(e == 0)
    def _():
        comb_ref[...] = _route(rl_ref[...], rb_ref[...])
        acc_ref[...] = jnp.zeros_like(acc_ref)

    x = xn_ref[...]
    a = jnp.dot(x, w1_ref[0], preferred_element_type=jnp.float32)
    g = jnp.dot(x, w3_ref[0], preferred_element_type=jnp.float32)
    lane = lax.broadcasted_iota(jnp.int32, comb_ref.shape, 1)
    c = jnp.sum(jnp.where(lane == R_SUB0 + e, comb_ref[...], 0.0), axis=-1, keepdims=True)
    hid = (jax.nn.silu(a) * g).astype(jnp.bfloat16)
    acc_ref[...] += c * jnp.dot(hid, w2_ref[0], preferred_element_type=jnp.float32)

    @pl.when(e == N_EXPERTS - 1)
    def _():
        o_ref[...] = _rms(h_ref[...] + acc_ref[...], gf_ref[...])


def _moe(xn, rl, rb, h, w1, w3, w2, g_final):
    t, d = xn.shape
    hid = w1.shape[2]
    tm = min(1024, t)
    return pl.pallas_call(
        _moe_kernel,
        grid=(t // tm, N_EXPERTS),
        in_specs=[
            pl.BlockSpec((tm, d), lambda i, e: (i, 0)),
            pl.BlockSpec((tm, LANES), lambda i, e: (i, 0)),
            pl.BlockSpec((1, LANES), lambda i, e: (0, 0)),
            pl.BlockSpec((tm, d), lambda i, e: (i, 0)),
            pl.BlockSpec((1, d, hid), lambda i, e: (e, 0, 0)),
            pl.BlockSpec((1, d, hid), lambda i, e: (e, 0, 0)),
            pl.BlockSpec((1, hid, d), lambda i, e: (e, 0, 0)),
            pl.BlockSpec((1, d), lambda i, e: (0, 0)),
        ],
        out_specs=pl.BlockSpec((tm, d), lambda i, e: (i, 0)),
        out_shape=jax.ShapeDtypeStruct((t, d), jnp.float32),
        scratch_shapes=[pltpu.VMEM((tm, LANES), jnp.float32), pltpu.VMEM((tm, d), jnp.float32)],
        compiler_params=_cparams(("parallel", "arbitrary")),
        name="moe",
    )(xn, rl, rb, h, w1, w3, w2, g_final)


def _pad_heads_kv(w):
    d = w.shape[0]
    group = A_HEADS // A_KV_HEADS
    w = w.reshape(d, A_KV_HEADS, group, HEAD_DIM)
    parts = [jnp.pad(w[:, j], ((0, 0), (0, 0), (j * HEAD_DIM, LANES - (j + 1) * HEAD_DIM)))
             for j in range(A_KV_HEADS)]
    return jnp.concatenate(parts, axis=1).reshape(d, A_HEADS * LANES)


def _pack_w_in(w_in):
    parts, off = [], 0
    for n in IN_SPLITS:
        parts.append(w_in[:, off:off + n])
        off += n
    aq, ak, av, iq, ik, iw, bq, bk, bv, mq = parts
    d = w_in.shape[0]
    qscale = HEAD_DIM ** -0.5
    aq_x = _pad_heads_kv(aq * (qscale * math.log2(math.e)))
    iq_x = jnp.pad((iq * qscale).reshape(d, IDX_HEADS, HEAD_DIM),
                   ((0, 0), (0, 0), (0, LANES - HEAD_DIM))).reshape(d, IDX_HEADS * LANES)
    ik_x = jnp.pad(ik, ((0, 0), (0, LANES - ik.shape[1])))
    iw2 = jnp.concatenate([iw, iw], axis=1) * (IDX_HEADS ** -0.5)
    iw_x = jnp.pad(iw2, ((0, 0), (0, LANES - iw2.shape[1])))
    width = B_HEADS * HEAD_DIM
    bqs = bq * qscale
    grp = lambda a, gi: a[:, gi * width:(gi + 1) * width]
    nat = jnp.concatenate([aq_x, iq_x, grp(bqs, 0), grp(bk, 0), ak, ik_x, av, iw_x, grp(bv, 0), mq], axis=1)
    dil = [jnp.concatenate([grp(bqs, gi), grp(bk, gi), grp(bv, gi)], axis=1) for gi in (1, 2)]
    return nat, dil


def _rope_tables(seq):
    half = HEAD_DIM // 2
    inv = ROPE_THETA ** (-jnp.arange(half, dtype=jnp.float32) / half)
    ang = jnp.arange(seq, dtype=jnp.float32)[:, None] * inv[None, :]
    cos = jnp.tile(jnp.cos(ang), (1, LANES // half))
    sign = jnp.tile(jnp.concatenate([-jnp.ones((half,), jnp.float32), jnp.ones((half,), jnp.float32)]),
                    LANES // HEAD_DIM)
    sin = jnp.tile(jnp.sin(ang), (1, LANES // half)) * sign[None, :]
    return cos, sin


def _layer(x3, mem, g_mix, g_mem, w_in, w_mem_kv, w_gate, b_gate, w_branch, w_out, g_ffn,
           w_group, b_group, w_sub, b_sub, w1, w3, w2, g_out):
    b, s, d = x3.shape
    bf = jnp.bfloat16
    cos_t, sin_t = _rope_tables(s)
    w_nat, w_dil = _pack_w_in(w_in)
    gm = g_mix[None, :]

    p4 = _in_proj(x3, gm, w_nat.astype(bf), cos_t, sin_t, P_MODES, 1)
    p3 = p4.reshape(b, s, p4.shape[3])
    srcs = [p4] + [_in_proj(x3, gm, w.astype(bf), cos_t, sin_t, D_MODES, B_PATTERNS[gi + 1][1])
                   for gi, w in enumerate(w_dil)]

    kv = _mem_kv(mem, g_mem[None, :], w_mem_kv.astype(bf))

    half = 256
    tri = (jnp.arange(half)[:, None] <= jnp.arange(half)[None, :]).astype(bf)
    ya = _dsa(p3, tri)

    obs, lss = [], []
    for gi, src in enumerate(srcs):
        tiles = (T_BQ, T_BK, T_BV) if gi == 0 else (0, 1, 2)
        o, l = _dilated(src, *tiles, gi)
        obs.append(o)
        lss.append(l)

    ym = _mem_attn(p3, kv)

    wa = _pad_heads_kv(w_branch[0].T).T.astype(bf)
    expand = (jnp.arange(LANES)[:, None] == (jnp.arange(B_HEADS * HEAD_DIM)[None, :] // HEAD_DIM)).astype(bf)
    w_route = jnp.concatenate([w_group, jnp.moveaxis(w_sub, 0, 1).reshape(d, N_EXPERTS)], axis=1)
    w_route = jnp.pad(w_route, ((0, 0), (0, LANES - w_route.shape[1])))
    wrh = w_route.astype(bf)
    wrl = (w_route - wrh.astype(jnp.float32)).astype(bf)
    r_bias = jnp.pad(jnp.concatenate([b_group, b_sub.reshape(-1)]), (0, LANES - MOE_GROUPS - N_EXPERTS))

    h, xn, rl = _merge(x3, gm, w_gate.astype(bf), b_gate[None, :], ya, wa, obs, lss, expand,
                       w_branch[1].astype(bf), ym, w_branch[2].astype(bf), w_out.astype(bf),
                       g_ffn[None, :], wrh, wrl)
    t = b * s
    out = _moe(xn.reshape(t, d), rl.reshape(t, LANES), r_bias[None, :], h.reshape(t, d),
               w1.astype(bf), w3.astype(bf), w2.astype(bf), g_out[None, :])
    return out.reshape(b, s, d)


def kernel(x, mem, g_mix, g_mem, w_in, w_mem_kv, w_gate, b_gate, w_branch, w_out, g_ffn,
           w_group, b_group, w_sub, b_sub, w1, w3, w2, g_final):
    depth = g_mix.shape[0]
    assert depth == 1, "the final rmsnorm is fused into the single layer's last kernel"
    return _layer(x, mem, g_mix[0], g_mem[0], w_in[0], w_mem_kv[0], w_gate[0], b_gate[0], w_branch[0],
                  w_out[0], g_ffn[0], w_group[0], b_group[0], w_sub[0], b_sub[0], w1[0], w3[0], w2[0],
                  g_final)
```

```python
import functools
import math

import jax
import jax.numpy as jnp
from jax import lax
from jax.experimental import pallas as pl
from jax.experimental.pallas import tpu as pltpu

HEAD_DIM = 64
ROPE_THETA = 10000.0
RMS_EPS = 1e-6
A_HEADS = 8
A_KV_HEADS = 2
IDX_HEADS = 4
TOPK_MAX = 256
B_PATTERNS = ((128, 1), (512, 4), (2048, 16))
B_HEADS = 8
M_HEADS = 4
M_HEAD_DIM = 128
MOE_GROUPS = 4
EXPERTS_PER_GROUP = 4
N_EXPERTS = 16
IN_SPLITS = (512, 128, 128, 256, 64, 4, 1536, 1536, 1536, 512)

LANES = 128
PROJ_TILE = 512
SLABS = PROJ_TILE // LANES
VMEM_LIMIT = 56 * 1024 * 1024

T_AQ, T_IQ, T_BQ, T_BK, T_MIX, T_BV, T_MQ = 0, 2, 3, 4, 5, 6, 7
P_MODES = (("rope",) * 4,) * 5 + (("rope", "rope", "plain", "hi_lo"),) + (("plain",) * 4,) * 2
SLAB_AK = T_MIX * SLABS + 0
SLAB_IK = T_MIX * SLABS + 1
SLAB_AV = T_MIX * SLABS + 2
SLAB_IW = T_MIX * SLABS + 3
D_MODES = (("rope",) * 4, ("rope",) * 4, ("plain",) * 4)

INT_MIN = -(2 ** 31)
NEG_BIG = -1e30

R_SUB0 = MOE_GROUPS


def _cparams(sem):
    return pltpu.CompilerParams(dimension_semantics=sem, vmem_limit_bytes=VMEM_LIMIT)


def _rms(xf, g):
    return xf * lax.rsqrt(jnp.mean(xf * xf, axis=-1, keepdims=True) + RMS_EPS) * g


def _rope_slab(y, cos, sin_signed, first_half):
    partner = jnp.where(first_half, pltpu.roll(y, 96, 1), pltpu.roll(y, 32, 1))
    return y * cos + partner * sin_signed


def _in_proj_kernel(x_ref, g_ref, w_ref, cos_ref, sin_ref, o_ref, n_ref, *stage, modes, dil, rows):
    j = pl.program_id(2)
    tm = x_ref.shape[1]

    @pl.when(j == 0)
    def _():
        n_ref[...] = _rms(x_ref[0], g_ref[...]).astype(jnp.bfloat16)

    lane = lax.broadcasted_iota(jnp.int32, (rows, LANES), 1)
    first_half = (lane & (HEAD_DIM - 1)) < (HEAD_DIM // 2)

    def epilogue(slab_modes):
        for r in range(tm // rows):
            rs = pl.ds(r * rows, rows)
            acc = jnp.dot(n_ref[rs, :], w_ref[...], preferred_element_type=jnp.float32)
            cos = cos_ref[rs, :]
            sin = sin_ref[rs, :]
            for s in range(SLABS):
                y = acc[:, s * LANES:(s + 1) * LANES]
                if slab_modes[s] == "rope":
                    y = _rope_slab(y, cos, sin, first_half)
                elif slab_modes[s] == "hi_lo":
                    resid = y - y.astype(jnp.bfloat16).astype(jnp.float32)
                    y = jnp.where(lane < IDX_HEADS, y, resid)
                if dil == 1:
                    o_ref[0, 0, rs, s * LANES:(s + 1) * LANES] = y.astype(o_ref.dtype)
                else:
                    stage[0][s, rs, :] = y
        if dil > 1:
            for r in range(dil):
                for s in range(SLABS):
                    o_ref[0, r, :, s * LANES:(s + 1) * LANES] = (
                        stage[0][s, pl.ds(r, tm // dil, stride=dil), :].astype(o_ref.dtype))

    for pattern in sorted(set(modes)):
        tiles = [t for t, m in enumerate(modes) if m == pattern]
        cond = functools.reduce(jnp.logical_or, [j == t for t in tiles])
        pl.when(cond)(functools.partial(epilogue, pattern))


def _in_proj(x3, g_mix, wp, cos_t, sin_t, modes, dil):
    b, s, d = x3.shape
    tm = min(1024, s)
    ntiles = len(modes)
    scratch = [pltpu.VMEM((tm, d), jnp.bfloat16)]
    if dil > 1:
        scratch.append(pltpu.VMEM((SLABS, tm, LANES), jnp.float32))
    return pl.pallas_call(
        functools.partial(_in_proj_kernel, modes=modes, dil=dil, rows=256),
        grid=(b, s // tm, ntiles),
        in_specs=[
            pl.BlockSpec((1, tm, d), lambda bi, i, j: (bi, i, 0)),
            pl.BlockSpec((1, d), lambda bi, i, j: (0, 0)),
            pl.BlockSpec((d, PROJ_TILE), lambda bi, i, j: (0, j)),
            pl.BlockSpec((tm, LANES), lambda bi, i, j: (i, 0)),
            pl.BlockSpec((tm, LANES), lambda bi, i, j: (i, 0)),
        ],
        out_specs=pl.BlockSpec((1, dil, tm // dil, PROJ_TILE), lambda bi, i, j: (bi, 0, i, j)),
        out_shape=jax.ShapeDtypeStruct((b, dil, s // dil, ntiles * PROJ_TILE), jnp.bfloat16),
        scratch_shapes=scratch,
        compiler_params=_cparams(("parallel", "parallel", "arbitrary")),
        name=f"in_proj_d{dil}",
    )(x3, g_mix, wp, cos_t, sin_t)


def _mem_kv_kernel(m_ref, g_ref, w_ref, o_ref):
    n = _rms(m_ref[0], g_ref[...]).astype(jnp.bfloat16)
    o_ref[0] = jnp.dot(n, w_ref[...], preferred_element_type=jnp.float32).astype(o_ref.dtype)


def _mem_kv(mem, g_mem, w_kv):
    b, m, d = mem.shape
    n = w_kv.shape[1]
    return pl.pallas_call(
        _mem_kv_kernel,
        grid=(b,),
        in_specs=[
            pl.BlockSpec((1, m, d), lambda i: (i, 0, 0)),
            pl.BlockSpec((1, d), lambda i: (0, 0)),
            pl.BlockSpec((d, n), lambda i: (0, 0)),
        ],
        out_specs=pl.BlockSpec((1, m, n), lambda i: (i, 0, 0)),
        out_shape=jax.ShapeDtypeStruct((b, m, n), jnp.bfloat16),
        compiler_params=_cparams(("parallel",)),
        name="mem_kv",
    )(mem, g_mem, w_kv)


def _key_to_float(key):
    bits = jnp.where(key < 0, key ^ jnp.int32(0x7FFFFFFF), key)
    f = lax.bitcast_convert_type(bits, jnp.float32)
    return jnp.where((key < 0) & (f != f), -jnp.inf, f)


def _dsa_kernel(aq_ref, iq_ref, iw_ref, ik_ref, ak_ref, av_ref, tri_ref, o_ref,
                sc_ref, lists_ref, thr_ref, need_ref, off_ref, m_ref, acc_ref,
                *, tq, kc, topk, depth, ins_rows, bis_rows):
    i = pl.program_id(1)
    q0 = i * tq
    nchunks = (q0 + tq + kc - 1) // kc
    nslab = kc // LANES
    group = A_HEADS // A_KV_HEADS

    iw = iw_ref[0].astype(jnp.float32)
    w_heads = [iw[:, h:h + 1] + iw[:, IDX_HEADS + h:IDX_HEADS + h + 1] for h in range(IDX_HEADS)]
    qpos = q0 + lax.broadcasted_iota(jnp.int32, (tq, kc), 0)
    kiota = lax.broadcasted_iota(jnp.int32, (tq, kc), 1)

    def score_chunk(c, carry):
        k0 = pl.multiple_of(c * kc, kc)
        ikc = ik_ref[0, pl.ds(k0, kc), :]
        s = jnp.zeros((tq, kc), jnp.float32)
        for h in range(IDX_HEADS):
            d = lax.dot_general(iq_ref[0, :, h * LANES:(h + 1) * LANES], ikc,
                                (((1,), (1,)), ((), ())), preferred_element_type=jnp.float32)
            s = s + jnp.maximum(d, 0.0) * w_heads[h]
        sc_ref[c] = jnp.where(kiota + k0 <= qpos, s, -jnp.inf)
        return carry

    lax.fori_loop(0, nchunks, score_chunk, 0)

    def insert_rows(rp, carry):
        r0 = pl.multiple_of(rp * ins_rows, ins_rows)

        def chunk_body(c, tops):
            tops = list(tops)
            for s in range(nslab):
                x = sc_ref[c, pl.ds(r0, ins_rows), s * LANES:(s + 1) * LANES]
                for j in range(depth):
                    hi = jnp.maximum(tops[j], x)
                    x = jnp.minimum(tops[j], x)
                    tops[j] = hi
            return tuple(tops)

        empty = tuple(jnp.full((ins_rows, LANES), -jnp.inf, jnp.float32) for _ in range(depth))
        tops = lax.fori_loop(0, nchunks, chunk_body, empty)
        for j in range(depth):
            lists_ref[pl.ds(r0, ins_rows), j * LANES:(j + 1) * LANES] = tops[j]
        return carry

    lax.fori_loop(0, tq // ins_rows, insert_rows, 0)

    def bisect(count_ge):
        def bit_step(it, prefix):
            cand = prefix + lax.shift_left(jnp.int32(1), 31 - it)
            return jnp.where(count_ge(_key_to_float(cand)) >= float(topk), cand, prefix)
        return bit_step

    def finish(rs, thr, c_gt):
        thr_ref[rs, :] = thr
        need_ref[rs, :] = jnp.where(thr == -jnp.inf, 0.0, float(topk) - c_gt)

    def list_count(rs, cand, strict):
        acc = jnp.zeros((bis_rows, LANES), jnp.float32)
        for j in range(depth):
            v = lists_ref[rs, j * LANES:(j + 1) * LANES]
            acc = acc + jnp.where((v > cand) if strict else (v >= cand), 1.0, 0.0)
        return jnp.sum(acc, axis=-1, keepdims=True)

    nblk = tq // bis_rows
    blocks = [pl.ds(rb * bis_rows, bis_rows) for rb in range(nblk)]

    def list_bit_step(it, prefixes):
        return tuple(bisect(functools.partial(list_count, rs, strict=False))(it, pre)
                     for rs, pre in zip(blocks, prefixes))

    start = tuple(jnp.full((bis_rows, LANES), INT_MIN, jnp.int32) for _ in range(nblk))
    keys = lax.fori_loop(0, 32, list_bit_step, start)
    overflow = jnp.zeros((bis_rows, LANES), jnp.float32)
    for rs, key in zip(blocks, keys):
        thr = _key_to_float(key)
        finish(rs, thr, list_count(rs, thr, True))
        last = lists_ref[rs, (depth - 1) * LANES:depth * LANES]
        overflow = jnp.maximum(overflow, jnp.where(last > thr, 1.0, 0.0))

    @pl.when(jnp.max(overflow) > 0.0)
    def _():
        def full_count(rs, cand, strict):
            def body(c, acc):
                for s in range(nslab):
                    v = sc_ref[c, rs, s * LANES:(s + 1) * LANES]
                    acc = acc + jnp.where((v > cand) if strict else (v >= cand), 1.0, 0.0)
                return acc
            acc = lax.fori_loop(0, nchunks, body, jnp.zeros((bis_rows, LANES), jnp.float32))
            return jnp.sum(acc, axis=-1, keepdims=True)

        for rs in blocks:
            key = lax.fori_loop(0, 32, bisect(functools.partial(full_count, rs, strict=False)),
                                jnp.full((bis_rows, LANES), INT_MIN, jnp.int32))
            thr = _key_to_float(key)
            finish(rs, thr, full_count(rs, thr, True))

    off_ref[...] = jnp.zeros_like(off_ref)
    m_ref[...] = jnp.full(m_ref.shape, NEG_BIG, jnp.float32)
    acc_ref[...] = jnp.zeros_like(acc_ref)
    half = tri_ref.shape[0]

    def attn_chunk(c, carry):
        k0 = pl.multiple_of(c * kc, kc)
        akc = ak_ref[0, pl.ds(k0, kc), :]
        avc = av_ref[0, pl.ds(k0, kc), :]
        thr_b = thr_ref[...]
        need_b = need_ref[...]
        bias = []
        off = off_ref[...]
        for hh in range(kc // half):
            eq_parts = []
            for s in range(half // LANES):
                v = sc_ref[c, :, hh * half + s * LANES: hh * half + (s + 1) * LANES]
                eq_parts.append(jnp.where(v == thr_b, 1.0, 0.0).astype(jnp.bfloat16))
            eqf = jnp.concatenate(eq_parts, axis=1)
            pre = jnp.dot(eqf, tri_ref[...], preferred_element_type=jnp.float32)
            for s in range(half // LANES):
                v = sc_ref[c, :, hh * half + s * LANES: hh * half + (s + 1) * LANES]
                rank = pre[:, s * LANES:(s + 1) * LANES] + off
                tie_ok = (v == thr_b) & (rank <= need_b)
                sel = (v > thr_b) | tie_ok
                bias.append(jnp.where(sel, 0.0, NEG_BIG))
            off = off + jnp.sum(eqf.astype(jnp.float32), axis=-1, keepdims=True)
        off_ref[...] = off

        av_ones = jnp.concatenate([avc, jnp.ones_like(avc)], axis=1)
        for h in range(A_HEADS):
            logits = lax.dot_general(aq_ref[0, :, h * LANES:(h + 1) * LANES], akc,
                                     (((1,), (1,)), ((), ())), preferred_element_type=jnp.float32)
            lg = [logits[:, s * LANES:(s + 1) * LANES] + bias[s] for s in range(nslab)]
            m_old = m_ref[h]
            m_cur = functools.reduce(jnp.maximum, lg)
            m_new = jnp.maximum(m_old, jnp.max(m_cur, axis=-1, keepdims=True))
            alpha = jnp.exp2(m_old - m_new)
            p = jnp.concatenate([jnp.exp2((x - m_new).astype(jnp.bfloat16)) for x in lg], axis=1)
            m_ref[h] = m_new
            pv = jnp.dot(p, av_ones, preferred_element_type=jnp.float32)
            acc_ref[h] = jnp.concatenate([alpha, alpha], axis=1) * acc_ref[h] + pv
        return carry

    lax.fori_loop(0, nchunks, attn_chunk, 0)

    lane = lax.broadcasted_iota(jnp.int32, (tq, LANES), 1)
    for h in range(A_HEADS):
        j = h // group
        own = (lane >= j * HEAD_DIM) & (lane < (j + 1) * HEAD_DIM)
        out = acc_ref[h, :, :LANES] / acc_ref[h, :, LANES:]
        o_ref[0, :, h * LANES:(h + 1) * LANES] = jnp.where(own, out, 0.0).astype(o_ref.dtype)


def _dsa(p3, tri):
    b, s, _ = p3.shape
    tq = min(256, s)
    kc = min(512, s)
    topk = min(TOPK_MAX, s // 4)
    depth = 12
    kern = functools.partial(_dsa_kernel, tq=tq, kc=kc, topk=topk, depth=depth,
                             ins_rows=16, bis_rows=min(64, tq))
    wide = A_HEADS * LANES
    return pl.pallas_call(
        kern,
        grid=(b, s // tq),
        in_specs=[
            pl.BlockSpec((1, tq, wide), lambda bi, i: (bi, i, T_AQ)),
            pl.BlockSpec((1, tq, PROJ_TILE), lambda bi, i: (bi, i, T_IQ)),
            pl.BlockSpec((1, tq, LANES), lambda bi, i: (bi, i, SLAB_IW)),
            pl.BlockSpec((1, s, LANES), lambda bi, i: (bi, 0, SLAB_IK)),
            pl.BlockSpec((1, s, LANES), lambda bi, i: (bi, 0, SLAB_AK)),
            pl.BlockSpec((1, s, LANES), lambda bi, i: (bi, 0, SLAB_AV)),
            pl.BlockSpec(tri.shape, lambda bi, i: (0, 0)),
        ],
        out_specs=pl.BlockSpec((1, tq, wide), lambda bi, i: (bi, i, 0)),
        out_shape=jax.ShapeDtypeStruct((b, s, wide), jnp.bfloat16),
        scratch_shapes=[
            pltpu.VMEM((s // kc, tq, kc), jnp.float32),
            pltpu.VMEM((tq, depth * LANES), jnp.float32),
            pltpu.VMEM((tq, LANES), jnp.float32),
            pltpu.VMEM((tq, LANES), jnp.float32),
            pltpu.VMEM((tq, LANES), jnp.float32),
            pltpu.VMEM((A_HEADS, tq, LANES), jnp.float32),
            pltpu.VMEM((A_HEADS, tq, 2 * LANES), jnp.float32),
        ],
        compiler_params=_cparams(("parallel", "arbitrary")),
        name="dsa",
    )(p3, p3, p3, p3, p3, p3, tri)


def _dil_kernel(q_ref, kp_ref, kc_ref, vp_ref, vc_ref, o_ref, lse_ref, *, tu, w):
    u = pl.program_id(2)
    qq = lax.broadcasted_iota(jnp.int32, (2 * w, 2 * w), 0) % w
    kk = lax.broadcasted_iota(jnp.int32, (2 * w, 2 * w), 1)
    band = (kk >= qq) & (kk <= qq + w)
    band_first = band & (kk >= jnp.where(u > 0, 0, w))
    lane = lax.broadcasted_iota(jnp.int32, (w, LANES), 1)
    low = lane < HEAD_DIM
    for sb in range(tu // w):
        rows = slice(sb * w, (sb + 1) * w)
        lse_tile = jnp.zeros((w, LANES), jnp.float32)
        for pr in range(B_HEADS // 2):
            sl = slice(pr * LANES, (pr + 1) * LANES)
            q = q_ref[0, 0, rows, sl]
            if sb == 0:
                k = jnp.concatenate([kp_ref[0, 0, :, sl], kc_ref[0, 0, :w, sl]], axis=0)
                v = jnp.concatenate([vp_ref[0, 0, :, sl], vc_ref[0, 0, :w, sl]], axis=0)
            else:
                k = kc_ref[0, 0, (sb - 1) * w:(sb + 1) * w, sl]
                v = vc_ref[0, 0, (sb - 1) * w:(sb + 1) * w, sl]
            zero = jnp.zeros_like(q)
            qm = jnp.concatenate([jnp.where(low, q, zero), jnp.where(low, zero, q)], axis=0)
            lg = lax.dot_general(qm, k, (((1,), (1,)), ((), ())), preferred_element_type=jnp.float32)
            lg = jnp.where(band_first if sb == 0 else band, lg, NEG_BIG)
            m = jnp.max(lg, axis=-1, keepdims=True)
            p = jnp.exp2(lg - m).astype(jnp.bfloat16)
            nd = jnp.dot(p, jnp.concatenate([v, jnp.ones_like(v)], axis=1),
                         preferred_element_type=jnp.float32)
            l = nd[:, LANES:]
            o = nd[:, :LANES] / l
            o_ref[0, 0, rows, sl] = jnp.where(low, o[:w], o[w:]).astype(o_ref.dtype)
            lse = (m + jnp.log2(l)) * math.log(2.0)
            lse_tile = jnp.where(lane == 2 * pr, lse[:w], lse_tile)
            lse_tile = jnp.where(lane == 2 * pr + 1, lse[w:], lse_tile)
        lse_ref[0, 0, rows, :] = lse_tile


def _dilated(src, tq_tile, tk_tile, tv_tile, gi):
    b, dil, su, _ = src.shape
    window, d2 = B_PATTERNS[gi]
    assert d2 == dil
    w = window // dil
    tu = min(512, su)
    ratio = tu // w
    width = B_HEADS * HEAD_DIM
    cur = lambda t: pl.BlockSpec((1, 1, tu, width), lambda bi, r, u: (bi, r, u, t))
    prev = lambda t: pl.BlockSpec((1, 1, w, width),
                                  lambda bi, r, u: (bi, r, jnp.maximum(u * ratio - 1, 0), t))
    return pl.pallas_call(
        functools.partial(_dil_kernel, tu=tu, w=w),
        grid=(b, dil, su // tu),
        in_specs=[cur(tq_tile), prev(tk_tile), cur(tk_tile), prev(tv_tile), cur(tv_tile)],
        out_specs=[pl.BlockSpec((1, 1, tu, width), lambda bi, r, u: (bi, r, u, 0)),
                   pl.BlockSpec((1, 1, tu, LANES), lambda bi, r, u: (bi, r, u, 0))],
        out_shape=[jax.ShapeDtypeStruct((b, dil, su, width), jnp.bfloat16),
                   jax.ShapeDtypeStruct((b, dil, su, LANES), jnp.float32)],
        compiler_params=_cparams(("parallel", "parallel", "arbitrary")),
        name=f"dilated{gi}",
    )(src, src, src, src, src)


def _mem_attn_kernel(q_ref, k_ref, v_ref, o_ref):
    scale = M_HEAD_DIM ** -0.5
    for h in range(M_HEADS):
        sl = slice(h * M_HEAD_DIM, (h + 1) * M_HEAD_DIM)
        lg = lax.dot_general(q_ref[0, :, sl], k_ref[0, :, sl], (((1,), (1,)), ((), ())),
                             preferred_element_type=jnp.float32) * scale
        m = jnp.max(lg, axis=-1, keepdims=True)
        p = jnp.exp(lg - m)
        l = jnp.sum(p, axis=-1, keepdims=True)
        o = jnp.dot(p.astype(jnp.bfloat16), v_ref[0, :, sl], preferred_element_type=jnp.float32)
        o_ref[0, :, sl] = (o / l).astype(o_ref.dtype)


def _mem_attn(p3, kv):
    b, s, _ = p3.shape
    m = kv.shape[1]
    width = M_HEADS * M_HEAD_DIM
    tq = min(512, s)
    return pl.pallas_call(
        _mem_attn_kernel,
        grid=(b, s // tq),
        in_specs=[
            pl.BlockSpec((1, tq, width), lambda bi, i: (bi, i, T_MQ)),
            pl.BlockSpec((1, m, width), lambda bi, i: (bi, 0, 0)),
            pl.BlockSpec((1, m, width), lambda bi, i: (bi, 0, 1)),
        ],
        out_specs=pl.BlockSpec((1, tq, width), lambda bi, i: (bi, i, 0)),
        out_shape=jax.ShapeDtypeStruct((b, s, width), jnp.bfloat16),
        compiler_params=_cparams(("parallel", "parallel")),
        name="mem_attn",
    )(p3, kv, kv)


def _split_bf16(v):
    hi = v.astype(jnp.bfloat16)
    lo = (v - hi.astype(jnp.float32)).astype(jnp.bfloat16)
    return hi, lo


def _merge_kernel(x_ref, gmix_ref, wg_ref, bg_ref, ya_ref, wa_ref, ob0_ref, ob1_ref, ob2_ref,
                  ls0_ref, ls1_ref, ls2_ref, exp_ref, wb_ref, ym_ref, wm_ref, wo_ref, gffn_ref,
                  wrh_ref, wrl_ref, h_ref, xn_ref, rl_ref, ls_scr, ob_scr):
    d = x_ref.shape[2]
    tm = x_ref.shape[1]
    x = x_ref[0]
    n = _rms(x, gmix_ref[...]).astype(jnp.bfloat16)

    def gate(k):
        z = jnp.dot(n, wg_ref[:, k * d:(k + 1) * d], preferred_element_type=jnp.float32)
        return jax.nn.sigmoid(z + bg_ref[:, k * d:(k + 1) * d])

    merged = gate(0) * jnp.dot(ya_ref[0], wa_ref[...], preferred_element_type=jnp.float32)

    ob_refs = (ob0_ref, ob1_ref, ob2_ref)
    ls = []
    for gi, ls_ref in enumerate((ls0_ref, ls1_ref, ls2_ref)):
        dil = ls_ref.shape[1]
        if dil == 1:
            ls.append(ls_ref[0, 0])
        else:
            for r in range(dil):
                ls_scr[gi - 1, pl.ds(r, tm // dil, stride=dil), :] = ls_ref[0, r]
            ls.append(ls_scr[gi - 1])
    mx = jnp.maximum(jnp.maximum(ls[0], ls[1]), ls[2])
    es = [jnp.exp(v - mx) for v in ls]
    inv = 1.0 / (es[0] + es[1] + es[2])
    yb = [jnp.zeros((tm, LANES), jnp.float32) for _ in range(SLABS)]
    for e, ob in zip(es, ob_refs):
        dil = ob.shape[1]
        hi, lo = _split_bf16(e * inv)
        a = (jnp.dot(hi, exp_ref[...], preferred_element_type=jnp.float32)
             + jnp.dot(lo, exp_ref[...], preferred_element_type=jnp.float32))
        for s in range(SLABS):
            sl = slice(s * LANES, (s + 1) * LANES)
            if dil == 1:
                o = ob[0, 0, :, sl].astype(jnp.float32)
            else:
                for r in range(dil):
                    ob_scr[s, pl.ds(r, tm // dil, stride=dil), :] = ob[0, r, :, sl].astype(jnp.float32)
                o = ob_scr[s]
            yb[s] = yb[s] + a[:, sl] * o
    yb = jnp.concatenate(yb, axis=1)
    merged = merged + gate(1) * jnp.dot(yb.astype(jnp.bfloat16), wb_ref[...],
                                        preferred_element_type=jnp.float32)
    merged = merged + gate(2) * jnp.dot(ym_ref[0], wm_ref[...], preferred_element_type=jnp.float32)

    h = x + jnp.dot(merged.astype(jnp.bfloat16), wo_ref[...], preferred_element_type=jnp.float32)
    h_ref[0] = h
    xn = _rms(h, gffn_ref[...])
    xn_ref[0] = xn.astype(jnp.bfloat16)
    hi, lo = _split_bf16(xn)
    rl_ref[0] = (jnp.dot(hi, wrh_ref[...], preferred_element_type=jnp.float32)
                 + jnp.dot(lo, wrh_ref[...], preferred_element_type=jnp.float32)
                 + jnp.dot(hi, wrl_ref[...], preferred_element_type=jnp.float32))


def _merge(x3, g_mix, wg, bg, ya, wa, obs, lss, expand, wb, ym, wm, wo, g_ffn, wrh, wrl):
    b, s, d = x3.shape
    tm = min(256, s)
    row = lambda w: pl.BlockSpec((1, tm, w), lambda bi, i: (bi, i, 0))
    full = lambda a: pl.BlockSpec(a.shape, lambda bi, i: (0,) * a.ndim)

    def dil_spec(a):
        dil, width = a.shape[1], a.shape[3]
        return pl.BlockSpec((1, dil, tm // dil, width), lambda bi, i: (bi, 0, i, 0))

    return pl.pallas_call(
        _merge_kernel,
        grid=(b, s // tm),
        in_specs=[row(d), full(g_mix), full(wg), full(bg), row(ya.shape[2]), full(wa),
                  dil_spec(obs[0]), dil_spec(obs[1]), dil_spec(obs[2]),
                  dil_spec(lss[0]), dil_spec(lss[1]), dil_spec(lss[2]), full(expand), full(wb),
                  row(ym.shape[2]), full(wm), full(wo), full(g_ffn), full(wrh), full(wrl)],
        out_specs=[row(d), row(d), row(LANES)],
        out_shape=[jax.ShapeDtypeStruct((b, s, d), jnp.float32),
                   jax.ShapeDtypeStruct((b, s, d), jnp.bfloat16),
                   jax.ShapeDtypeStruct((b, s, LANES), jnp.float32)],
        scratch_shapes=[pltpu.VMEM((2, tm, LANES), jnp.float32),
                        pltpu.VMEM((SLABS, tm, LANES), jnp.float32)],
        compiler_params=_cparams(("parallel", "parallel")),
        name="merge",
    )(x3, g_mix, wg, bg, ya, wa, obs[0], obs[1], obs[2], lss[0], lss[1], lss[2], expand, wb,
      ym, wm, wo, g_ffn, wrh, wrl)


def _first_lane_where(cond, lane):
    return jnp.min(jnp.where(cond, lane, float(LANES)), axis=-1, keepdims=True)


def _route(logits, rb):
    z = logits + rb
    lane = lax.broadcasted_iota(jnp.int32, z.shape, 1).astype(jnp.float32)
    is_g = lane < MOE_GROUPS
    zg = jnp.where(is_g, z, -jnp.inf)
    mg = jnp.max(zg, axis=-1, keepdims=True)
    eg = jnp.exp(zg - mg)
    gp = eg / jnp.sum(eg, axis=-1, keepdims=True)
    g_w = jnp.max(gp, axis=-1, keepdims=True)
    g_sel = _first_lane_where(is_g & (gp == g_w), lane)
    lo = R_SUB0 + g_sel * EXPERTS_PER_GROUP
    in_grp = (lane >= lo) & (lane < lo + EXPERTS_PER_GROUP)
    zs = jnp.where(in_grp, z, -jnp.inf)
    ms = jnp.max(zs, axis=-1, keepdims=True)
    es = jnp.exp(zs - ms)
    sp = es / jnp.sum(es, axis=-1, keepdims=True)
    p1 = jnp.max(sp, axis=-1, keepdims=True)
    i1 = _first_lane_where(in_grp & (sp == p1), lane)
    rest = in_grp & (lane != i1)
    sp2 = jnp.where(rest, sp, -1.0)
    p2 = jnp.max(sp2, axis=-1, keepdims=True)
    i2 = _first_lane_where(rest & (sp2 == p2), lane)
    tot = p1 + p2
    comb = jnp.where(lane == i1, g_w * (p1 / tot), jnp.where(lane == i2, g_w * (p2 / tot), 0.0))
    return comb, g_sel


MOE_BLK = 256
MOE_PAIR = 2


def _moe_kernel(xn_ref, rl_ref, rb_ref, ltri_ref, w1_ref, w3_ref, w2_ref, o_ref,
                xs_ref, cs_ref, acc_ref, perm_ref, permt_ref, blk_ref):
    pr = pl.program_id(1)
    tm = xn_ref.shape[0]

    @pl.when(pr == 0)
    def _():
        comb, g_sel = _route(rl_ref[...], rb_ref[...])
        lane = lax.broadcasted_iota(jnp.int32, (tm, LANES), 1).astype(jnp.float32)
        onehot = jnp.where(lane == g_sel, 1.0, 0.0)
        seen = jnp.dot(ltri_ref[...], onehot.astype(jnp.bfloat16), preferred_element_type=jnp.float32)
        counts = seen[tm - 1:tm, :]
        lane_row = lane[0:1, :]
        offs, off = [], 0.0
        off_row = jnp.zeros((1, LANES), jnp.float32)
        for g in range(MOE_GROUPS):
            n_g = jnp.sum(jnp.where(lane_row == g, counts, 0.0))
            off_row = jnp.where(lane_row == g, off, off_row)
            first = off.astype(jnp.int32) if g else jnp.int32(0)
            start = (first // 16) * 16
            blk_ref[g] = start
            blk_ref[MOE_GROUPS + g] = (first - start + n_g.astype(jnp.int32) + MOE_BLK - 1) // MOE_BLK
            off = off + n_g
        dest = jnp.sum(onehot * (off_row + seen - 1.0), axis=-1, keepdims=True)
        dest_row = jnp.transpose(jnp.broadcast_to(dest, (tm, LANES)))[0:1, :]
        row = lax.broadcasted_iota(jnp.int32, (tm, LANES), 0).astype(jnp.float32)
        for s in range(tm // LANES):
            sl = slice(s * LANES, (s + 1) * LANES)
            permt_ref[:, sl] = jnp.where(lane + float(s * LANES) == dest, 1.0, 0.0).astype(jnp.bfloat16)
            perm_ref[:, sl] = jnp.where(row == dest_row[:, sl], 1.0, 0.0).astype(jnp.bfloat16)
        perm = perm_ref[...]
        xs_ref[0:tm, :] = jnp.dot(perm, xn_ref[...], preferred_element_type=jnp.float32).astype(jnp.bfloat16)
        hi, lo = _split_bf16(comb)
        cs_ref[0:tm, :] = (jnp.dot(perm, hi, preferred_element_type=jnp.float32)
                           + jnp.dot(perm, lo, preferred_element_type=jnp.float32))
        xs_ref[tm:, :] = jnp.zeros((MOE_BLK, xs_ref.shape[1]), jnp.bfloat16)
        cs_ref[tm:, :] = jnp.zeros((MOE_BLK, LANES), jnp.float32)
        acc_ref[...] = jnp.zeros_like(acc_ref)

    g = pr // (EXPERTS_PER_GROUP // MOE_PAIR)
    start = blk_ref[g]
    lane_i = lax.broadcasted_iota(jnp.int32, (MOE_BLK, LANES), 1)

    def block(b, carry):
        r0 = pl.multiple_of(start + b * MOE_BLK, 16)
        rows = pl.ds(r0, MOE_BLK)
        xb = xs_ref[rows, :]
        cb = cs_ref[rows, :]
        y = jnp.zeros((MOE_BLK, o_ref.shape[1]), jnp.float32)
        for e2 in range(MOE_PAIR):
            c = jnp.sum(jnp.where(lane_i == R_SUB0 + MOE_PAIR * pr + e2, cb, 0.0), axis=-1, keepdims=True)
            a = jnp.dot(xb, w1_ref[e2], preferred_element_type=jnp.float32)
            u = jnp.dot(xb, w3_ref[e2], preferred_element_type=jnp.float32)
            hid = (jax.nn.silu(a) * u * c).astype(jnp.bfloat16)
            y = y + jnp.dot(hid, w2_ref[e2], preferred_element_type=jnp.float32)
        acc_ref[rows, :] += y
        return carry

    lax.fori_loop(0, blk_ref[MOE_GROUPS + g], block, 0)

    @pl.when(pr == N_EXPERTS // MOE_PAIR - 1)
    def _():
        o_ref[...] = jnp.dot(permt_ref[...], acc_ref[0:tm, :].astype(jnp.bfloat16),
                             preferred_element_type=jnp.float32).astype(o_ref.dtype)


def _moe(xn, rl, rb, w1, w3, w2):
    t, d = xn.shape
    hid = w1.shape[2]
    tm = min(1024, t)
    ltri = (jnp.arange(tm)[:, None] >= jnp.arange(tm)[None, :]).astype(jnp.bfloat16)
    return pl.pallas_call(
        _moe_kernel,
        grid=(t // tm, N_EXPERTS // MOE_PAIR),
        in_specs=[
            pl.BlockSpec((tm, d), lambda i, p: (i, 0)),
            pl.BlockSpec((tm, LANES), lambda i, p: (i, 0)),
            pl.BlockSpec((1, LANES), lambda i, p: (0, 0)),
            pl.BlockSpec((tm, tm), lambda i, p: (0, 0)),
            pl.BlockSpec((MOE_PAIR, d, hid), lambda i, p: (p, 0, 0)),
            pl.BlockSpec((MOE_PAIR, d, hid), lambda i, p: (p, 0, 0)),
            pl.BlockSpec((MOE_PAIR, hid, d), lambda i, p: (p, 0, 0)),
        ],
        out_specs=pl.BlockSpec((tm, d), lambda i, p: (i, 0)),
        out_shape=jax.ShapeDtypeStruct((t, d), jnp.bfloat16),
        scratch_shapes=[
            pltpu.VMEM((tm + MOE_BLK, d), jnp.bfloat16),
            pltpu.VMEM((tm + MOE_BLK, LANES), jnp.float32),
            pltpu.VMEM((tm + MOE_BLK, d), jnp.float32),
            pltpu.VMEM((tm, tm), jnp.bfloat16),
            pltpu.VMEM((tm, tm), jnp.bfloat16),
            pltpu.SMEM((2 * MOE_GROUPS,), jnp.int32),
        ],
        compiler_params=_cparams(("parallel", "arbitrary")),
        name="moe",
    )(xn, rl, rb, ltri, w1, w3, w2)


def _final_kernel(h_ref, m_ref, g_ref, o_ref):
    o_ref[...] = _rms(h_ref[...] + m_ref[...].astype(jnp.float32), g_ref[...])


def _final(h, moe, g_final):
    t, d = h.shape
    tm = min(512, t)
    return pl.pallas_call(
        _final_kernel,
        grid=(t // tm,),
        in_specs=[pl.BlockSpec((tm, d), lambda i: (i, 0)), pl.BlockSpec((tm, d), lambda i: (i, 0)),
                  pl.BlockSpec((1, d), lambda i: (0, 0))],
        out_specs=pl.BlockSpec((tm, d), lambda i: (i, 0)),
        out_shape=jax.ShapeDtypeStruct((t, d), jnp.float32),
        compiler_params=_cparams(("parallel",)),
        name="final_norm",
    )(h, moe, g_final)


def _pad_heads_kv(w):
    d = w.shape[0]
    group = A_HEADS // A_KV_HEADS
    w = w.reshape(d, A_KV_HEADS, group, HEAD_DIM)
    parts = [jnp.pad(w[:, j], ((0, 0), (0, 0), (j * HEAD_DIM, LANES - (j + 1) * HEAD_DIM)))
             for j in range(A_KV_HEADS)]
    return jnp.concatenate(parts, axis=1).reshape(d, A_HEADS * LANES)


def _pack_w_in(w_in):
    parts, off = [], 0
    for n in IN_SPLITS:
        parts.append(w_in[:, off:off + n])
        off += n
    aq, ak, av, iq, ik, iw, bq, bk, bv, mq = parts
    d = w_in.shape[0]
    qscale = HEAD_DIM ** -0.5
    aq_x = _pad_heads_kv(aq * (qscale * math.log2(math.e)))
    iq_x = jnp.pad((iq * qscale).reshape(d, IDX_HEADS, HEAD_DIM),
                   ((0, 0), (0, 0), (0, LANES - HEAD_DIM))).reshape(d, IDX_HEADS * LANES)
    ik_x = jnp.pad(ik, ((0, 0), (0, LANES - ik.shape[1])))
    iw2 = jnp.concatenate([iw, iw], axis=1) * (IDX_HEADS ** -0.5)
    iw_x = jnp.pad(iw2, ((0, 0), (0, LANES - iw2.shape[1])))
    width = B_HEADS * HEAD_DIM
    bqs = bq * (qscale * math.log2(math.e))
    grp = lambda a, gi: a[:, gi * width:(gi + 1) * width]
    nat = jnp.concatenate([aq_x, iq_x, grp(bqs, 0), grp(bk, 0), ak, ik_x, av, iw_x, grp(bv, 0), mq], axis=1)
    dil = [jnp.concatenate([grp(bqs, gi), grp(bk, gi), grp(bv, gi)], axis=1) for gi in (1, 2)]
    return nat, dil


def _rope_tables(seq):
    half = HEAD_DIM // 2
    inv = ROPE_THETA ** (-jnp.arange(half, dtype=jnp.float32) / half)
    ang = jnp.arange(seq, dtype=jnp.float32)[:, None] * inv[None, :]
    cos = jnp.tile(jnp.cos(ang), (1, LANES // half))
    sign = jnp.tile(jnp.concatenate([-jnp.ones((half,), jnp.float32), jnp.ones((half,), jnp.float32)]),
                    LANES // HEAD_DIM)
    sin = jnp.tile(jnp.sin(ang), (1, LANES // half)) * sign[None, :]
    return cos, sin


def _layer(x3, mem, g_mix, g_mem, w_in, w_mem_kv, w_gate, b_gate, w_branch, w_out, g_ffn,
           w_group, b_group, w_sub, b_sub, w1, w3, w2, g_out):
    b, s, d = x3.shape
    bf = jnp.bfloat16
    cos_t, sin_t = _rope_tables(s)
    w_nat, w_dil = _pack_w_in(w_in)
    gm = g_mix[None, :]

    p4 = _in_proj(x3, gm, w_nat.astype(bf), cos_t, sin_t, P_MODES, 1)
    p3 = p4.reshape(b, s, p4.shape[3])
    srcs = [p4] + [_in_proj(x3, gm, w.astype(bf), cos_t, sin_t, D_MODES, B_PATTERNS[gi + 1][1])
                   for gi, w in enumerate(w_dil)]

    kv = _mem_kv(mem, g_mem[None, :], w_mem_kv.astype(bf))

    half = 256
    tri = (jnp.arange(half)[:, None] <= jnp.arange(half)[None, :]).astype(bf)
    ya = _dsa(p3, tri)

    obs, lss = [], []
    for gi, src in enumerate(srcs):
        tiles = (T_BQ, T_BK, T_BV) if gi == 0 else (0, 1, 2)
        o, l = _dilated(src, *tiles, gi)
        obs.append(o)
        lss.append(l)

    ym = _mem_attn(p3, kv)

    wa = _pad_heads_kv(w_branch[0].T).T.astype(bf)
    expand = (jnp.arange(LANES)[:, None] == (jnp.arange(B_HEADS * HEAD_DIM)[None, :] // HEAD_DIM)).astype(bf)
    w_route = jnp.concatenate([w_group, jnp.moveaxis(w_sub, 0, 1).reshape(d, N_EXPERTS)], axis=1)
    w_route = jnp.pad(w_route, ((0, 0), (0, LANES - w_route.shape[1])))
    wrh = w_route.astype(bf)
    wrl = (w_route - wrh.astype(jnp.float32)).astype(bf)
    r_bias = jnp.pad(jnp.concatenate([b_group, b_sub.reshape(-1)]), (0, LANES - MOE_GROUPS - N_EXPERTS))

    h, xn, rl = _merge(x3, gm, w_gate.astype(bf), b_gate[None, :], ya, wa, obs, lss, expand,
                       w_branch[1].astype(bf), ym, w_branch[2].astype(bf), w_out.astype(bf),
                       g_ffn[None, :], wrh, wrl)
    t = b * s
    moe = _moe(xn.reshape(t, d), rl.reshape(t, LANES), r_bias[None, :],
               w1.astype(bf), w3.astype(bf), w2.astype(bf))
    return _final(h.reshape(t, d), moe, g_out[None, :]).reshape(b, s, d)


def kernel(x, mem, g_mix, g_mem, w_in, w_mem_kv, w_gate, b_gate, w_branch, w_out, g_ffn,
           w_group, b_group, w_sub, b_sub, w1, w3, w2, g_final):
    depth = g_mix.shape[0]
    assert depth == 1, "the final rmsnorm is fused into the single layer's last kernel"
    return _layer(x, mem, g_mix[0], g_mem[0], w_in[0], w_mem_kv[0], w_gate[0], b_gate[0], w_branch[0],
                  w_out[0], g_ffn[0], w_group[0], b_group[0], w_sub[0], b_sub[0], w1[0], w3[0], w2[0],
                  g_final)
```

```python
import functools
import math

import jax
import jax.numpy as jnp
from jax import lax
from jax.experimental import pallas as pl
from jax.experimental.pallas import tpu as pltpu

HEAD_DIM = 64
ROPE_THETA = 10000.0
RMS_EPS = 1e-6
A_HEADS = 8
A_KV_HEADS = 2
IDX_HEADS = 4
TOPK_MAX = 256
B_PATTERNS = ((128, 1), (512, 4), (2048, 16))
B_HEADS = 8
M_HEADS = 4
M_HEAD_DIM = 128
MOE_GROUPS = 4
EXPERTS_PER_GROUP = 4
N_EXPERTS = 16
IN_SPLITS = (512, 128, 128, 256, 64, 4, 1536, 1536, 1536, 512)

LANES = 128
PROJ_TILE = 512
SLABS = PROJ_TILE // LANES
VMEM_LIMIT = 56 * 1024 * 1024

T_AQ, T_IQ, T_BQ, T_BK, T_MIX, T_BV, T_MQ = 0, 2, 3, 4, 5, 6, 7
P_MODES = (("rope",) * 4,) * 5 + (("rope", "rope", "plain", "hi_lo"),) + (("plain",) * 4,) * 2
SLAB_AK = T_MIX * SLABS + 0
SLAB_IK = T_MIX * SLABS + 1
SLAB_AV = T_MIX * SLABS + 2
SLAB_IW = T_MIX * SLABS + 3
D_MODES = (("rope",) * 4, ("rope",) * 4, ("plain",) * 4)

INT_MIN = -(2 ** 31)
NEG_BIG = -1e30

R_SUB0 = MOE_GROUPS


def _cparams(sem):
    return pltpu.CompilerParams(dimension_semantics=sem, vmem_limit_bytes=VMEM_LIMIT)


def _rms(xf, g):
    return xf * lax.rsqrt(jnp.mean(xf * xf, axis=-1, keepdims=True) + RMS_EPS) * g


def _rope_slab(y, cos, sin_signed, first_half):
    partner = jnp.where(first_half, pltpu.roll(y, 96, 1), pltpu.roll(y, 32, 1))
    return y * cos + partner * sin_signed


def _in_proj_kernel(x_ref, g_ref, w_ref, cos_ref, sin_ref, o_ref, n_ref, *stage, modes, dil, rows):
    j = pl.program_id(2)
    tm = x_ref.shape[1]

    @pl.when(j == 0)
    def _():
        n_ref[...] = _rms(x_ref[0], g_ref[...]).astype(jnp.bfloat16)

    lane = lax.broadcasted_iota(jnp.int32, (rows, LANES), 1)
    first_half = (lane & (HEAD_DIM - 1)) < (HEAD_DIM // 2)

    def epilogue(slab_modes):
        for r in range(tm // rows):
            rs = pl.ds(r * rows, rows)
            acc = jnp.dot(n_ref[rs, :], w_ref[...], preferred_element_type=jnp.float32)
            cos = cos_ref[rs, :]
            sin = sin_ref[rs, :]
            for s in range(SLABS):
                y = acc[:, s * LANES:(s + 1) * LANES]
                if slab_modes[s] == "rope":
                    y = _rope_slab(y, cos, sin, first_half)
                elif slab_modes[s] == "hi_lo":
                    resid = y - y.astype(jnp.bfloat16).astype(jnp.float32)
                    y = jnp.where(lane < IDX_HEADS, y, resid)
                if dil == 1:
                    o_ref[0, 0, rs, s * LANES:(s + 1) * LANES] = y.astype(o_ref.dtype)
                else:
                    stage[0][s, rs, :] = y
        if dil > 1:
            for r in range(dil):
                for s in range(SLABS):
                    o_ref[0, r, :, s * LANES:(s + 1) * LANES] = (
                        stage[0][s, pl.ds(r, tm // dil, stride=dil), :].astype(o_ref.dtype))

    for pattern in sorted(set(modes)):
        tiles = [t for t, m in enumerate(modes) if m == pattern]
        cond = functools.reduce(jnp.logical_or, [j == t for t in tiles])
        pl.when(cond)(functools.partial(epilogue, pattern))


def _in_proj(x3, g_mix, wp, cos_t, sin_t, modes, dil):
    b, s, d = x3.shape
    tm = min(1024, s)
    ntiles = len(modes)
    scratch = [pltpu.VMEM((tm, d), jnp.bfloat16)]
    if dil > 1:
        scratch.append(pltpu.VMEM((SLABS, tm, LANES), jnp.float32))
    return pl.pallas_call(
        functools.partial(_in_proj_kernel, modes=modes, dil=dil, rows=256),
        grid=(b, s // tm, ntiles),
        in_specs=[
            pl.BlockSpec((1, tm, d), lambda bi, i, j: (bi, i, 0)),
            pl.BlockSpec((1, d), lambda bi, i, j: (0, 0)),
            pl.BlockSpec((d, PROJ_TILE), lambda bi, i, j: (0, j)),
            pl.BlockSpec((tm, LANES), lambda bi, i, j: (i, 0)),
            pl.BlockSpec((tm, LANES), lambda bi, i, j: (i, 0)),
        ],
        out_specs=pl.BlockSpec((1, dil, tm // dil, PROJ_TILE), lambda bi, i, j: (bi, 0, i, j)),
        out_shape=jax.ShapeDtypeStruct((b, dil, s // dil, ntiles * PROJ_TILE), jnp.bfloat16),
        scratch_shapes=scratch,
        compiler_params=_cparams(("parallel", "parallel", "arbitrary")),
        name=f"in_proj_d{dil}",
    )(x3, g_mix, wp, cos_t, sin_t)


def _mem_kv_kernel(m_ref, g_ref, w_ref, o_ref):
    n = _rms(m_ref[0], g_ref[...]).astype(jnp.bfloat16)
    o_ref[0] = jnp.dot(n, w_ref[...], preferred_element_type=jnp.float32).astype(o_ref.dtype)


def _mem_kv(mem, g_mem, w_kv):
    b, m, d = mem.shape
    n = w_kv.shape[1]
    return pl.pallas_call(
        _mem_kv_kernel,
        grid=(b,),
        in_specs=[
            pl.BlockSpec((1, m, d), lambda i: (i, 0, 0)),
            pl.BlockSpec((1, d), lambda i: (0, 0)),
            pl.BlockSpec((d, n), lambda i: (0, 0)),
        ],
        out_specs=pl.BlockSpec((1, m, n), lambda i: (i, 0, 0)),
        out_shape=jax.ShapeDtypeStruct((b, m, n), jnp.bfloat16),
        compiler_params=_cparams(("parallel",)),
        name="mem_kv",
    )(mem, g_mem, w_kv)


def _key_to_float(key):
    bits = jnp.where(key < 0, key ^ jnp.int32(0x7FFFFFFF), key)
    f = lax.bitcast_convert_type(bits, jnp.float32)
    return jnp.where((key < 0) & (f != f), -jnp.inf, f)


def _dsa_kernel(aq_ref, iq_ref, iw_ref, ik_ref, ak_ref, av_ref, tri_ref, o_ref,
                sc_ref, lists_ref, lbf_ref, thr_ref, need_ref, off_ref, m_ref, acc_ref,
                *, tq, kc, topk, depth, ins_rows, bis_rows, att_rows):
    i = pl.program_id(1)
    q0 = i * tq
    nchunks = (q0 + tq + kc - 1) // kc
    nslab = kc // LANES
    group = A_HEADS // A_KV_HEADS

    iw = iw_ref[0].astype(jnp.float32)
    w_heads = [iw[:, h:h + 1] + iw[:, IDX_HEADS + h:IDX_HEADS + h + 1] for h in range(IDX_HEADS)]
    qpos = q0 + lax.broadcasted_iota(jnp.int32, (tq, kc), 0)
    kiota = lax.broadcasted_iota(jnp.int32, (tq, kc), 1)

    def score_chunk(c, carry):
        k0 = pl.multiple_of(c * kc, kc)
        ikc = ik_ref[0, pl.ds(k0, kc), :]
        s = jnp.zeros((tq, kc), jnp.float32)
        for h in range(IDX_HEADS):
            d = lax.dot_general(iq_ref[0, :, h * LANES:(h + 1) * LANES], ikc,
                                (((1,), (1,)), ((), ())), preferred_element_type=jnp.float32)
            s = s + jnp.maximum(d, 0.0) * w_heads[h]
        sc_ref[c] = jnp.where(kiota + k0 <= qpos, s, -jnp.inf)
        return carry

    lax.fori_loop(0, nchunks, score_chunk, 0)

    def insert_rows(rp, carry):
        r0 = pl.multiple_of(rp * ins_rows, ins_rows)

        def chunk_body(c, tops):
            tops = list(tops)
            for s in range(nslab):
                x = sc_ref[c, pl.ds(r0, ins_rows), s * LANES:(s + 1) * LANES]
                for j in range(depth):
                    hi = jnp.maximum(tops[j], x)
                    x = jnp.minimum(tops[j], x)
                    tops[j] = hi
            return tuple(tops)

        empty = tuple(jnp.full((ins_rows, LANES), -jnp.inf, jnp.float32) for _ in range(depth))
        tops = lax.fori_loop(0, nchunks, chunk_body, empty)
        for j in range(depth):
            lists_ref[pl.ds(r0, ins_rows), j * LANES:(j + 1) * LANES] = tops[j]
            lbf_ref[pl.ds(r0, ins_rows), j * LANES:(j + 1) * LANES] = tops[j].astype(jnp.bfloat16)
        return carry

    lax.fori_loop(0, tq // ins_rows, insert_rows, 0)

    def bisect(count_ge):
        def bit_step(it, prefix):
            cand = prefix + lax.shift_left(jnp.int32(1), 31 - it)
            return jnp.where(count_ge(_key_to_float(cand)) >= float(topk), cand, prefix)
        return bit_step

    def finish(rs, thr, c_gt):
        thr_ref[rs, :] = thr
        need_ref[rs, :] = jnp.where(thr == -jnp.inf, 0.0, float(topk) - c_gt)

    def list_count(rs, cand, strict):
        acc = jnp.zeros((bis_rows, LANES), jnp.float32)
        for j in range(depth):
            v = lists_ref[rs, j * LANES:(j + 1) * LANES]
            acc = acc + jnp.where((v > cand) if strict else (v >= cand), 1.0, 0.0)
        return jnp.sum(acc, axis=-1, keepdims=True)

    nblk = tq // bis_rows
    blocks = [pl.ds(rb * bis_rows, bis_rows) for rb in range(nblk)]

    def coarse_count(rs, cand16):
        bits = jnp.where(cand16 < 0, cand16 ^ jnp.int32(0x7FFF), cand16)
        f = lax.bitcast_convert_type(lax.shift_left(bits, 16), jnp.float32)
        cand = jnp.where((cand16 < 0) & (f != f), -jnp.inf, f).astype(jnp.bfloat16)
        acc = jnp.zeros((bis_rows, LANES), jnp.bfloat16)
        one, zero = jnp.ones_like(acc), jnp.zeros_like(acc)
        for j in range(depth):
            acc = acc + jnp.where(lbf_ref[rs, j * LANES:(j + 1) * LANES] >= cand, one, zero)
        return jnp.sum(acc.astype(jnp.float32), axis=-1, keepdims=True)

    def coarse_step(it, prefixes):
        out = []
        for rs, pre in zip(blocks, prefixes):
            cand = pre + lax.shift_left(jnp.int32(1), 15 - it)
            out.append(jnp.where(coarse_count(rs, cand) >= float(topk), cand, pre))
        return tuple(out)

    start16 = tuple(jnp.full((bis_rows, LANES), -(2 ** 15), jnp.int32) for _ in range(nblk))
    keys16 = lax.fori_loop(0, 16, coarse_step, start16)

    fine_bits = 18

    def fine_step(it, prefixes):
        out = []
        for rs, pre in zip(blocks, prefixes):
            cand = pre + lax.shift_left(jnp.int32(1), fine_bits - 1 - it)
            cnt = list_count(rs, _key_to_float(cand), False)
            out.append(jnp.where(cnt >= float(topk), cand, pre))
        return tuple(out)

    start32 = tuple(lax.shift_left(k16, 16) - jnp.int32(1 << 16) for k16 in keys16)
    keys = lax.fori_loop(0, fine_bits, fine_step, start32)

    def full_count(rs, cand, strict):
        def body(c, acc):
            for s in range(nslab):
                v = sc_ref[c, rs, s * LANES:(s + 1) * LANES]
                acc = acc + jnp.where((v > cand) if strict else (v >= cand), 1.0, 0.0)
            return acc
        acc = lax.fori_loop(0, nchunks, body, jnp.zeros((bis_rows, LANES), jnp.float32))
        return jnp.sum(acc, axis=-1, keepdims=True)

    def redo_block(rs):
        key = lax.fori_loop(0, 32, bisect(functools.partial(full_count, rs, strict=False)),
                            jnp.full((bis_rows, LANES), INT_MIN, jnp.int32))
        thr = _key_to_float(key)
        finish(rs, thr, full_count(rs, thr, True))

    for rs, key in zip(blocks, keys):
        thr = _key_to_float(key)
        finish(rs, thr, list_count(rs, thr, True))
        last = lists_ref[rs, (depth - 1) * LANES:depth * LANES]
        overflow = jnp.max(jnp.where(last > thr, 1.0, 0.0))
        pl.when(overflow > 0.0)(functools.partial(redo_block, rs))

    off_ref[...] = jnp.zeros_like(off_ref)
    m_ref[...] = jnp.full(m_ref.shape, NEG_BIG, jnp.float32)
    acc_ref[...] = jnp.zeros_like(acc_ref)
    half = tri_ref.shape[0]

    def attn_chunk(c, carry):
        k0 = pl.multiple_of(c * kc, kc)
        akc = ak_ref[0, pl.ds(k0, kc), :]
        avc = av_ref[0, pl.ds(k0, kc), :]
        thr_b = thr_ref[...]
        need_b = need_ref[...]
        bias = []
        off = off_ref[...]
        for hh in range(kc // half):
            eq_parts = []
            for s in range(half // LANES):
                v = sc_ref[c, :, hh * half + s * LANES: hh * half + (s + 1) * LANES]
                eq_parts.append(jnp.where(v == thr_b, 1.0, 0.0).astype(jnp.bfloat16))
            eqf = jnp.concatenate(eq_parts, axis=1)
            pre = jnp.dot(eqf, tri_ref[...], preferred_element_type=jnp.float32)
            for s in range(half // LANES):
                v = sc_ref[c, :, hh * half + s * LANES: hh * half + (s + 1) * LANES]
                rank = pre[:, s * LANES:(s + 1) * LANES] + off
                tie_ok = (v == thr_b) & (rank <= need_b)
                sel = (v > thr_b) | tie_ok
                bias.append(jnp.where(sel, 0.0, NEG_BIG))
            off = off + jnp.sum(eqf.astype(jnp.float32), axis=-1, keepdims=True)
        off_ref[...] = off

        av_ones = jnp.concatenate([avc, jnp.ones_like(avc)], axis=1)
        for h in range(A_HEADS):
            for rh in range(tq // att_rows):
                rs = slice(rh * att_rows, (rh + 1) * att_rows)
                logits = lax.dot_general(aq_ref[0, rs, h * LANES:(h + 1) * LANES], akc,
                                         (((1,), (1,)), ((), ())), preferred_element_type=jnp.float32)
                lg = [logits[:, s * LANES:(s + 1) * LANES] + bias[s][rs] for s in range(nslab)]
                m_old = m_ref[h, rs, :]
                m_cur = functools.reduce(jnp.maximum, lg)
                m_new = jnp.maximum(m_old, jnp.max(m_cur, axis=-1, keepdims=True))
                alpha = jnp.exp2(m_old - m_new)
                p = jnp.concatenate([jnp.exp2((x - m_new).astype(jnp.bfloat16)) for x in lg], axis=1)
                m_ref[h, rs, :] = m_new
                pv = jnp.dot(p, av_ones, preferred_element_type=jnp.float32)
                acc_ref[h, rs, :] = jnp.concatenate([alpha, alpha], axis=1) * acc_ref[h, rs, :] + pv
        return carry

    lax.fori_loop(0, nchunks, attn_chunk, 0)

    low = lax.broadcasted_iota(jnp.int32, (tq, LANES), 1) < HEAD_DIM
    for p in range(group):
        outs = [acc_ref[h, :, :LANES] / acc_ref[h, :, LANES:] for h in (p, group + p)]
        o_ref[0, :, p * LANES:(p + 1) * LANES] = jnp.where(low, outs[0], outs[1]).astype(o_ref.dtype)


def _dsa(p3, tri):
    b, s, _ = p3.shape
    tq = min(256, s)
    kc = min(512, s)
    topk = min(TOPK_MAX, s // 4)
    depth = 12
    kern = functools.partial(_dsa_kernel, tq=tq, kc=kc, topk=topk, depth=depth,
                             ins_rows=16, bis_rows=min(64, tq), att_rows=tq)
    wide = A_HEADS * LANES
    return pl.pallas_call(
        kern,
        grid=(b, s // tq),
        in_specs=[
            pl.BlockSpec((1, tq, wide), lambda bi, i: (bi, i, T_AQ)),
            pl.BlockSpec((1, tq, PROJ_TILE), lambda bi, i: (bi, i, T_IQ)),
            pl.BlockSpec((1, tq, LANES), lambda bi, i: (bi, i, SLAB_IW)),
            pl.BlockSpec((1, s, LANES), lambda bi, i: (bi, 0, SLAB_IK)),
            pl.BlockSpec((1, s, LANES), lambda bi, i: (bi, 0, SLAB_AK)),
            pl.BlockSpec((1, s, LANES), lambda bi, i: (bi, 0, SLAB_AV)),
            pl.BlockSpec(tri.shape, lambda bi, i: (0, 0)),
        ],
        out_specs=pl.BlockSpec((1, tq, wide // 2), lambda bi, i: (bi, i, 0)),
        out_shape=jax.ShapeDtypeStruct((b, s, wide // 2), jnp.bfloat16),
        scratch_shapes=[
            pltpu.VMEM((s // kc, tq, kc), jnp.float32),
            pltpu.VMEM((tq, depth * LANES), jnp.float32),
            pltpu.VMEM((tq, depth * LANES), jnp.bfloat16),
            pltpu.VMEM((tq, LANES), jnp.float32),
            pltpu.VMEM((tq, LANES), jnp.float32),
            pltpu.VMEM((tq, LANES), jnp.float32),
            pltpu.VMEM((A_HEADS, tq, LANES), jnp.float32),
            pltpu.VMEM((A_HEADS, tq, 2 * LANES), jnp.float32),
        ],
        compiler_params=_cparams(("parallel", "arbitrary")),
        name="dsa",
    )(p3, p3, p3, p3, p3, p3, tri)


def _dil_kernel(q_ref, kp_ref, kc_ref, vp_ref, vc_ref, o_ref, lse_ref, *, tu, w):
    u = pl.program_id(2)
    qq = lax.broadcasted_iota(jnp.int32, (2 * w, 2 * w), 0) % w
    kk = lax.broadcasted_iota(jnp.int32, (2 * w, 2 * w), 1)
    band = (kk >= qq) & (kk <= qq + w)
    band_first = band & (kk >= jnp.where(u > 0, 0, w))
    lane = lax.broadcasted_iota(jnp.int32, (w, LANES), 1)
    low = lane < HEAD_DIM
    for sb in range(tu // w):
        rows = slice(sb * w, (sb + 1) * w)
        lse_tile = jnp.zeros((w, LANES), jnp.float32)
        for pr in range(B_HEADS // 2):
            sl = slice(pr * LANES, (pr + 1) * LANES)
            q = q_ref[0, 0, rows, sl]
            if sb == 0:
                k = jnp.concatenate([kp_ref[0, 0, :, sl], kc_ref[0, 0, :w, sl]], axis=0)
                v = jnp.concatenate([vp_ref[0, 0, :, sl], vc_ref[0, 0, :w, sl]], axis=0)
            else:
                k = kc_ref[0, 0, (sb - 1) * w:(sb + 1) * w, sl]
                v = vc_ref[0, 0, (sb - 1) * w:(sb + 1) * w, sl]
            zero = jnp.zeros_like(q)
            qm = jnp.concatenate([jnp.where(low, q, zero), jnp.where(low, zero, q)], axis=0)
            lg = lax.dot_general(qm, k, (((1,), (1,)), ((), ())), preferred_element_type=jnp.float32)
            lg = jnp.where(band_first if sb == 0 else band, lg, NEG_BIG)
            m = jnp.max(lg, axis=-1, keepdims=True)
            p = jnp.exp2(lg - m).astype(jnp.bfloat16)
            nd = jnp.dot(p, jnp.concatenate([v, jnp.ones_like(v)], axis=1),
                         preferred_element_type=jnp.float32)
            l = nd[:, LANES:]
            o = nd[:, :LANES] / l
            o_ref[0, 0, rows, sl] = jnp.where(low, o[:w], o[w:]).astype(o_ref.dtype)
            lse = (m + jnp.log2(l)) * math.log(2.0)
            lse_tile = jnp.where(lane == 2 * pr, lse[:w], lse_tile)
            lse_tile = jnp.where(lane == 2 * pr + 1, lse[w:], lse_tile)
        lse_ref[0, 0, rows, :] = lse_tile


def _dilated(src, tq_tile, tk_tile, tv_tile, gi):
    b, dil, su, _ = src.shape
    window, d2 = B_PATTERNS[gi]
    assert d2 == dil
    w = window // dil
    tu = min(512, su)
    ratio = tu // w
    width = B_HEADS * HEAD_DIM
    cur = lambda t: pl.BlockSpec((1, 1, tu, width), lambda bi, r, u: (bi, r, u, t))
    prev = lambda t: pl.BlockSpec((1, 1, w, width),
                                  lambda bi, r, u: (bi, r, jnp.maximum(u * ratio - 1, 0), t))
    return pl.pallas_call(
        functools.partial(_dil_kernel, tu=tu, w=w),
        grid=(b, dil, su // tu),
        in_specs=[cur(tq_tile), prev(tk_tile), cur(tk_tile), prev(tv_tile), cur(tv_tile)],
        out_specs=[pl.BlockSpec((1, 1, tu, width), lambda bi, r, u: (bi, r, u, 0)),
                   pl.BlockSpec((1, 1, tu, LANES), lambda bi, r, u: (bi, r, u, 0))],
        out_shape=[jax.ShapeDtypeStruct((b, dil, su, width), jnp.bfloat16),
                   jax.ShapeDtypeStruct((b, dil, su, LANES), jnp.float32)],
        compiler_params=_cparams(("parallel", "parallel", "arbitrary")),
        name=f"dilated{gi}",
    )(src, src, src, src, src)


def _mem_attn_kernel(q_ref, k_ref, v_ref, o_ref):
    scale = M_HEAD_DIM ** -0.5
    for h in range(M_HEADS):
        sl = slice(h * M_HEAD_DIM, (h + 1) * M_HEAD_DIM)
        lg = lax.dot_general(q_ref[0, :, sl], k_ref[0, :, sl], (((1,), (1,)), ((), ())),
                             preferred_element_type=jnp.float32) * scale
        m = jnp.max(lg, axis=-1, keepdims=True)
        p = jnp.exp(lg - m)
        l = jnp.sum(p, axis=-1, keepdims=True)
        o = jnp.dot(p.astype(jnp.bfloat16), v_ref[0, :, sl], preferred_element_type=jnp.float32)
        o_ref[0, :, sl] = (o / l).astype(o_ref.dtype)


def _mem_attn(p3, kv):
    b, s, _ = p3.shape
    m = kv.shape[1]
    width = M_HEADS * M_HEAD_DIM
    tq = min(512, s)
    return pl.pallas_call(
        _mem_attn_kernel,
        grid=(b, s // tq),
        in_specs=[
            pl.BlockSpec((1, tq, width), lambda bi, i: (bi, i, T_MQ)),
            pl.BlockSpec((1, m, width), lambda bi, i: (bi, 0, 0)),
            pl.BlockSpec((1, m, width), lambda bi, i: (bi, 0, 1)),
        ],
        out_specs=pl.BlockSpec((1, tq, width), lambda bi, i: (bi, i, 0)),
        out_shape=jax.ShapeDtypeStruct((b, s, width), jnp.bfloat16),
        compiler_params=_cparams(("parallel", "parallel")),
        name="mem_attn",
    )(p3, kv, kv)


def _split_bf16(v):
    hi = v.astype(jnp.bfloat16)
    lo = (v - hi.astype(jnp.float32)).astype(jnp.bfloat16)
    return hi, lo


def _merge_kernel(x_ref, gmix_ref, wg_ref, bg_ref, ya_ref, wa_ref, ob0_ref, ob1_ref, ob2_ref,
                  ls0_ref, ls1_ref, ls2_ref, wb_ref, ym_ref, wm_ref, wo_ref, gffn_ref,
                  wr2_ref, h_ref, xn_ref, rl_ref, ls_scr, ob_scr):
    d = x_ref.shape[2]
    tm = x_ref.shape[1]
    x = x_ref[0]
    n = _rms(x, gmix_ref[...]).astype(jnp.bfloat16)

    def gate(k):
        z = jnp.dot(n, wg_ref[:, k * d:(k + 1) * d], preferred_element_type=jnp.float32)
        return jax.nn.sigmoid(z + bg_ref[:, k * d:(k + 1) * d])

    merged = gate(0) * jnp.dot(ya_ref[0], wa_ref[...], preferred_element_type=jnp.float32)

    ob_refs = (ob0_ref, ob1_ref, ob2_ref)
    ls = []
    for gi, ls_ref in enumerate((ls0_ref, ls1_ref, ls2_ref)):
        dil = ls_ref.shape[1]
        if dil == 1:
            ls.append(ls_ref[0, 0])
        else:
            for r in range(dil):
                ls_scr[gi - 1, pl.ds(r, tm // dil, stride=dil), :] = ls_ref[0, r]
            ls.append(ls_scr[gi - 1])
    mx = jnp.maximum(jnp.maximum(ls[0], ls[1]), ls[2])
    es = [jnp.exp(v - mx) for v in ls]
    inv = 1.0 / (es[0] + es[1] + es[2])
    yb = [jnp.zeros((tm, LANES), jnp.float32) for _ in range(SLABS)]
    low = lax.broadcasted_iota(jnp.int32, (tm, LANES), 1) < HEAD_DIM
    for e, ob in zip(es, ob_refs):
        dil = ob.shape[1]
        a = e * inv
        for s in range(SLABS):
            sl = slice(s * LANES, (s + 1) * LANES)
            if dil == 1:
                o = ob[0, 0, :, sl].astype(jnp.float32)
            else:
                for r in range(dil):
                    ob_scr[s, pl.ds(r, tm // dil, stride=dil), :] = ob[0, r, :, sl].astype(jnp.float32)
                o = ob_scr[s]
            a_s = jnp.where(low, jnp.broadcast_to(a[:, 2 * s:2 * s + 1], (tm, LANES)),
                            jnp.broadcast_to(a[:, 2 * s + 1:2 * s + 2], (tm, LANES)))
            yb[s] = yb[s] + a_s * o
    yb = jnp.concatenate(yb, axis=1)
    merged = merged + gate(1) * jnp.dot(yb.astype(jnp.bfloat16), wb_ref[...],
                                        preferred_element_type=jnp.float32)
    merged = merged + gate(2) * jnp.dot(ym_ref[0], wm_ref[...], preferred_element_type=jnp.float32)

    h = x + jnp.dot(merged.astype(jnp.bfloat16), wo_ref[...], preferred_element_type=jnp.float32)
    h_ref[0] = h
    xn = _rms(h, gffn_ref[...])
    xn_ref[0] = xn.astype(jnp.bfloat16)
    hi, lo = _split_bf16(xn)
    rh = jnp.dot(hi, wr2_ref[...], preferred_element_type=jnp.float32)
    rlo = jnp.dot(lo, wr2_ref[...], preferred_element_type=jnp.float32)
    rl_ref[0] = rh[:, :LANES] + rh[:, LANES:] + rlo[:, :LANES]


def _merge(x3, g_mix, wg, bg, ya, wa, obs, lss, wb, ym, wm, wo, g_ffn, wr2):
    b, s, d = x3.shape
    tm = min(256, s)
    row = lambda w: pl.BlockSpec((1, tm, w), lambda bi, i: (bi, i, 0))
    full = lambda a: pl.BlockSpec(a.shape, lambda bi, i: (0,) * a.ndim)

    def dil_spec(a):
        dil, width = a.shape[1], a.shape[3]
        return pl.BlockSpec((1, dil, tm // dil, width), lambda bi, i: (bi, 0, i, 0))

    return pl.pallas_call(
        _merge_kernel,
        grid=(b, s // tm),
        in_specs=[row(d), full(g_mix), full(wg), full(bg), row(ya.shape[2]), full(wa),
                  dil_spec(obs[0]), dil_spec(obs[1]), dil_spec(obs[2]),
                  dil_spec(lss[0]), dil_spec(lss[1]), dil_spec(lss[2]), full(wb),
                  row(ym.shape[2]), full(wm), full(wo), full(g_ffn), full(wr2)],
        out_specs=[row(d), row(d), row(LANES)],
        out_shape=[jax.ShapeDtypeStruct((b, s, d), jnp.float32),
                   jax.ShapeDtypeStruct((b, s, d), jnp.bfloat16),
                   jax.ShapeDtypeStruct((b, s, LANES), jnp.float32)],
        scratch_shapes=[pltpu.VMEM((2, tm, LANES), jnp.float32),
                        pltpu.VMEM((SLABS, tm, LANES), jnp.float32)],
        compiler_params=_cparams(("parallel", "parallel")),
        name="merge",
    )(x3, g_mix, wg, bg, ya, wa, obs[0], obs[1], obs[2], lss[0], lss[1], lss[2], wb,
      ym, wm, wo, g_ffn, wr2)


def _first_lane_where(cond, lane):
    return jnp.min(jnp.where(cond, lane, float(LANES)), axis=-1, keepdims=True)


def _route(logits, rb):
    z = logits + rb
    lane = lax.broadcasted_iota(jnp.int32, z.shape, 1).astype(jnp.float32)
    is_g = lane < MOE_GROUPS
    zg = jnp.where(is_g, z, -jnp.inf)
    mg = jnp.max(zg, axis=-1, keepdims=True)
    eg = jnp.exp(zg - mg)
    gp = eg / jnp.sum(eg, axis=-1, keepdims=True)
    g_w = jnp.max(gp, axis=-1, keepdims=True)
    g_sel = _first_lane_where(is_g & (gp == g_w), lane)
    lo = R_SUB0 + g_sel * EXPERTS_PER_GROUP
    in_grp = (lane >= lo) & (lane < lo + EXPERTS_PER_GROUP)
    zs = jnp.where(in_grp, z, -jnp.inf)
    ms = jnp.max(zs, axis=-1, keepdims=True)
    es = jnp.exp(zs - ms)
    sp = es / jnp.sum(es, axis=-1, keepdims=True)
    p1 = jnp.max(sp, axis=-1, keepdims=True)
    i1 = _first_lane_where(in_grp & (sp == p1), lane)
    rest = in_grp & (lane != i1)
    sp2 = jnp.where(rest, sp, -1.0)
    p2 = jnp.max(sp2, axis=-1, keepdims=True)
    i2 = _first_lane_where(rest & (sp2 == p2), lane)
    tot = p1 + p2
    comb = jnp.where(lane == i1, g_w * (p1 / tot), jnp.where(lane == i2, g_w * (p2 / tot), 0.0))
    return comb, g_sel


MOE_BLK = 256
MOE_PAIR = 4


def _moe_kernel(xn_ref, rl_ref, rb_ref, ltri_ref, w1_ref, w3_ref, w2_ref, o_ref,
                xs_ref, cs_ref, acc_ref, perm_ref, permt_ref, blk_ref):
    pr = pl.program_id(1)
    tm = xn_ref.shape[0]

    @pl.when(pr == 0)
    def _():
        comb, g_sel = _route(rl_ref[...], rb_ref[...])
        lane = lax.broadcasted_iota(jnp.int32, (tm, LANES), 1).astype(jnp.float32)
        onehot = jnp.where(lane == g_sel, 1.0, 0.0)
        seen = jnp.dot(ltri_ref[...], onehot.astype(jnp.bfloat16), preferred_element_type=jnp.float32)
        counts = seen[tm - 1:tm, :]
        lane_row = lane[0:1, :]
        offs, off = [], 0.0
        off_row = jnp.zeros((1, LANES), jnp.float32)
        for g in range(MOE_GROUPS):
            n_g = jnp.sum(jnp.where(lane_row == g, counts, 0.0))
            off_row = jnp.where(lane_row == g, off, off_row)
            first = off.astype(jnp.int32) if g else jnp.int32(0)
            start = (first // 16) * 16
            blk_ref[g] = start
            blk_ref[MOE_GROUPS + g] = (first - start + n_g.astype(jnp.int32) + MOE_BLK - 1) // MOE_BLK
            off = off + n_g
        dest = jnp.sum(onehot * (off_row + seen - 1.0), axis=-1, keepdims=True)
        dest_row = jnp.transpose(jnp.broadcast_to(dest, (tm, LANES)))[0:1, :]
        row = lax.broadcasted_iota(jnp.int32, (tm, LANES), 0).astype(jnp.float32)
        for s in range(tm // LANES):
            sl = slice(s * LANES, (s + 1) * LANES)
            permt_ref[:, sl] = jnp.where(lane + float(s * LANES) == dest, 1.0, 0.0).astype(jnp.bfloat16)
            perm_ref[:, sl] = jnp.where(row == dest_row[:, sl], 1.0, 0.0).astype(jnp.bfloat16)
        perm = perm_ref[...]
        xs_ref[0:tm, :] = jnp.dot(perm, xn_ref[...], preferred_element_type=jnp.float32).astype(jnp.bfloat16)
        hi, lo = _split_bf16(comb)
        cs_ref[0:tm, :] = (jnp.dot(perm, hi, preferred_element_type=jnp.float32)
                           + jnp.dot(perm, lo, preferred_element_type=jnp.float32))
        xs_ref[tm:, :] = jnp.zeros((MOE_BLK, xs_ref.shape[1]), jnp.bfloat16)
        cs_ref[tm:, :] = jnp.zeros((MOE_BLK, LANES), jnp.float32)
        acc_ref[...] = jnp.zeros_like(acc_ref)

    g = pr // (EXPERTS_PER_GROUP // MOE_PAIR)
    start = blk_ref[g]
    lane_i = lax.broadcasted_iota(jnp.int32, (MOE_BLK, LANES), 1)

    def block(b, carry):
        r0 = pl.multiple_of(start + b * MOE_BLK, 16)
        rows = pl.ds(r0, MOE_BLK)
        xb = xs_ref[rows, :]
        cb = cs_ref[rows, :]
        y = jnp.zeros((MOE_BLK, o_ref.shape[1]), jnp.float32)
        for e2 in range(MOE_PAIR):
            c = jnp.sum(jnp.where(lane_i == R_SUB0 + MOE_PAIR * pr + e2, cb, 0.0), axis=-1, keepdims=True)
            a = jnp.dot(xb, w1_ref[e2], preferred_element_type=jnp.float32)
            u = jnp.dot(xb, w3_ref[e2], preferred_element_type=jnp.float32)
            hid = (jax.nn.silu(a) * u * c).astype(jnp.bfloat16)
            y = y + jnp.dot(hid, w2_ref[e2], preferred_element_type=jnp.float32)
        acc_ref[rows, :] += y
        return carry

    lax.fori_loop(0, blk_ref[MOE_GROUPS + g], block, 0)

    @pl.when(pr == N_EXPERTS // MOE_PAIR - 1)
    def _():
        o_ref[...] = jnp.dot(permt_ref[...], acc_ref[0:tm, :].astype(jnp.bfloat16),
                             preferred_element_type=jnp.float32).astype(o_ref.dtype)


def _moe(xn, rl, rb, w1, w3, w2):
    t, d = xn.shape
    hid = w1.shape[2]
    tm = min(1024, t)
    ltri = (jnp.arange(tm)[:, None] >= jnp.arange(tm)[None, :]).astype(jnp.bfloat16)
    return pl.pallas_call(
        _moe_kernel,
        grid=(t // tm, N_EXPERTS // MOE_PAIR),
        in_specs=[
            pl.BlockSpec((tm, d), lambda i, p: (i, 0)),
            pl.BlockSpec((tm, LANES), lambda i, p: (i, 0)),
            pl.BlockSpec((1, LANES), lambda i, p: (0, 0)),
            pl.BlockSpec((tm, tm), lambda i, p: (0, 0)),
            pl.BlockSpec((MOE_PAIR, d, hid), lambda i, p: (p, 0, 0)),
            pl.BlockSpec((MOE_PAIR, d, hid), lambda i, p: (p, 0, 0)),
            pl.BlockSpec((MOE_PAIR, hid, d), lambda i, p: (p, 0, 0)),
        ],
        out_specs=pl.BlockSpec((tm, d), lambda i, p: (i, 0)),
        out_shape=jax.ShapeDtypeStruct((t, d), jnp.bfloat16),
        scratch_shapes=[
            pltpu.VMEM((tm + MOE_BLK, d), jnp.bfloat16),
            pltpu.VMEM((tm + MOE_BLK, LANES), jnp.float32),
            pltpu.VMEM((tm + MOE_BLK, d), jnp.float32),
            pltpu.VMEM((tm, tm), jnp.bfloat16),
            pltpu.VMEM((tm, tm), jnp.bfloat16),
            pltpu.SMEM((2 * MOE_GROUPS,), jnp.int32),
        ],
        compiler_params=_cparams(("parallel", "arbitrary")),
        name="moe",
    )(xn, rl, rb, ltri, w1, w3, w2)


def _final_kernel(h_ref, m_ref, g_ref, o_ref):
    o_ref[...] = _rms(h_ref[...] + m_ref[...].astype(jnp.float32), g_ref[...])


def _final(h, moe, g_final):
    t, d = h.shape
    tm = min(512, t)
    return pl.pallas_call(
        _final_kernel,
        grid=(t // tm,),
        in_specs=[pl.BlockSpec((tm, d), lambda i: (i, 0)), pl.BlockSpec((tm, d), lambda i: (i, 0)),
                  pl.BlockSpec((1, d), lambda i: (0, 0))],
        out_specs=pl.BlockSpec((tm, d), lambda i: (i, 0)),
        out_shape=jax.ShapeDtypeStruct((t, d), jnp.float32),
        compiler_params=_cparams(("parallel",)),
        name="final_norm",
    )(h, moe, g_final)


def _pad_heads_kv(w):
    d = w.shape[0]
    group = A_HEADS // A_KV_HEADS
    w = w.reshape(d, A_KV_HEADS, group, HEAD_DIM)
    parts = [jnp.pad(w[:, j], ((0, 0), (0, 0), (j * HEAD_DIM, LANES - (j + 1) * HEAD_DIM)))
             for j in range(A_KV_HEADS)]
    return jnp.concatenate(parts, axis=1).reshape(d, A_HEADS * LANES)


def _pack_w_in(w_in):
    parts, off = [], 0
    for n in IN_SPLITS:
        parts.append(w_in[:, off:off + n])
        off += n
    aq, ak, av, iq, ik, iw, bq, bk, bv, mq = parts
    d = w_in.shape[0]
    qscale = HEAD_DIM ** -0.5
    aq_x = _pad_heads_kv(aq * (qscale * math.log2(math.e)))
    iq_x = jnp.pad((iq * qscale).reshape(d, IDX_HEADS, HEAD_DIM),
                   ((0, 0), (0, 0), (0, LANES - HEAD_DIM))).reshape(d, IDX_HEADS * LANES)
    ik_x = jnp.pad(ik, ((0, 0), (0, LANES - ik.shape[1])))
    iw2 = jnp.concatenate([iw, iw], axis=1) * (IDX_HEADS ** -0.5)
    iw_x = jnp.pad(iw2, ((0, 0), (0, LANES - iw2.shape[1])))
    width = B_HEADS * HEAD_DIM
    bqs = bq * (qscale * math.log2(math.e))
    grp = lambda a, gi: a[:, gi * width:(gi + 1) * width]
    nat = jnp.concatenate([aq_x, iq_x, grp(bqs, 0), grp(bk, 0), ak, ik_x, av, iw_x, grp(bv, 0), mq], axis=1)
    dil = [jnp.concatenate([grp(bqs, gi), grp(bk, gi), grp(bv, gi)], axis=1) for gi in (1, 2)]
    return nat, dil


def _rope_tables(seq):
    half = HEAD_DIM // 2
    inv = ROPE_THETA ** (-jnp.arange(half, dtype=jnp.float32) / half)
    ang = jnp.arange(seq, dtype=jnp.float32)[:, None] * inv[None, :]
    cos = jnp.tile(jnp.cos(ang), (1, LANES // half))
    sign = jnp.tile(jnp.concatenate([-jnp.ones((half,), jnp.float32), jnp.ones((half,), jnp.float32)]),
                    LANES // HEAD_DIM)
    sin = jnp.tile(jnp.sin(ang), (1, LANES // half)) * sign[None, :]
    return cos, sin


def _layer(x3, mem, g_mix, g_mem, w_in, w_mem_kv, w_gate, b_gate, w_branch, w_out, g_ffn,
           w_group, b_group, w_sub, b_sub, w1, w3, w2, g_out):
    b, s, d = x3.shape
    bf = jnp.bfloat16
    cos_t, sin_t = _rope_tables(s)
    w_nat, w_dil = _pack_w_in(w_in)
    gm = g_mix[None, :]

    p4 = _in_proj(x3, gm, w_nat.astype(bf), cos_t, sin_t, P_MODES, 1)
    p3 = p4.reshape(b, s, p4.shape[3])
    srcs = [p4] + [_in_proj(x3, gm, w.astype(bf), cos_t, sin_t, D_MODES, B_PATTERNS[gi + 1][1])
                   for gi, w in enumerate(w_dil)]

    kv = _mem_kv(mem, g_mem[None, :], w_mem_kv.astype(bf))

    half = 256
    tri = (jnp.arange(half)[:, None] <= jnp.arange(half)[None, :]).astype(bf)
    ya = _dsa(p3, tri)

    obs, lss = [], []
    for gi, src in enumerate(srcs):
        tiles = (T_BQ, T_BK, T_BV) if gi == 0 else (0, 1, 2)
        o, l = _dilated(src, *tiles, gi)
        obs.append(o)
        lss.append(l)

    ym = _mem_attn(p3, kv)

    group = A_HEADS // A_KV_HEADS
    wa = jnp.swapaxes(w_branch[0].reshape(A_KV_HEADS, group, HEAD_DIM, d), 0, 1).reshape(-1, d).astype(bf)
    w_route = jnp.concatenate([w_group, jnp.moveaxis(w_sub, 0, 1).reshape(d, N_EXPERTS)], axis=1)
    w_route = jnp.pad(w_route, ((0, 0), (0, LANES - w_route.shape[1])))
    wrh = w_route.astype(bf)
    wrl = (w_route - wrh.astype(jnp.float32)).astype(bf)
    wr2 = jnp.concatenate([wrh, wrl], axis=1)
    r_bias = jnp.pad(jnp.concatenate([b_group, b_sub.reshape(-1)]), (0, LANES - MOE_GROUPS - N_EXPERTS))

    h, xn, rl = _merge(x3, gm, w_gate.astype(bf), b_gate[None, :], ya, wa, obs, lss,
                       w_branch[1].astype(bf), ym, w_branch[2].astype(bf), w_out.astype(bf),
                       g_ffn[None, :], wr2)
    t = b * s
    moe = _moe(xn.reshape(t, d), rl.reshape(t, LANES), r_bias[None, :],
               w1.astype(bf), w3.astype(bf), w2.astype(bf))
    return _final(h.reshape(t, d), moe, g_out[None, :]).reshape(b, s, d)


def kernel(x, mem, g_mix, g_mem, w_in, w_mem_kv, w_gate, b_gate, w_branch, w_out, g_ffn,
           w_group, b_group, w_sub, b_sub, w1, w3, w2, g_final):
    depth = g_mix.shape[0]
    assert depth == 1, "the final rmsnorm is fused into the single layer's last kernel"
    return _layer(x, mem, g_mix[0], g_mem[0], w_in[0], w_mem_kv[0], w_gate[0], b_gate[0], w_branch[0],
                  w_out[0], g_ffn[0], w_group[0], b_group[0], w_sub[0], b_sub[0], w1[0], w3[0], w2[0],
                  g_final)
```

```python
import functools
import math

import jax
import jax.numpy as jnp
from jax import lax
from jax.experimental import pallas as pl
from jax.experimental.pallas import tpu as pltpu

HEAD_DIM = 64
ROPE_THETA = 10000.0
RMS_EPS = 1e-6
A_HEADS = 8
A_KV_HEADS = 2
IDX_HEADS = 4
TOPK_MAX = 256
B_PATTERNS = ((128, 1), (512, 4), (2048, 16))
B_HEADS = 8
M_HEADS = 4
M_HEAD_DIM = 128
MOE_GROUPS = 4
EXPERTS_PER_GROUP = 4
N_EXPERTS = 16
IN_SPLITS = (512, 128, 128, 256, 64, 4, 1536, 1536, 1536, 512)

LANES = 128
PROJ_TILE = 512
SLABS = PROJ_TILE // LANES
VMEM_LIMIT = 56 * 1024 * 1024

T_AQ, T_IQ, T_BQ, T_BK, T_MIX, T_BV, T_MQ = 0, 2, 3, 4, 5, 6, 7
P_MODES = (("rope",) * 4,) * 5 + (("rope", "rope", "plain", "hi_lo"),) + (("plain",) * 4,) * 2
SLAB_AK = T_MIX * SLABS + 0
SLAB_IK = T_MIX * SLABS + 1
SLAB_AV = T_MIX * SLABS + 2
SLAB_IW = T_MIX * SLABS + 3
D_MODES = (("rope",) * 4, ("rope",) * 4, ("plain",) * 4)

INT_MIN = -(2 ** 31)
NEG_BIG = -1e30

R_SUB0 = MOE_GROUPS

MOE_BLK = 256
MOE_STEP = 4


def _cparams(sem):
    return pltpu.CompilerParams(dimension_semantics=sem, vmem_limit_bytes=VMEM_LIMIT)


def _rms(xf, g):
    return xf * lax.rsqrt(jnp.mean(xf * xf, axis=-1, keepdims=True) + RMS_EPS) * g


def _rope_slab(y, cos, sin_signed, first_half):
    partner = jnp.where(first_half, pltpu.roll(y, 96, 1), pltpu.roll(y, 32, 1))
    return y * cos + partner * sin_signed


def _in_proj_kernel(x_ref, g_ref, w_ref, cos_ref, sin_ref, o_ref, n_ref, *stage, modes, dil, rows):
    j = pl.program_id(2)
    tm = x_ref.shape[1]

    @pl.when(j == 0)
    def _():
        n_ref[...] = _rms(x_ref[0], g_ref[...]).astype(jnp.bfloat16)

    lane = lax.broadcasted_iota(jnp.int32, (rows, LANES), 1)
    first_half = (lane & (HEAD_DIM - 1)) < (HEAD_DIM // 2)

    def epilogue(slab_modes):
        for r in range(tm // rows):
            rs = pl.ds(r * rows, rows)
            acc = jnp.dot(n_ref[rs, :], w_ref[...], preferred_element_type=jnp.float32)
            cos = cos_ref[rs, :]
            sin = sin_ref[rs, :]
            for s in range(SLABS):
                y = acc[:, s * LANES:(s + 1) * LANES]
                if slab_modes[s] == "rope":
                    y = _rope_slab(y, cos, sin, first_half)
                elif slab_modes[s] == "hi_lo":
                    resid = y - y.astype(jnp.bfloat16).astype(jnp.float32)
                    y = jnp.where(lane < IDX_HEADS, y, resid)
                if dil == 1:
                    o_ref[0, 0, rs, s * LANES:(s + 1) * LANES] = y.astype(o_ref.dtype)
                else:
                    stage[0][s, rs, :] = y
        if dil > 1:
            for r in range(dil):
                for s in range(SLABS):
                    o_ref[0, r, :, s * LANES:(s + 1) * LANES] = (
                        stage[0][s, pl.ds(r, tm // dil, stride=dil), :].astype(o_ref.dtype))

    for pattern in sorted(set(modes)):
        tiles = [t for t, m in enumerate(modes) if m == pattern]
        cond = functools.reduce(jnp.logical_or, [j == t for t in tiles])
        pl.when(cond)(functools.partial(epilogue, pattern))


def _in_proj(x3, g_mix, wp, cos_t, sin_t, modes, dil):
    b, s, d = x3.shape
    tm = min(1024, s)
    ntiles = len(modes)
    scratch = [pltpu.VMEM((tm, d), jnp.bfloat16)]
    if dil > 1:
        scratch.append(pltpu.VMEM((SLABS, tm, LANES), jnp.float32))
    return pl.pallas_call(
        functools.partial(_in_proj_kernel, modes=modes, dil=dil, rows=256),
        grid=(b, s // tm, ntiles),
        in_specs=[
            pl.BlockSpec((1, tm, d), lambda bi, i, j: (bi, i, 0)),
            pl.BlockSpec((1, d), lambda bi, i, j: (0, 0)),
            pl.BlockSpec((d, PROJ_TILE), lambda bi, i, j: (0, j)),
            pl.BlockSpec((tm, LANES), lambda bi, i, j: (i, 0)),
            pl.BlockSpec((tm, LANES), lambda bi, i, j: (i, 0)),
        ],
        out_specs=pl.BlockSpec((1, dil, tm // dil, PROJ_TILE), lambda bi, i, j: (bi, 0, i, j)),
        out_shape=jax.ShapeDtypeStruct((b, dil, s // dil, ntiles * PROJ_TILE), jnp.bfloat16),
        scratch_shapes=scratch,
        compiler_params=_cparams(("parallel", "parallel", "arbitrary")),
        name=f"in_proj_d{dil}",
    )(x3, g_mix, wp, cos_t, sin_t)


def _mem_kv_kernel(m_ref, g_ref, w_ref, o_ref):
    n = _rms(m_ref[0], g_ref[...]).astype(jnp.bfloat16)
    o_ref[0] = jnp.dot(n, w_ref[...], preferred_element_type=jnp.float32).astype(o_ref.dtype)


def _mem_kv(mem, g_mem, w_kv):
    b, m, d = mem.shape
    n = w_kv.shape[1]
    return pl.pallas_call(
        _mem_kv_kernel,
        grid=(b,),
        in_specs=[
            pl.BlockSpec((1, m, d), lambda i: (i, 0, 0)),
            pl.BlockSpec((1, d), lambda i: (0, 0)),
            pl.BlockSpec((d, n), lambda i: (0, 0)),
        ],
        out_specs=pl.BlockSpec((1, m, n), lambda i: (i, 0, 0)),
        out_shape=jax.ShapeDtypeStruct((b, m, n), jnp.bfloat16),
        compiler_params=_cparams(("parallel",)),
        name="mem_kv",
    )(mem, g_mem, w_kv)


def _key_to_float(key):
    bits = jnp.where(key < 0, key ^ jnp.int32(0x7FFFFFFF), key)
    f = lax.bitcast_convert_type(bits, jnp.float32)
    return jnp.where((key < 0) & (f != f), -jnp.inf, f)


def _dsa_kernel(aq_ref, iq_ref, iw_ref, ik_ref, ak_ref, av_ref, tri_ref, o_ref,
                sc_ref, lists_ref, key_ref, thr_ref, need_ref, off_ref, m_ref, acc_ref,
                *, tq, kc, topk, depth, ins_rows, bis_rows, steps_per_chunk):
    i = pl.program_id(1)
    nt = pl.num_programs(1) - 1
    slot = i % 2
    pslot = 1 - slot
    nslab = kc // LANES
    group = A_HEADS // A_KV_HEADS
    nbits = 32
    blocks = [pl.ds(rb * bis_rows, bis_rows) for rb in range(tq // bis_rows)]

    def chunks_of(tile):
        return (tile * tq + tq + kc - 1) // kc

    @pl.when(i < nt)
    def _():
        q0 = i * tq
        iw = iw_ref[0].astype(jnp.float32)
        w_heads = [iw[:, h:h + 1] + iw[:, IDX_HEADS + h:IDX_HEADS + h + 1] for h in range(IDX_HEADS)]
        qpos = q0 + lax.broadcasted_iota(jnp.int32, (tq, kc), 0)
        kiota = lax.broadcasted_iota(jnp.int32, (tq, kc), 1)

        def score_chunk(c, carry):
            k0 = pl.multiple_of(c * kc, kc)
            ikc = ik_ref[0, pl.ds(k0, kc), :]
            s = jnp.zeros((tq, kc), jnp.float32)
            for h in range(IDX_HEADS):
                d = lax.dot_general(iq_ref[0, :, h * LANES:(h + 1) * LANES], ikc,
                                    (((1,), (1,)), ((), ())), preferred_element_type=jnp.float32)
                s = s + jnp.maximum(d, 0.0) * w_heads[h]
            s = jnp.where(kiota + k0 <= qpos, s, -jnp.inf)
            sc_ref[slot, c] = s
            for rg in range(tq // ins_rows):
                rows = slice(rg * ins_rows, (rg + 1) * ins_rows)
                tops = [lists_ref[rows, j * LANES:(j + 1) * LANES] for j in range(depth)]
                for sl in range(nslab):
                    x = s[rows, sl * LANES:(sl + 1) * LANES]
                    for j in range(depth):
                        hi = jnp.maximum(tops[j], x)
                        x = jnp.minimum(tops[j], x)
                        tops[j] = hi
                for j in range(depth):
                    lists_ref[rows, j * LANES:(j + 1) * LANES] = tops[j]
            return carry

        lists_ref[...] = jnp.full(lists_ref.shape, -jnp.inf, jnp.float32)
        lax.fori_loop(0, chunks_of(i), score_chunk, 0)
        key_ref[...] = jnp.full(key_ref.shape, INT_MIN, jnp.int32)

    def list_count(rs, cand, strict):
        acc = jnp.zeros((bis_rows, LANES), jnp.float32)
        for j in range(depth):
            v = lists_ref[rs, j * LANES:(j + 1) * LANES]
            acc = acc + jnp.where((v > cand) if strict else (v >= cand), 1.0, 0.0)
        return jnp.sum(acc, axis=-1, keepdims=True)

    def bit_step(it):
        for rs in blocks:
            pre = key_ref[rs, :]
            cand = pre + lax.shift_left(jnp.int32(1), nbits - 1 - it)
            cnt = list_count(rs, _key_to_float(cand), False)
            key_ref[rs, :] = jnp.where(cnt >= float(topk), cand, pre)

    half = tri_ref.shape[0]

    def attn_chunk(c):
        k0 = pl.multiple_of(c * kc, kc)
        akc = ak_ref[0, pl.ds(k0, kc), :]
        avc = av_ref[0, pl.ds(k0, kc), :]
        thr_b = thr_ref[pslot]
        need_b = need_ref[pslot]
        bias = []
        off = off_ref[...]
        for hh in range(kc // half):
            eq_parts = []
            for s in range(half // LANES):
                v = sc_ref[pslot, c, :, hh * half + s * LANES: hh * half + (s + 1) * LANES]
                eq_parts.append(jnp.where(v == thr_b, 1.0, 0.0).astype(jnp.bfloat16))
            eqf = jnp.concatenate(eq_parts, axis=1)
            pre = jnp.dot(eqf, tri_ref[...], preferred_element_type=jnp.float32)
            for s in range(half // LANES):
                v = sc_ref[pslot, c, :, hh * half + s * LANES: hh * half + (s + 1) * LANES]
                rank = pre[:, s * LANES:(s + 1) * LANES] + off
                tie_ok = (v == thr_b) & (rank <= need_b)
                sel = (v > thr_b) | tie_ok
                bias.append(jnp.where(sel, 0.0, NEG_BIG))
            off = off + jnp.sum(eqf.astype(jnp.float32), axis=-1, keepdims=True)
        off_ref[...] = off

        av_ones = jnp.concatenate([avc, jnp.ones_like(avc)], axis=1)
        for h in range(A_HEADS):
            logits = lax.dot_general(aq_ref[0, :, h * LANES:(h + 1) * LANES], akc,
                                     (((1,), (1,)), ((), ())), preferred_element_type=jnp.float32)
            lg = [logits[:, s * LANES:(s + 1) * LANES] + bias[s] for s in range(nslab)]
            m_old = m_ref[h]
            m_cur = functools.reduce(jnp.maximum, lg)
            m_new = jnp.maximum(m_old, jnp.max(m_cur, axis=-1, keepdims=True))
            alpha = jnp.exp2(m_old - m_new)
            p = jnp.concatenate([jnp.exp2((x - m_new).astype(jnp.bfloat16)) for x in lg], axis=1)
            m_ref[h] = m_new
            pv = jnp.dot(p, av_ones, preferred_element_type=jnp.float32)
            acc_ref[h] = jnp.concatenate([alpha, alpha], axis=1) * acc_ref[h] + pv

    nprev = chunks_of(i - 1)
    nride = jnp.minimum(nprev, nbits // steps_per_chunk)

    @pl.when(i >= 1)
    def _():
        off_ref[...] = jnp.zeros_like(off_ref)
        m_ref[...] = jnp.full(m_ref.shape, NEG_BIG, jnp.float32)
        acc_ref[...] = jnp.zeros_like(acc_ref)

        def chunk_with_steps(c, carry):
            attn_chunk(c)
            for j in range(steps_per_chunk):
                bit_step(c * steps_per_chunk + j)
            return carry

        def chunk_plain(c, carry):
            attn_chunk(c)
            return carry

        lax.fori_loop(0, nride, chunk_with_steps, 0)
        lax.fori_loop(nride, nprev, chunk_plain, 0)

        low = lax.broadcasted_iota(jnp.int32, (tq, LANES), 1) < HEAD_DIM
        for p in range(group):
            outs = [acc_ref[h, :, :LANES] / acc_ref[h, :, LANES:] for h in (p, group + p)]
            o_ref[0, :, p * LANES:(p + 1) * LANES] = jnp.where(low, outs[0], outs[1]).astype(o_ref.dtype)

    @pl.when(i < nt)
    def _():
        nchunks = chunks_of(i)
        done = jnp.where(i >= 1, nride * steps_per_chunk, 0)

        def rest(it, carry):
            bit_step(it)
            return carry

        lax.fori_loop(done, nbits, rest, 0)

        def finish(rs, thr, c_gt):
            thr_ref[slot, rs, :] = thr
            need_ref[slot, rs, :] = jnp.where(thr == -jnp.inf, 0.0, float(topk) - c_gt)

        def full_count(rs, cand, strict):
            def body(c, acc):
                for s in range(nslab):
                    v = sc_ref[slot, c, rs, s * LANES:(s + 1) * LANES]
                    acc = acc + jnp.where((v > cand) if strict else (v >= cand), 1.0, 0.0)
                return acc
            acc = lax.fori_loop(0, nchunks, body, jnp.zeros((bis_rows, LANES), jnp.float32))
            return jnp.sum(acc, axis=-1, keepdims=True)

        def redo_block(rs):
            def step(it, prefix):
                cand = prefix + lax.shift_left(jnp.int32(1), nbits - 1 - it)
                cnt = full_count(rs, _key_to_float(cand), False)
                return jnp.where(cnt >= float(topk), cand, prefix)
            key = lax.fori_loop(0, nbits, step, jnp.full((bis_rows, LANES), INT_MIN, jnp.int32))
            thr = _key_to_float(key)
            finish(rs, thr, full_count(rs, thr, True))

        for rs in blocks:
            thr = _key_to_float(key_ref[rs, :])
            finish(rs, thr, list_count(rs, thr, True))
            last = lists_ref[rs, (depth - 1) * LANES:depth * LANES]
            overflow = jnp.max(jnp.where(last > thr, 1.0, 0.0))
            pl.when(overflow > 0.0)(functools.partial(redo_block, rs))


def _dsa(p3, tri):
    b, s, _ = p3.shape
    tq = min(256, s)
    kc = min(512, s)
    nt = s // tq
    topk = min(TOPK_MAX, s // 4)
    depth = 12
    kern = functools.partial(_dsa_kernel, tq=tq, kc=kc, topk=topk, depth=depth,
                             ins_rows=16, bis_rows=min(64, tq), steps_per_chunk=4)
    wide = A_HEADS * LANES
    prev = lambda i: jnp.maximum(i - 1, 0)
    cur = lambda i: jnp.minimum(i, nt - 1)
    return pl.pallas_call(
        kern,
        grid=(b, nt + 1),
        in_specs=[
            pl.BlockSpec((1, tq, wide), lambda bi, i: (bi, prev(i), T_AQ)),
            pl.BlockSpec((1, tq, PROJ_TILE), lambda bi, i: (bi, cur(i), T_IQ)),
            pl.BlockSpec((1, tq, LANES), lambda bi, i: (bi, cur(i), SLAB_IW)),
            pl.BlockSpec((1, s, LANES), lambda bi, i: (bi, 0, SLAB_IK)),
            pl.BlockSpec((1, s, LANES), lambda bi, i: (bi, 0, SLAB_AK)),
            pl.BlockSpec((1, s, LANES), lambda bi, i: (bi, 0, SLAB_AV)),
            pl.BlockSpec(tri.shape, lambda bi, i: (0, 0)),
        ],
        out_specs=pl.BlockSpec((1, tq, wide // 2), lambda bi, i: (bi, prev(i), 0)),
        out_shape=jax.ShapeDtypeStruct((b, s, wide // 2), jnp.bfloat16),
        scratch_shapes=[
            pltpu.VMEM((2, s // kc, tq, kc), jnp.float32),
            pltpu.VMEM((tq, depth * LANES), jnp.float32),
            pltpu.VMEM((tq, LANES), jnp.int32),
            pltpu.VMEM((2, tq, LANES), jnp.float32),
            pltpu.VMEM((2, tq, LANES), jnp.float32),
            pltpu.VMEM((tq, LANES), jnp.float32),
            pltpu.VMEM((A_HEADS, tq, LANES), jnp.float32),
            pltpu.VMEM((A_HEADS, tq, 2 * LANES), jnp.float32),
        ],
        compiler_params=_cparams(("parallel", "arbitrary")),
        name="dsa",
    )(p3, p3, p3, p3, p3, p3, tri)


def _dil_kernel(q_ref, kp_ref, kc_ref, vp_ref, vc_ref, o_ref, lse_ref, *, tu, w):
    u = pl.program_id(2)
    qq = lax.broadcasted_iota(jnp.int32, (2 * w, 2 * w), 0) % w
    kk = lax.broadcasted_iota(jnp.int32, (2 * w, 2 * w), 1)
    band = (kk >= qq) & (kk <= qq + w)
    band_first = band & (kk >= jnp.where(u > 0, 0, w))
    lane = lax.broadcasted_iota(jnp.int32, (w, LANES), 1)
    low = lane < HEAD_DIM
    for sb in range(tu // w):
        rows = slice(sb * w, (sb + 1) * w)
        lse_tile = jnp.zeros((w, LANES), jnp.float32)
        for pr in range(B_HEADS // 2):
            sl = slice(pr * LANES, (pr + 1) * LANES)
            q = q_ref[0, 0, rows, sl]
            if sb == 0:
                k = jnp.concatenate([kp_ref[0, 0, :, sl], kc_ref[0, 0, :w, sl]], axis=0)
                v = jnp.concatenate([vp_ref[0, 0, :, sl], vc_ref[0, 0, :w, sl]], axis=0)
            else:
                k = kc_ref[0, 0, (sb - 1) * w:(sb + 1) * w, sl]
                v = vc_ref[0, 0, (sb - 1) * w:(sb + 1) * w, sl]
            zero = jnp.zeros_like(q)
            qm = jnp.concatenate([jnp.where(low, q, zero), jnp.where(low, zero, q)], axis=0)
            lg = lax.dot_general(qm, k, (((1,), (1,)), ((), ())), preferred_element_type=jnp.float32)
            lg = jnp.where(band_first if sb == 0 else band, lg, NEG_BIG)
            m = jnp.max(lg, axis=-1, keepdims=True)
            p = jnp.exp2(lg - m).astype(jnp.bfloat16)
            nd = jnp.dot(p, jnp.concatenate([v, jnp.ones_like(v)], axis=1),
                         preferred_element_type=jnp.float32)
            l = nd[:, LANES:]
            o = nd[:, :LANES] / l
            o_ref[0, 0, rows, sl] = jnp.where(low, o[:w], o[w:]).astype(o_ref.dtype)
            lse = (m + jnp.log2(l)) * math.log(2.0)
            lse_tile = jnp.where(lane == 2 * pr, lse[:w], lse_tile)
            lse_tile = jnp.where(lane == 2 * pr + 1, lse[w:], lse_tile)
        lse_ref[0, 0, rows, :] = lse_tile


def _dilated(src, tq_tile, tk_tile, tv_tile, gi):
    b, dil, su, _ = src.shape
    window, d2 = B_PATTERNS[gi]
    assert d2 == dil
    w = window // dil
    tu = min(512, su)
    ratio = tu // w
    width = B_HEADS * HEAD_DIM
    cur = lambda t: pl.BlockSpec((1, 1, tu, width), lambda bi, r, u: (bi, r, u, t))
    prev = lambda t: pl.BlockSpec((1, 1, w, width),
                                  lambda bi, r, u: (bi, r, jnp.maximum(u * ratio - 1, 0), t))
    return pl.pallas_call(
        functools.partial(_dil_kernel, tu=tu, w=w),
        grid=(b, dil, su // tu),
        in_specs=[cur(tq_tile), prev(tk_tile), cur(tk_tile), prev(tv_tile), cur(tv_tile)],
        out_specs=[pl.BlockSpec((1, 1, tu, width), lambda bi, r, u: (bi, r, u, 0)),
                   pl.BlockSpec((1, 1, tu, LANES), lambda bi, r, u: (bi, r, u, 0))],
        out_shape=[jax.ShapeDtypeStruct((b, dil, su, width), jnp.bfloat16),
                   jax.ShapeDtypeStruct((b, dil, su, LANES), jnp.float32)],
        compiler_params=_cparams(("parallel", "parallel", "arbitrary")),
        name=f"dilated{gi}",
    )(src, src, src, src, src)


def _mem_attn_kernel(q_ref, k_ref, v_ref, o_ref):
    scale = M_HEAD_DIM ** -0.5
    for h in range(M_HEADS):
        sl = slice(h * M_HEAD_DIM, (h + 1) * M_HEAD_DIM)
        lg = lax.dot_general(q_ref[0, :, sl], k_ref[0, :, sl], (((1,), (1,)), ((), ())),
                             preferred_element_type=jnp.float32) * scale
        m = jnp.max(lg, axis=-1, keepdims=True)
        p = jnp.exp(lg - m)
        l = jnp.sum(p, axis=-1, keepdims=True)
        o = jnp.dot(p.astype(jnp.bfloat16), v_ref[0, :, sl], preferred_element_type=jnp.float32)
        o_ref[0, :, sl] = (o / l).astype(o_ref.dtype)


def _mem_attn(p3, kv):
    b, s, _ = p3.shape
    m = kv.shape[1]
    width = M_HEADS * M_HEAD_DIM
    tq = min(512, s)
    return pl.pallas_call(
        _mem_attn_kernel,
        grid=(b, s // tq),
        in_specs=[
            pl.BlockSpec((1, tq, width), lambda bi, i: (bi, i, T_MQ)),
            pl.BlockSpec((1, m, width), lambda bi, i: (bi, 0, 0)),
            pl.BlockSpec((1, m, width), lambda bi, i: (bi, 0, 1)),
        ],
        out_specs=pl.BlockSpec((1, tq, width), lambda bi, i: (bi, i, 0)),
        out_shape=jax.ShapeDtypeStruct((b, s, width), jnp.bfloat16),
        compiler_params=_cparams(("parallel", "parallel")),
        name="mem_attn",
    )(p3, kv, kv)


def _split_bf16(v):
    hi = v.astype(jnp.bfloat16)
    lo = (v - hi.astype(jnp.float32)).astype(jnp.bfloat16)
    return hi, lo


def _merge_kernel(x_ref, gmix_ref, wg_ref, bg_ref, ya_ref, wa_ref, ob0_ref, ob1_ref, ob2_ref,
                  ls0_ref, ls1_ref, ls2_ref, wb_ref, ym_ref, wm_ref, wo_ref, gffn_ref,
                  wr2_ref, h_ref, xn_ref, rl_ref, ls_scr, ob_scr):
    d = x_ref.shape[2]
    tm = x_ref.shape[1]
    x = x_ref[0]
    n = _rms(x, gmix_ref[...]).astype(jnp.bfloat16)

    def gate(k):
        z = jnp.dot(n, wg_ref[:, k * d:(k + 1) * d], preferred_element_type=jnp.float32)
        return jax.nn.sigmoid(z + bg_ref[:, k * d:(k + 1) * d])

    merged = gate(0) * jnp.dot(ya_ref[0], wa_ref[...], preferred_element_type=jnp.float32)

    ob_refs = (ob0_ref, ob1_ref, ob2_ref)
    ls = []
    for gi, ls_ref in enumerate((ls0_ref, ls1_ref, ls2_ref)):
        dil = ls_ref.shape[1]
        if dil == 1:
            ls.append(ls_ref[0, 0])
        else:
            for r in range(dil):
                ls_scr[gi - 1, pl.ds(r, tm // dil, stride=dil), :] = ls_ref[0, r]
            ls.append(ls_scr[gi - 1])
    mx = jnp.maximum(jnp.maximum(ls[0], ls[1]), ls[2])
    es = [jnp.exp(v - mx) for v in ls]
    inv = 1.0 / (es[0] + es[1] + es[2])
    yb = [jnp.zeros((tm, LANES), jnp.float32) for _ in range(SLABS)]
    low = lax.broadcasted_iota(jnp.int32, (tm, LANES), 1) < HEAD_DIM
    for e, ob in zip(es, ob_refs):
        dil = ob.shape[1]
        a = e * inv
        for s in range(SLABS):
            sl = slice(s * LANES, (s + 1) * LANES)
            if dil == 1:
                o = ob[0, 0, :, sl].astype(jnp.float32)
            else:
                for r in range(dil):
                    ob_scr[s, pl.ds(r, tm // dil, stride=dil), :] = ob[0, r, :, sl].astype(jnp.float32)
                o = ob_scr[s]
            a_s = jnp.where(low, jnp.broadcast_to(a[:, 2 * s:2 * s + 1], (tm, LANES)),
                            jnp.broadcast_to(a[:, 2 * s + 1:2 * s + 2], (tm, LANES)))
            yb[s] = yb[s] + a_s * o
    yb = jnp.concatenate(yb, axis=1)
    merged = merged + gate(1) * jnp.dot(yb.astype(jnp.bfloat16), wb_ref[...],
                                        preferred_element_type=jnp.float32)
    merged = merged + gate(2) * jnp.dot(ym_ref[0], wm_ref[...], preferred_element_type=jnp.float32)

    h = x + jnp.dot(merged.astype(jnp.bfloat16), wo_ref[...], preferred_element_type=jnp.float32)
    h_ref[0] = h
    xn = _rms(h, gffn_ref[...])
    xn_ref[0] = xn.astype(jnp.bfloat16)
    hi, lo = _split_bf16(xn)
    rh = jnp.dot(hi, wr2_ref[...], preferred_element_type=jnp.float32)
    rlo = jnp.dot(lo, wr2_ref[...], preferred_element_type=jnp.float32)
    rl_ref[0] = rh[:, :LANES] + rh[:, LANES:] + rlo[:, :LANES]


def _merge(x3, g_mix, wg, bg, ya, wa, obs, lss, wb, ym, wm, wo, g_ffn, wr2):
    b, s, d = x3.shape
    tm = min(256, s)
    row = lambda w: pl.BlockSpec((1, tm, w), lambda bi, i: (bi, i, 0))
    full = lambda a: pl.BlockSpec(a.shape, lambda bi, i: (0,) * a.ndim)

    def dil_spec(a):
        dil, width = a.shape[1], a.shape[3]
        return pl.BlockSpec((1, dil, tm // dil, width), lambda bi, i: (bi, 0, i, 0))

    return pl.pallas_call(
        _merge_kernel,
        grid=(b, s // tm),
        in_specs=[row(d), full(g_mix), full(wg), full(bg), row(ya.shape[2]), full(wa),
                  dil_spec(obs[0]), dil_spec(obs[1]), dil_spec(obs[2]),
                  dil_spec(lss[0]), dil_spec(lss[1]), dil_spec(lss[2]), full(wb),
                  row(ym.shape[2]), full(wm), full(wo), full(g_ffn), full(wr2)],
        out_specs=[row(d), row(d), row(LANES)],
        out_shape=[jax.ShapeDtypeStruct((b, s, d), jnp.float32),
                   jax.ShapeDtypeStruct((b, s, d), jnp.bfloat16),
                   jax.ShapeDtypeStruct((b, s, LANES), jnp.float32)],
        scratch_shapes=[pltpu.VMEM((2, tm, LANES), jnp.float32),
                        pltpu.VMEM((SLABS, tm, LANES), jnp.float32)],
        compiler_params=_cparams(("parallel", "parallel")),
        name="merge",
    )(x3, g_mix, wg, bg, ya, wa, obs[0], obs[1], obs[2], lss[0], lss[1], lss[2], wb,
      ym, wm, wo, g_ffn, wr2)


def _first_lane_where(cond, lane):
    return jnp.min(jnp.where(cond, lane, float(LANES)), axis=-1, keepdims=True)


def _route(logits, rb):
    z = logits + rb
    lane = lax.broadcasted_iota(jnp.int32, z.shape, 1).astype(jnp.float32)
    is_g = lane < MOE_GROUPS
    zg = jnp.where(is_g, z, -jnp.inf)
    mg = jnp.max(zg, axis=-1, keepdims=True)
    eg = jnp.exp(zg - mg)
    gp = eg / jnp.sum(eg, axis=-1, keepdims=True)
    g_w = jnp.max(gp, axis=-1, keepdims=True)
    g_sel = _first_lane_where(is_g & (gp == g_w), lane)
    lo = R_SUB0 + g_sel * EXPERTS_PER_GROUP
    in_grp = (lane >= lo) & (lane < lo + EXPERTS_PER_GROUP)
    zs = jnp.where(in_grp, z, -jnp.inf)
    ms = jnp.max(zs, axis=-1, keepdims=True)
    es = jnp.exp(zs - ms)
    sp = es / jnp.sum(es, axis=-1, keepdims=True)
    p1 = jnp.max(sp, axis=-1, keepdims=True)
    i1 = _first_lane_where(in_grp & (sp == p1), lane)
    rest = in_grp & (lane != i1)
    sp2 = jnp.where(rest, sp, -1.0)
    p2 = jnp.max(sp2, axis=-1, keepdims=True)
    i2 = _first_lane_where(rest & (sp2 == p2), lane)
    tot = p1 + p2
    comb = jnp.where(lane == i1, g_w * (p1 / tot), jnp.where(lane == i2, g_w * (p2 / tot), 0.0))
    return comb, g_sel


def _moe_kernel(xn_ref, rl_ref, rb_ref, ltri_ref, w1_ref, w3_ref, w2_ref, o_ref,
                xs_ref, cs_ref, acc_ref, perm_ref, permt_ref, blk_ref):
    pr = pl.program_id(1)
    tm = xn_ref.shape[0]

    @pl.when(pr == 0)
    def _():
        comb, g_sel = _route(rl_ref[...], rb_ref[...])
        lane = lax.broadcasted_iota(jnp.int32, (tm, LANES), 1).astype(jnp.float32)
        onehot = jnp.where(lane == g_sel, 1.0, 0.0)
        seen = jnp.dot(ltri_ref[...], onehot.astype(jnp.bfloat16), preferred_element_type=jnp.float32)
        counts = seen[tm - 1:tm, :]
        lane_row = lane[0:1, :]
        off = 0.0
        off_row = jnp.zeros((1, LANES), jnp.float32)
        for g in range(MOE_GROUPS):
            n_g = jnp.sum(jnp.where(lane_row == g, counts, 0.0))
            off_row = jnp.where(lane_row == g, off, off_row)
            first = off.astype(jnp.int32) if g else jnp.int32(0)
            start = (first // 16) * 16
            blk_ref[g] = start
            blk_ref[MOE_GROUPS + g] = (first - start + n_g.astype(jnp.int32) + MOE_BLK - 1) // MOE_BLK
            off = off + n_g
        dest = jnp.sum(onehot * (off_row + seen - 1.0), axis=-1, keepdims=True)
        dest_row = jnp.transpose(jnp.broadcast_to(dest, (tm, LANES)))[0:1, :]
        row = lax.broadcasted_iota(jnp.int32, (tm, LANES), 0).astype(jnp.float32)
        for s in range(tm // LANES):
            sl = slice(s * LANES, (s + 1) * LANES)
            permt_ref[:, sl] = jnp.where(lane + float(s * LANES) == dest, 1.0, 0.0).astype(jnp.bfloat16)
            perm_ref[:, sl] = jnp.where(row == dest_row[:, sl], 1.0, 0.0).astype(jnp.bfloat16)
        perm = perm_ref[...]
        xs_ref[0:tm, :] = jnp.dot(perm, xn_ref[...], preferred_element_type=jnp.float32).astype(jnp.bfloat16)
        hi, lo = _split_bf16(comb)
        cs_ref[0:tm, :] = (jnp.dot(perm, hi, preferred_element_type=jnp.float32)
                           + jnp.dot(perm, lo, preferred_element_type=jnp.float32))
        xs_ref[tm:, :] = jnp.zeros((MOE_BLK, xs_ref.shape[1]), jnp.bfloat16)
        cs_ref[tm:, :] = jnp.zeros((MOE_BLK, LANES), jnp.float32)
        acc_ref[...] = jnp.zeros_like(acc_ref)

    g = pr // (EXPERTS_PER_GROUP // MOE_STEP)
    start = blk_ref[g]
    lane_i = lax.broadcasted_iota(jnp.int32, (MOE_BLK, LANES), 1)

    def block(b, carry):
        r0 = pl.multiple_of(start + b * MOE_BLK, 16)
        rows = pl.ds(r0, MOE_BLK)
        xb = xs_ref[rows, :]
        cb = cs_ref[rows, :]
        y = jnp.zeros((MOE_BLK, o_ref.shape[1]), jnp.float32)
        for e2 in range(MOE_STEP):
            c = jnp.sum(jnp.where(lane_i == R_SUB0 + MOE_STEP * pr + e2, cb, 0.0), axis=-1, keepdims=True)
            a = jnp.dot(xb, w1_ref[e2], preferred_element_type=jnp.float32)
            u = jnp.dot(xb, w3_ref[e2], preferred_element_type=jnp.float32)
            hid = (jax.nn.silu(a) * u * c).astype(jnp.bfloat16)
            y = y + jnp.dot(hid, w2_ref[e2], preferred_element_type=jnp.float32)
        acc_ref[rows, :] += y
        return carry

    lax.fori_loop(0, blk_ref[MOE_GROUPS + g], block, 0)

    @pl.when(pr == N_EXPERTS // MOE_STEP - 1)
    def _():
        o_ref[...] = jnp.dot(permt_ref[...], acc_ref[0:tm, :].astype(jnp.bfloat16),
                             preferred_element_type=jnp.float32).astype(o_ref.dtype)


def _moe(xn, rl, rb, w1, w3, w2):
    t, d = xn.shape
    hid = w1.shape[2]
    tm = min(1024, t)
    ltri = (jnp.arange(tm)[:, None] >= jnp.arange(tm)[None, :]).astype(jnp.bfloat16)
    return pl.pallas_call(
        _moe_kernel,
        grid=(t // tm, N_EXPERTS // MOE_STEP),
        in_specs=[
            pl.BlockSpec((tm, d), lambda i, p: (i, 0)),
            pl.BlockSpec((tm, LANES), lambda i, p: (i, 0)),
            pl.BlockSpec((1, LANES), lambda i, p: (0, 0)),
            pl.BlockSpec((tm, tm), lambda i, p: (0, 0)),
            pl.BlockSpec((MOE_STEP, d, hid), lambda i, p: (p, 0, 0)),
            pl.BlockSpec((MOE_STEP, d, hid), lambda i, p: (p, 0, 0)),
            pl.BlockSpec((MOE_STEP, hid, d), lambda i, p: (p, 0, 0)),
        ],
        out_specs=pl.BlockSpec((tm, d), lambda i, p: (i, 0)),
        out_shape=jax.ShapeDtypeStruct((t, d), jnp.bfloat16),
        scratch_shapes=[
            pltpu.VMEM((tm + MOE_BLK, d), jnp.bfloat16),
            pltpu.VMEM((tm + MOE_BLK, LANES), jnp.float32),
            pltpu.VMEM((tm + MOE_BLK, d), jnp.float32),
            pltpu.VMEM((tm, tm), jnp.bfloat16),
            pltpu.VMEM((tm, tm), jnp.bfloat16),
            pltpu.SMEM((2 * MOE_GROUPS,), jnp.int32),
        ],
        compiler_params=_cparams(("parallel", "arbitrary")),
        name="moe",
    )(xn, rl, rb, ltri, w1, w3, w2)


def _final_kernel(h_ref, m_ref, g_ref, o_ref):
    o_ref[...] = _rms(h_ref[...] + m_ref[...].astype(jnp.float32), g_ref[...])


def _final(h, moe, g_final):
    t, d = h.shape
    tm = min(512, t)
    return pl.pallas_call(
        _final_kernel,
        grid=(t // tm,),
        in_specs=[pl.BlockSpec((tm, d), lambda i: (i, 0)), pl.BlockSpec((tm, d), lambda i: (i, 0)),
                  pl.BlockSpec((1, d), lambda i: (0, 0))],
        out_specs=pl.BlockSpec((tm, d), lambda i: (i, 0)),
        out_shape=jax.ShapeDtypeStruct((t, d), jnp.float32),
        compiler_params=_cparams(("parallel",)),
        name="final_norm",
    )(h, moe, g_final)


def _pad_heads_kv(w):
    d = w.shape[0]
    group = A_HEADS // A_KV_HEADS
    w = w.reshape(d, A_KV_HEADS, group, HEAD_DIM)
    parts = [jnp.pad(w[:, j], ((0, 0), (0, 0), (j * HEAD_DIM, LANES - (j + 1) * HEAD_DIM)))
             for j in range(A_KV_HEADS)]
    return jnp.concatenate(parts, axis=1).reshape(d, A_HEADS * LANES)


def _pack_w_in(w_in):
    parts, off = [], 0
    for n in IN_SPLITS:
        parts.append(w_in[:, off:off + n])
        off += n
    aq, ak, av, iq, ik, iw, bq, bk, bv, mq = parts
    d = w_in.shape[0]
    qscale = HEAD_DIM ** -0.5
    aq_x = _pad_heads_kv(aq * (qscale * math.log2(math.e)))
    iq_x = jnp.pad((iq * qscale).reshape(d, IDX_HEADS, HEAD_DIM),
                   ((0, 0), (0, 0), (0, LANES - HEAD_DIM))).reshape(d, IDX_HEADS * LANES)
    ik_x = jnp.pad(ik, ((0, 0), (0, LANES - ik.shape[1])))
    iw2 = jnp.concatenate([iw, iw], axis=1) * (IDX_HEADS ** -0.5)
    iw_x = jnp.pad(iw2, ((0, 0), (0, LANES - iw2.shape[1])))
    width = B_HEADS * HEAD_DIM
    bqs = bq * (qscale * math.log2(math.e))
    grp = lambda a, gi: a[:, gi * width:(gi + 1) * width]
    nat = jnp.concatenate([aq_x, iq_x, grp(bqs, 0), grp(bk, 0), ak, ik_x, av, iw_x, grp(bv, 0), mq], axis=1)
    dil = [jnp.concatenate([grp(bqs, gi), grp(bk, gi), grp(bv, gi)], axis=1) for gi in (1, 2)]
    return nat, dil


def _rope_tables(seq):
    half = HEAD_DIM // 2
    inv = ROPE_THETA ** (-jnp.arange(half, dtype=jnp.float32) / half)
    ang = jnp.arange(seq, dtype=jnp.float32)[:, None] * inv[None, :]
    cos = jnp.tile(jnp.cos(ang), (1, LANES // half))
    sign = jnp.tile(jnp.concatenate([-jnp.ones((half,), jnp.float32), jnp.ones((half,), jnp.float32)]),
                    LANES // HEAD_DIM)
    sin = jnp.tile(jnp.sin(ang), (1, LANES // half)) * sign[None, :]
    return cos, sin


def _layer(x3, mem, g_mix, g_mem, w_in, w_mem_kv, w_gate, b_gate, w_branch, w_out, g_ffn,
           w_group, b_group, w_sub, b_sub, w1, w3, w2, g_out):
    b, s, d = x3.shape
    bf = jnp.bfloat16
    cos_t, sin_t = _rope_tables(s)
    w_nat, w_dil = _pack_w_in(w_in)
    gm = g_mix[None, :]

    p4 = _in_proj(x3, gm, w_nat.astype(bf), cos_t, sin_t, P_MODES, 1)
    p3 = p4.reshape(b, s, p4.shape[3])
    srcs = [p4] + [_in_proj(x3, gm, w.astype(bf), cos_t, sin_t, D_MODES, B_PATTERNS[gi + 1][1])
                   for gi, w in enumerate(w_dil)]

    kv = _mem_kv(mem, g_mem[None, :], w_mem_kv.astype(bf))

    half = 256
    tri = (jnp.arange(half)[:, None] <= jnp.arange(half)[None, :]).astype(bf)
    ya = _dsa(p3, tri)

    obs, lss = [], []
    for gi, src in enumerate(srcs):
        tiles = (T_BQ, T_BK, T_BV) if gi == 0 else (0, 1, 2)
        o, l = _dilated(src, *tiles, gi)
        obs.append(o)
        lss.append(l)

    ym = _mem_attn(p3, kv)

    group = A_HEADS // A_KV_HEADS
    wa = jnp.swapaxes(w_branch[0].reshape(A_KV_HEADS, group, HEAD_DIM, d), 0, 1).reshape(-1, d).astype(bf)
    w_route = jnp.concatenate([w_group, jnp.moveaxis(w_sub, 0, 1).reshape(d, N_EXPERTS)], axis=1)
    w_route = jnp.pad(w_route, ((0, 0), (0, LANES - w_route.shape[1])))
    wrh = w_route.astype(bf)
    wrl = (w_route - wrh.astype(jnp.float32)).astype(bf)
    wr2 = jnp.concatenate([wrh, wrl], axis=1)
    r_bias = jnp.pad(jnp.concatenate([b_group, b_sub.reshape(-1)]), (0, LANES - MOE_GROUPS - N_EXPERTS))

    h, xn, rl = _merge(x3, gm, w_gate.astype(bf), b_gate[None, :], ya, wa, obs, lss,
                       w_branch[1].astype(bf), ym, w_branch[2].astype(bf), w_out.astype(bf),
                       g_ffn[None, :], wr2)
    t = b * s
    moe = _moe(xn.reshape(t, d), rl.reshape(t, LANES), r_bias[None, :],
               w1.astype(bf), w3.astype(bf), w2.astype(bf))
    return _final(h.reshape(t, d), moe, g_out[None, :]).reshape(b, s, d)


def kernel(x, mem, g_mix, g_mem, w_in, w_mem_kv, w_gate, b_gate, w_branch, w_out, g_ffn,
           w_group, b_group, w_sub, b_sub, w1, w3, w2, g_final):
    depth = g_mix.shape[0]
    assert depth == 1, "the final rmsnorm is applied right after the single layer"
    return _layer(x, mem, g_mix[0], g_mem[0], w_in[0], w_mem_kv[0], w_gate[0], b_gate[0], w_branch[0],
                  w_out[0], g_ffn[0], w_group[0], b_group[0], w_sub[0], b_sub[0], w1[0], w3[0], w2[0],
                  g_final)
```

```python
import functools
import math

import jax
import jax.numpy as jnp
from jax import lax
from jax.experimental import pallas as pl
from jax.experimental.pallas import tpu as pltpu

HEAD_DIM = 64
ROPE_THETA = 10000.0
RMS_EPS = 1e-6
A_HEADS = 8
A_KV_HEADS = 2
IDX_HEADS = 4
TOPK_MAX = 256
B_PATTERNS = ((128, 1), (512, 4), (2048, 16))
B_HEADS = 8
M_HEADS = 4
M_HEAD_DIM = 128
MOE_GROUPS = 4
EXPERTS_PER_GROUP = 4
N_EXPERTS = 16
IN_SPLITS = (512, 128, 128, 256, 64, 4, 1536, 1536, 1536, 512)

LANES = 128
PROJ_TILE = 512
SLABS = PROJ_TILE // LANES
VMEM_LIMIT = 56 * 1024 * 1024

T_AQ, T_IQ, T_BQ, T_BK, T_MIX, T_BV, T_MQ = 0, 2, 3, 4, 5, 6, 7
P_MODES = (("rope",) * 4,) * 5 + (("rope", "rope", "plain", "hi_lo"),) + (("plain",) * 4,) * 2
SLAB_AK = T_MIX * SLABS + 0
SLAB_IK = T_MIX * SLABS + 1
SLAB_AV = T_MIX * SLABS + 2
SLAB_IW = T_MIX * SLABS + 3
D_MODES = (("rope",) * 4, ("rope",) * 4, ("plain",) * 4)

INT_MIN = -(2 ** 31)
NEG_BIG = -1e30

R_SUB0 = MOE_GROUPS

MOE_BLK = 256
MOE_STEP = 4


def _cparams(sem):
    return pltpu.CompilerParams(dimension_semantics=sem, vmem_limit_bytes=VMEM_LIMIT)


def _rms(xf, g):
    return xf * lax.rsqrt(jnp.mean(xf * xf, axis=-1, keepdims=True) + RMS_EPS) * g


def _rope_slab(y, cos, sin_signed, first_half):
    partner = jnp.where(first_half, pltpu.roll(y, 96, 1), pltpu.roll(y, 32, 1))
    return y * cos + partner * sin_signed


def _in_proj_kernel(x_ref, g_ref, w_ref, cos_ref, sin_ref, o_ref, n_ref, *stage, modes, dil, rows):
    j = pl.program_id(2)
    tm = x_ref.shape[1]

    @pl.when(j == 0)
    def _():
        n_ref[...] = _rms(x_ref[0], g_ref[...]).astype(jnp.bfloat16)

    lane = lax.broadcasted_iota(jnp.int32, (rows, LANES), 1)
    first_half = (lane & (HEAD_DIM - 1)) < (HEAD_DIM // 2)

    def epilogue(slab_modes):
        for r in range(tm // rows):
            rs = pl.ds(r * rows, rows)
            acc = jnp.dot(n_ref[rs, :], w_ref[...], preferred_element_type=jnp.float32)
            cos = cos_ref[rs, :]
            sin = sin_ref[rs, :]
            for s in range(SLABS):
                y = acc[:, s * LANES:(s + 1) * LANES]
                if slab_modes[s] == "rope":
                    y = _rope_slab(y, cos, sin, first_half)
                elif slab_modes[s] == "hi_lo":
                    resid = y - y.astype(jnp.bfloat16).astype(jnp.float32)
                    y = jnp.where(lane < IDX_HEADS, y, resid)
                if dil == 1:
                    o_ref[0, 0, rs, s * LANES:(s + 1) * LANES] = y.astype(o_ref.dtype)
                else:
                    stage[0][s, rs, :] = y
        if dil > 1:
            for r in range(dil):
                for s in range(SLABS):
                    o_ref[0, r, :, s * LANES:(s + 1) * LANES] = (
                        stage[0][s, pl.ds(r, tm // dil, stride=dil), :].astype(o_ref.dtype))

    for pattern in sorted(set(modes)):
        tiles = [t for t, m in enumerate(modes) if m == pattern]
        cond = functools.reduce(jnp.logical_or, [j == t for t in tiles])
        pl.when(cond)(functools.partial(epilogue, pattern))


def _in_proj(x3, g_mix, wp, cos_t, sin_t, modes, dil):
    b, s, d = x3.shape
    tm = min(1024, s)
    ntiles = len(modes)
    scratch = [pltpu.VMEM((tm, d), jnp.bfloat16)]
    if dil > 1:
        scratch.append(pltpu.VMEM((SLABS, tm, LANES), jnp.float32))
    return pl.pallas_call(
        functools.partial(_in_proj_kernel, modes=modes, dil=dil, rows=256),
        grid=(b, s // tm, ntiles),
        in_specs=[
            pl.BlockSpec((1, tm, d), lambda bi, i, j: (bi, i, 0)),
            pl.BlockSpec((1, d), lambda bi, i, j: (0, 0)),
            pl.BlockSpec((d, PROJ_TILE), lambda bi, i, j: (0, j)),
            pl.BlockSpec((tm, LANES), lambda bi, i, j: (i, 0)),
            pl.BlockSpec((tm, LANES), lambda bi, i, j: (i, 0)),
        ],
        out_specs=pl.BlockSpec((1, dil, tm // dil, PROJ_TILE), lambda bi, i, j: (bi, 0, i, j)),
        out_shape=jax.ShapeDtypeStruct((b, dil, s // dil, ntiles * PROJ_TILE), jnp.bfloat16),
        scratch_shapes=scratch,
        compiler_params=_cparams(("parallel", "parallel", "arbitrary")),
        name=f"in_proj_d{dil}",
    )(x3, g_mix, wp, cos_t, sin_t)


def _mem_kv_kernel(m_ref, g_ref, w_ref, o_ref):
    n = _rms(m_ref[0], g_ref[...]).astype(jnp.bfloat16)
    o_ref[0] = jnp.dot(n, w_ref[...], preferred_element_type=jnp.float32).astype(o_ref.dtype)


def _mem_kv(mem, g_mem, w_kv):
    b, m, d = mem.shape
    n = w_kv.shape[1]
    return pl.pallas_call(
        _mem_kv_kernel,
        grid=(b,),
        in_specs=[
            pl.BlockSpec((1, m, d), lambda i: (i, 0, 0)),
            pl.BlockSpec((1, d), lambda i: (0, 0)),
            pl.BlockSpec((d, n), lambda i: (0, 0)),
        ],
        out_specs=pl.BlockSpec((1, m, n), lambda i: (i, 0, 0)),
        out_shape=jax.ShapeDtypeStruct((b, m, n), jnp.bfloat16),
        compiler_params=_cparams(("parallel",)),
        name="mem_kv",
    )(mem, g_mem, w_kv)


def _key_to_float(key):
    bits = jnp.where(key < 0, key ^ jnp.int32(0x7FFFFFFF), key)
    f = lax.bitcast_convert_type(bits, jnp.float32)
    return jnp.where((key < 0) & (f != f), -jnp.inf, f)


def _dsa_kernel(aq_ref, iq_ref, iw_ref, ik_ref, ak_ref, av_ref, tri_ref, o_ref,
                sc_ref, lists_ref, key_ref, thr_ref, need_ref, off_ref, m_ref, acc_ref,
                *, tq, kc, topk, depth, ins_rows, bis_rows, steps_per_chunk):
    i = pl.program_id(1)
    nt = pl.num_programs(1) - 1
    slot = i % 2
    pslot = 1 - slot
    nslab = kc // LANES
    group = A_HEADS // A_KV_HEADS
    nbits = 32
    blocks = [pl.ds(rb * bis_rows, bis_rows) for rb in range(tq // bis_rows)]

    def chunks_of(tile):
        return (tile * tq + tq + kc - 1) // kc

    @pl.when(i < nt)
    def _():
        q0 = i * tq
        iw = iw_ref[0].astype(jnp.float32)
        w_heads = [iw[:, h:h + 1] + iw[:, IDX_HEADS + h:IDX_HEADS + h + 1] for h in range(IDX_HEADS)]
        qpos = q0 + lax.broadcasted_iota(jnp.int32, (tq, kc), 0)
        kiota = lax.broadcasted_iota(jnp.int32, (tq, kc), 1)

        def score_chunk(c, carry):
            k0 = pl.multiple_of(c * kc, kc)
            ikc = ik_ref[0, pl.ds(k0, kc), :]
            s = jnp.zeros((tq, kc), jnp.float32)
            for h in range(IDX_HEADS):
                d = lax.dot_general(iq_ref[0, :, h * LANES:(h + 1) * LANES], ikc,
                                    (((1,), (1,)), ((), ())), preferred_element_type=jnp.float32)
                s = s + jnp.maximum(d, 0.0) * w_heads[h]
            s = jnp.where(kiota + k0 <= qpos, s, -jnp.inf)
            sc_ref[slot, c] = s
            for rg in range(tq // ins_rows):
                rows = slice(rg * ins_rows, (rg + 1) * ins_rows)
                tops = [lists_ref[rows, j * LANES:(j + 1) * LANES] for j in range(depth)]
                for sl in range(nslab):
                    x = s[rows, sl * LANES:(sl + 1) * LANES]
                    for j in range(depth):
                        hi = jnp.maximum(tops[j], x)
                        x = jnp.minimum(tops[j], x)
                        tops[j] = hi
                for j in range(depth):
                    lists_ref[rows, j * LANES:(j + 1) * LANES] = tops[j]
            return carry

        lists_ref[...] = jnp.full(lists_ref.shape, -jnp.inf, jnp.float32)
        lax.fori_loop(0, chunks_of(i), score_chunk, 0)
        key_ref[...] = jnp.full(key_ref.shape, INT_MIN, jnp.int32)

    def list_count(rs, cand, strict):
        acc = jnp.zeros((bis_rows, LANES), jnp.float32)
        for j in range(depth):
            v = lists_ref[rs, j * LANES:(j + 1) * LANES]
            acc = acc + jnp.where((v > cand) if strict else (v >= cand), 1.0, 0.0)
        return jnp.sum(acc, axis=-1, keepdims=True)

    def bit_step(it):
        for rs in blocks:
            pre = key_ref[rs, :]
            cand = pre + lax.shift_left(jnp.int32(1), nbits - 1 - it)
            cnt = list_count(rs, _key_to_float(cand), False)
            key_ref[rs, :] = jnp.where(cnt >= float(topk), cand, pre)

    half = tri_ref.shape[0]

    def attn_chunk(c):
        k0 = pl.multiple_of(c * kc, kc)
        akc = ak_ref[0, pl.ds(k0, kc), :]
        avc = av_ref[0, pl.ds(k0, kc), :]
        thr_b = thr_ref[pslot]
        need_b = need_ref[pslot]
        bias = []
        off = off_ref[...]
        for hh in range(kc // half):
            eq_parts = []
            for s in range(half // LANES):
                v = sc_ref[pslot, c, :, hh * half + s * LANES: hh * half + (s + 1) * LANES]
                eq_parts.append(jnp.where(v == thr_b, 1.0, 0.0).astype(jnp.bfloat16))
            eqf = jnp.concatenate(eq_parts, axis=1)
            pre = jnp.dot(eqf, tri_ref[...], preferred_element_type=jnp.float32)
            for s in range(half // LANES):
                v = sc_ref[pslot, c, :, hh * half + s * LANES: hh * half + (s + 1) * LANES]
                rank = pre[:, s * LANES:(s + 1) * LANES] + off
                tie_ok = (v == thr_b) & (rank <= need_b)
                sel = (v > thr_b) | tie_ok
                bias.append(jnp.where(sel, 0.0, NEG_BIG))
            off = off + jnp.sum(eqf.astype(jnp.float32), axis=-1, keepdims=True)
        off_ref[...] = off

        av_ones = jnp.concatenate([avc, jnp.ones_like(avc)], axis=1)
        for h in range(A_HEADS):
            logits = lax.dot_general(aq_ref[0, :, h * LANES:(h + 1) * LANES], akc,
                                     (((1,), (1,)), ((), ())), preferred_element_type=jnp.float32)
            lg = [logits[:, s * LANES:(s + 1) * LANES] + bias[s] for s in range(nslab)]
            m_old = m_ref[h]
            m_cur = functools.reduce(jnp.maximum, lg)
            m_new = jnp.maximum(m_old, jnp.max(m_cur, axis=-1, keepdims=True))
            alpha = jnp.exp2(m_old - m_new)
            p = jnp.concatenate([jnp.exp2((x - m_new).astype(jnp.bfloat16)) for x in lg], axis=1)
            m_ref[h] = m_new
            pv = jnp.dot(p, av_ones, preferred_element_type=jnp.float32)
            acc_ref[h] = jnp.concatenate([alpha, alpha], axis=1) * acc_ref[h] + pv

    nprev = chunks_of(i - 1)
    nride = jnp.minimum(nprev, nbits // steps_per_chunk)

    @pl.when(i >= 1)
    def _():
        off_ref[...] = jnp.zeros_like(off_ref)
        m_ref[...] = jnp.full(m_ref.shape, NEG_BIG, jnp.float32)
        acc_ref[...] = jnp.zeros_like(acc_ref)

        def chunk_with_steps(c, carry):
            attn_chunk(c)
            for j in range(steps_per_chunk):
                bit_step(c * steps_per_chunk + j)
            return carry

        def chunk_plain(c, carry):
            attn_chunk(c)
            return carry

        lax.fori_loop(0, nride, chunk_with_steps, 0)
        lax.fori_loop(nride, nprev, chunk_plain, 0)

        low = lax.broadcasted_iota(jnp.int32, (tq, LANES), 1) < HEAD_DIM
        for p in range(group):
            outs = [acc_ref[h, :, :LANES] / acc_ref[h, :, LANES:] for h in (p, group + p)]
            o_ref[0, :, p * LANES:(p + 1) * LANES] = jnp.where(low, outs[0], outs[1]).astype(o_ref.dtype)

    @pl.when(i < nt)
    def _():
        nchunks = chunks_of(i)
        done = jnp.where(i >= 1, nride * steps_per_chunk, 0)

        def rest(it, carry):
            bit_step(it)
            return carry

        lax.fori_loop(done, nbits, rest, 0)

        def finish(rs, thr, c_gt):
            thr_ref[slot, rs, :] = thr
            need_ref[slot, rs, :] = jnp.where(thr == -jnp.inf, 0.0, float(topk) - c_gt)

        def full_count(rs, cand, strict):
            def body(c, acc):
                for s in range(nslab):
                    v = sc_ref[slot, c, rs, s * LANES:(s + 1) * LANES]
                    acc = acc + jnp.where((v > cand) if strict else (v >= cand), 1.0, 0.0)
                return acc
            acc = lax.fori_loop(0, nchunks, body, jnp.zeros((bis_rows, LANES), jnp.float32))
            return jnp.sum(acc, axis=-1, keepdims=True)

        def redo_block(rs):
            def step(it, prefix):
                cand = prefix + lax.shift_left(jnp.int32(1), nbits - 1 - it)
                cnt = full_count(rs, _key_to_float(cand), False)
                return jnp.where(cnt >= float(topk), cand, prefix)
            key = lax.fori_loop(0, nbits, step, jnp.full((bis_rows, LANES), INT_MIN, jnp.int32))
            thr = _key_to_float(key)
            finish(rs, thr, full_count(rs, thr, True))

        for rs in blocks:
            thr = _key_to_float(key_ref[rs, :])
            finish(rs, thr, list_count(rs, thr, True))
            last = lists_ref[rs, (depth - 1) * LANES:depth * LANES]
            overflow = jnp.max(jnp.where(last > thr, 1.0, 0.0))
            pl.when(overflow > 0.0)(functools.partial(redo_block, rs))


def _dsa(p3, tri):
    b, s, _ = p3.shape
    tq = min(256, s)
    kc = min(512, s)
    nt = s // tq
    topk = min(TOPK_MAX, s // 4)
    depth = 12
    kern = functools.partial(_dsa_kernel, tq=tq, kc=kc, topk=topk, depth=depth,
                             ins_rows=16, bis_rows=min(64, tq), steps_per_chunk=4)
    wide = A_HEADS * LANES
    prev = lambda i: jnp.maximum(i - 1, 0)
    cur = lambda i: jnp.minimum(i, nt - 1)
    return pl.pallas_call(
        kern,
        grid=(b, nt + 1),
        in_specs=[
            pl.BlockSpec((1, tq, wide), lambda bi, i: (bi, prev(i), T_AQ)),
            pl.BlockSpec((1, tq, PROJ_TILE), lambda bi, i: (bi, cur(i), T_IQ)),
            pl.BlockSpec((1, tq, LANES), lambda bi, i: (bi, cur(i), SLAB_IW)),
            pl.BlockSpec((1, s, LANES), lambda bi, i: (bi, 0, SLAB_IK)),
            pl.BlockSpec((1, s, LANES), lambda bi, i: (bi, 0, SLAB_AK)),
            pl.BlockSpec((1, s, LANES), lambda bi, i: (bi, 0, SLAB_AV)),
            pl.BlockSpec(tri.shape, lambda bi, i: (0, 0)),
        ],
        out_specs=pl.BlockSpec((1, tq, wide // 2), lambda bi, i: (bi, prev(i), 0)),
        out_shape=jax.ShapeDtypeStruct((b, s, wide // 2), jnp.bfloat16),
        scratch_shapes=[
            pltpu.VMEM((2, s // kc, tq, kc), jnp.float32),
            pltpu.VMEM((tq, depth * LANES), jnp.float32),
            pltpu.VMEM((tq, LANES), jnp.int32),
            pltpu.VMEM((2, tq, LANES), jnp.float32),
            pltpu.VMEM((2, tq, LANES), jnp.float32),
            pltpu.VMEM((tq, LANES), jnp.float32),
            pltpu.VMEM((A_HEADS, tq, LANES), jnp.float32),
            pltpu.VMEM((A_HEADS, tq, 2 * LANES), jnp.float32),
        ],
        compiler_params=_cparams(("parallel", "arbitrary")),
        name="dsa",
    )(p3, p3, p3, p3, p3, p3, tri)


def _dil_kernel(q_ref, kp_ref, kc_ref, vp_ref, vc_ref, o_ref, lse_ref, *, tu, w):
    u = pl.program_id(2)
    qq = lax.broadcasted_iota(jnp.int32, (2 * w, 2 * w), 0) % w
    kk = lax.broadcasted_iota(jnp.int32, (2 * w, 2 * w), 1)
    band = (kk >= qq) & (kk <= qq + w)
    band_first = band & (kk >= jnp.where(u > 0, 0, w))
    lane = lax.broadcasted_iota(jnp.int32, (w, LANES), 1)
    low = lane < HEAD_DIM
    for sb in range(tu // w):
        rows = slice(sb * w, (sb + 1) * w)
        lse_tile = jnp.zeros((w, LANES), jnp.float32)
        for pr in range(B_HEADS // 2):
            sl = slice(pr * LANES, (pr + 1) * LANES)
            q = q_ref[0, 0, rows, sl]
            if sb == 0:
                k = jnp.concatenate([kp_ref[0, 0, :, sl], kc_ref[0, 0, :w, sl]], axis=0)
                v = jnp.concatenate([vp_ref[0, 0, :, sl], vc_ref[0, 0, :w, sl]], axis=0)
            else:
                k = kc_ref[0, 0, (sb - 1) * w:(sb + 1) * w, sl]
                v = vc_ref[0, 0, (sb - 1) * w:(sb + 1) * w, sl]
            zero = jnp.zeros_like(q)
            qm = jnp.concatenate([jnp.where(low, q, zero), jnp.where(low, zero, q)], axis=0)
            lg = lax.dot_general(qm, k, (((1,), (1,)), ((), ())), preferred_element_type=jnp.float32)
            lg = jnp.where(band_first if sb == 0 else band, lg, NEG_BIG)
            m = jnp.max(lg, axis=-1, keepdims=True)
            p = jnp.exp2(lg - m).astype(jnp.bfloat16)
            nd = jnp.dot(p, jnp.concatenate([v, jnp.ones_like(v)], axis=1),
                         preferred_element_type=jnp.float32)
            l = nd[:, LANES:]
            o = nd[:, :LANES] / l
            o_ref[0, 0, rows, sl] = jnp.where(low, o[:w], o[w:]).astype(o_ref.dtype)
            lse = (m + jnp.log2(l)) * math.log(2.0)
            lse_tile = jnp.where(lane == 2 * pr, lse[:w], lse_tile)
            lse_tile = jnp.where(lane == 2 * pr + 1, lse[w:], lse_tile)
        lse_ref[0, 0, rows, :] = lse_tile


def _dilated(src, tq_tile, tk_tile, tv_tile, gi):
    b, dil, su, _ = src.shape
    window, d2 = B_PATTERNS[gi]
    assert d2 == dil
    w = window // dil
    tu = min(512, su)
    ratio = tu // w
    width = B_HEADS * HEAD_DIM
    cur = lambda t: pl.BlockSpec((1, 1, tu, width), lambda bi, r, u: (bi, r, u, t))
    prev = lambda t: pl.BlockSpec((1, 1, w, width),
                                  lambda bi, r, u: (bi, r, jnp.maximum(u * ratio - 1, 0), t))
    return pl.pallas_call(
        functools.partial(_dil_kernel, tu=tu, w=w),
        grid=(b, dil, su // tu),
        in_specs=[cur(tq_tile), prev(tk_tile), cur(tk_tile), prev(tv_tile), cur(tv_tile)],
        out_specs=[pl.BlockSpec((1, 1, tu, width), lambda bi, r, u: (bi, r, u, 0)),
                   pl.BlockSpec((1, 1, tu, LANES), lambda bi, r, u: (bi, r, u, 0))],
        out_shape=[jax.ShapeDtypeStruct((b, dil, su, width), jnp.bfloat16),
                   jax.ShapeDtypeStruct((b, dil, su, LANES), jnp.float32)],
        compiler_params=_cparams(("parallel", "parallel", "arbitrary")),
        name=f"dilated{gi}",
    )(src, src, src, src, src)


def _mem_attn_kernel(q_ref, k_ref, v_ref, o_ref):
    scale = M_HEAD_DIM ** -0.5
    for h in range(M_HEADS):
        sl = slice(h * M_HEAD_DIM, (h + 1) * M_HEAD_DIM)
        lg = lax.dot_general(q_ref[0, :, sl], k_ref[0, :, sl], (((1,), (1,)), ((), ())),
                             preferred_element_type=jnp.float32) * scale
        m = jnp.max(lg, axis=-1, keepdims=True)
        p = jnp.exp(lg - m)
        l = jnp.sum(p, axis=-1, keepdims=True)
        o = jnp.dot(p.astype(jnp.bfloat16), v_ref[0, :, sl], preferred_element_type=jnp.float32)
        o_ref[0, :, sl] = (o / l).astype(o_ref.dtype)


def _mem_attn(p3, kv):
    b, s, _ = p3.shape
    m = kv.shape[1]
    width = M_HEADS * M_HEAD_DIM
    tq = min(512, s)
    return pl.pallas_call(
        _mem_attn_kernel,
        grid=(b, s // tq),
        in_specs=[
            pl.BlockSpec((1, tq, width), lambda bi, i: (bi, i, T_MQ)),
            pl.BlockSpec((1, m, width), lambda bi, i: (bi, 0, 0)),
            pl.BlockSpec((1, m, width), lambda bi, i: (bi, 0, 1)),
        ],
        out_specs=pl.BlockSpec((1, tq, width), lambda bi, i: (bi, i, 0)),
        out_shape=jax.ShapeDtypeStruct((b, s, width), jnp.bfloat16),
        compiler_params=_cparams(("parallel", "parallel")),
        name="mem_attn",
    )(p3, kv, kv)


def _split_bf16(v):
    hi = v.astype(jnp.bfloat16)
    lo = (v - hi.astype(jnp.float32)).astype(jnp.bfloat16)
    return hi, lo


def _merge_kernel(x_ref, gmix_ref, wg_ref, bg_ref, ya_ref, wa_ref, ob0_ref, ob1_ref, ob2_ref,
                  ls0_ref, ls1_ref, ls2_ref, wb_ref, ym_ref, wm_ref, wo_ref, gffn_ref,
                  wr2_ref, h_ref, xn_ref, rl_ref, ls_scr, ob_scr):
    d = x_ref.shape[2]
    tm = x_ref.shape[1]
    x = x_ref[0]
    n = _rms(x, gmix_ref[...]).astype(jnp.bfloat16)

    def gate(k):
        z = jnp.dot(n, wg_ref[:, k * d:(k + 1) * d], preferred_element_type=jnp.float32)
        return jax.nn.sigmoid(z + bg_ref[:, k * d:(k + 1) * d])

    merged = gate(0) * jnp.dot(ya_ref[0], wa_ref[...], preferred_element_type=jnp.float32)

    ob_refs = (ob0_ref, ob1_ref, ob2_ref)
    ls = []
    for gi, ls_ref in enumerate((ls0_ref, ls1_ref, ls2_ref)):
        dil = ls_ref.shape[1]
        if dil == 1:
            ls.append(ls_ref[0, 0])
        else:
            for r in range(dil):
                ls_scr[gi - 1, pl.ds(r, tm // dil, stride=dil), :] = ls_ref[0, r]
            ls.append(ls_scr[gi - 1])
    mx = jnp.maximum(jnp.maximum(ls[0], ls[1]), ls[2])
    es = [jnp.exp(v - mx) for v in ls]
    inv = 1.0 / (es[0] + es[1] + es[2])
    yb = [jnp.zeros((tm, LANES), jnp.float32) for _ in range(SLABS)]
    low = lax.broadcasted_iota(jnp.int32, (tm, LANES), 1) < HEAD_DIM
    for e, ob in zip(es, ob_refs):
        dil = ob.shape[1]
        a = e * inv
        for s in range(SLABS):
            sl = slice(s * LANES, (s + 1) * LANES)
            if dil == 1:
                o = ob[0, 0, :, sl].astype(jnp.float32)
            else:
                for r in range(dil):
                    ob_scr[s, pl.ds(r, tm // dil, stride=dil), :] = ob[0, r, :, sl].astype(jnp.float32)
                o = ob_scr[s]
            a_s = jnp.where(low, jnp.broadcast_to(a[:, 2 * s:2 * s + 1], (tm, LANES)),
                            jnp.broadcast_to(a[:, 2 * s + 1:2 * s + 2], (tm, LANES)))
            yb[s] = yb[s] + a_s * o
    yb = jnp.concatenate(yb, axis=1)
    merged = merged + gate(1) * jnp.dot(yb.astype(jnp.bfloat16), wb_ref[...],
                                        preferred_element_type=jnp.float32)
    merged = merged + gate(2) * jnp.dot(ym_ref[0], wm_ref[...], preferred_element_type=jnp.float32)

    h = x + jnp.dot(merged.astype(jnp.bfloat16), wo_ref[...], preferred_element_type=jnp.float32)
    h_ref[0] = h
    xn = _rms(h, gffn_ref[...])
    xn_ref[0] = xn.astype(jnp.bfloat16)
    hi, lo = _split_bf16(xn)
    rh = jnp.dot(hi, wr2_ref[...], preferred_element_type=jnp.float32)
    rlo = jnp.dot(lo, wr2_ref[...], preferred_element_type=jnp.float32)
    rl_ref[0] = rh[:, :LANES] + rh[:, LANES:] + rlo[:, :LANES]


def _merge(x3, g_mix, wg, bg, ya, wa, obs, lss, wb, ym, wm, wo, g_ffn, wr2):
    b, s, d = x3.shape
    tm = min(256, s)
    row = lambda w: pl.BlockSpec((1, tm, w), lambda bi, i: (bi, i, 0))
    full = lambda a: pl.BlockSpec(a.shape, lambda bi, i: (0,) * a.ndim)

    def dil_spec(a):
        dil, width = a.shape[1], a.shape[3]
        return pl.BlockSpec((1, dil, tm // dil, width), lambda bi, i: (bi, 0, i, 0))

    return pl.pallas_call(
        _merge_kernel,
        grid=(b, s // tm),
        in_specs=[row(d), full(g_mix), full(wg), full(bg), row(ya.shape[2]), full(wa),
                  dil_spec(obs[0]), dil_spec(obs[1]), dil_spec(obs[2]),
                  dil_spec(lss[0]), dil_spec(lss[1]), dil_spec(lss[2]), full(wb),
                  row(ym.shape[2]), full(wm), full(wo), full(g_ffn), full(wr2)],
        out_specs=[row(d), row(d), row(LANES)],
        out_shape=[jax.ShapeDtypeStruct((b, s, d), jnp.float32),
                   jax.ShapeDtypeStruct((b, s, d), jnp.bfloat16),
                   jax.ShapeDtypeStruct((b, s, LANES), jnp.float32)],
        scratch_shapes=[pltpu.VMEM((2, tm, LANES), jnp.float32),
                        pltpu.VMEM((SLABS, tm, LANES), jnp.float32)],
        compiler_params=_cparams(("parallel", "parallel")),
        name="merge",
    )(x3, g_mix, wg, bg, ya, wa, obs[0], obs[1], obs[2], lss[0], lss[1], lss[2], wb,
      ym, wm, wo, g_ffn, wr2)


def _first_lane_where(cond, lane):
    return jnp.min(jnp.where(cond, lane, float(LANES)), axis=-1, keepdims=True)


def _route(logits, rb):
    z = logits + rb
    lane = lax.broadcasted_iota(jnp.int32, z.shape, 1).astype(jnp.float32)
    is_g = lane < MOE_GROUPS
    zg = jnp.where(is_g, z, -jnp.inf)
    mg = jnp.max(zg, axis=-1, keepdims=True)
    eg = jnp.exp(zg - mg)
    gp = eg / jnp.sum(eg, axis=-1, keepdims=True)
    g_w = jnp.max(gp, axis=-1, keepdims=True)
    g_sel = _first_lane_where(is_g & (gp == g_w), lane)
    lo = R_SUB0 + g_sel * EXPERTS_PER_GROUP
    in_grp = (lane >= lo) & (lane < lo + EXPERTS_PER_GROUP)
    zs = jnp.where(in_grp, z, -jnp.inf)
    ms = jnp.max(zs, axis=-1, keepdims=True)
    es = jnp.exp(zs - ms)
    sp = es / jnp.sum(es, axis=-1, keepdims=True)
    p1 = jnp.max(sp, axis=-1, keepdims=True)
    i1 = _first_lane_where(in_grp & (sp == p1), lane)
    rest = in_grp & (lane != i1)
    sp2 = jnp.where(rest, sp, -1.0)
    p2 = jnp.max(sp2, axis=-1, keepdims=True)
    i2 = _first_lane_where(rest & (sp2 == p2), lane)
    tot = p1 + p2
    comb = jnp.where(lane == i1, g_w * (p1 / tot), jnp.where(lane == i2, g_w * (p2 / tot), 0.0))
    return comb, g_sel


def _moe_kernel(xn_ref, rl_ref, rb_ref, ltri_ref, w1_ref, w3_ref, w2_ref, o_ref,
                xs_ref, cs_ref, acc_ref, perm_ref, permt_ref, blk_ref):
    pr = pl.program_id(1)
    tm = xn_ref.shape[0]

    @pl.when(pr == 0)
    def _():
        comb, g_sel = _route(rl_ref[...], rb_ref[...])
        lane = lax.broadcasted_iota(jnp.int32, (tm, LANES), 1).astype(jnp.float32)
        onehot = jnp.where(lane == g_sel, 1.0, 0.0)
        seen = jnp.dot(ltri_ref[...], onehot.astype(jnp.bfloat16), preferred_element_type=jnp.float32)
        counts = seen[tm - 1:tm, :]
        lane_row = lane[0:1, :]
        off = 0.0
        off_row = jnp.zeros((1, LANES), jnp.float32)
        for g in range(MOE_GROUPS):
            n_g = jnp.sum(jnp.where(lane_row == g, counts, 0.0))
            off_row = jnp.where(lane_row == g, off, off_row)
            first = off.astype(jnp.int32) if g else jnp.int32(0)
            start = (first // 16) * 16
            span = first - start + n_g.astype(jnp.int32)
            tail = span % MOE_BLK
            blk_ref[g] = start
            blk_ref[MOE_GROUPS + g] = span // MOE_BLK + (tail > MOE_BLK // 2).astype(jnp.int32)
            blk_ref[2 * MOE_GROUPS + g] = ((tail > 0) & (tail <= MOE_BLK // 2)).astype(jnp.int32)
            off = off + n_g
        dest = jnp.sum(onehot * (off_row + seen - 1.0), axis=-1, keepdims=True)
        dest_row = jnp.transpose(jnp.broadcast_to(dest, (tm, LANES)))[0:1, :]
        row = lax.broadcasted_iota(jnp.int32, (tm, LANES), 0).astype(jnp.float32)
        for s in range(tm // LANES):
            sl = slice(s * LANES, (s + 1) * LANES)
            permt_ref[:, sl] = jnp.where(lane + float(s * LANES) == dest, 1.0, 0.0).astype(jnp.bfloat16)
            perm_ref[:, sl] = jnp.where(row == dest_row[:, sl], 1.0, 0.0).astype(jnp.bfloat16)
        perm = perm_ref[...]
        xs_ref[0:tm, :] = jnp.dot(perm, xn_ref[...], preferred_element_type=jnp.float32).astype(jnp.bfloat16)
        hi, lo = _split_bf16(comb)
        cs_ref[0:tm, :] = (jnp.dot(perm, hi, preferred_element_type=jnp.float32)
                           + jnp.dot(perm, lo, preferred_element_type=jnp.float32))
        xs_ref[tm:, :] = jnp.zeros((MOE_BLK, xs_ref.shape[1]), jnp.bfloat16)
        cs_ref[tm:, :] = jnp.zeros((MOE_BLK, LANES), jnp.float32)
        acc_ref[...] = jnp.zeros_like(acc_ref)

    g = pr // (EXPERTS_PER_GROUP // MOE_STEP)
    start = blk_ref[g]
    nfull = blk_ref[MOE_GROUPS + g]

    def run_experts(r0, nrows):
        rows = pl.ds(pl.multiple_of(r0, 16), nrows)
        xb = xs_ref[rows, :]
        cb = cs_ref[rows, :]
        lane_i = lax.broadcasted_iota(jnp.int32, (nrows, LANES), 1)
        y = jnp.zeros((nrows, o_ref.shape[1]), jnp.float32)
        for e2 in range(MOE_STEP):
            c = jnp.sum(jnp.where(lane_i == R_SUB0 + MOE_STEP * pr + e2, cb, 0.0), axis=-1, keepdims=True)
            a = jnp.dot(xb, w1_ref[e2], preferred_element_type=jnp.float32)
            u = jnp.dot(xb, w3_ref[e2], preferred_element_type=jnp.float32)
            hid = (jax.nn.silu(a) * u * c).astype(jnp.bfloat16)
            y = y + jnp.dot(hid, w2_ref[e2], preferred_element_type=jnp.float32)
        acc_ref[rows, :] += y

    def block(b, carry):
        run_experts(start + b * MOE_BLK, MOE_BLK)
        return carry

    lax.fori_loop(0, nfull, block, 0)

    @pl.when(blk_ref[2 * MOE_GROUPS + g] > 0)
    def _():
        run_experts(start + nfull * MOE_BLK, MOE_BLK // 2)

    @pl.when(pr == N_EXPERTS // MOE_STEP - 1)
    def _():
        o_ref[...] = jnp.dot(permt_ref[...], acc_ref[0:tm, :].astype(jnp.bfloat16),
                             preferred_element_type=jnp.float32).astype(o_ref.dtype)


def _moe(xn, rl, rb, w1, w3, w2):
    t, d = xn.shape
    hid = w1.shape[2]
    tm = min(1024, t)
    ltri = (jnp.arange(tm)[:, None] >= jnp.arange(tm)[None, :]).astype(jnp.bfloat16)
    return pl.pallas_call(
        _moe_kernel,
        grid=(t // tm, N_EXPERTS // MOE_STEP),
        in_specs=[
            pl.BlockSpec((tm, d), lambda i, p: (i, 0)),
            pl.BlockSpec((tm, LANES), lambda i, p: (i, 0)),
            pl.BlockSpec((1, LANES), lambda i, p: (0, 0)),
            pl.BlockSpec((tm, tm), lambda i, p: (0, 0)),
            pl.BlockSpec((MOE_STEP, d, hid), lambda i, p: (p, 0, 0)),
            pl.BlockSpec((MOE_STEP, d, hid), lambda i, p: (p, 0, 0)),
            pl.BlockSpec((MOE_STEP, hid, d), lambda i, p: (p, 0, 0)),
        ],
        out_specs=pl.BlockSpec((tm, d), lambda i, p: (i, 0)),
        out_shape=jax.ShapeDtypeStruct((t, d), jnp.bfloat16),
        scratch_shapes=[
            pltpu.VMEM((tm + MOE_BLK, d), jnp.bfloat16),
            pltpu.VMEM((tm + MOE_BLK, LANES), jnp.float32),
            pltpu.VMEM((tm + MOE_BLK, d), jnp.float32),
            pltpu.VMEM((tm, tm), jnp.bfloat16),
            pltpu.VMEM((tm, tm), jnp.bfloat16),
            pltpu.SMEM((3 * MOE_GROUPS,), jnp.int32),
        ],
        compiler_params=_cparams(("parallel", "arbitrary")),
        name="moe",
    )(xn, rl, rb, ltri, w1, w3, w2)


def _final_kernel(h_ref, m_ref, g_ref, o_ref):
    o_ref[...] = _rms(h_ref[...] + m_ref[...].astype(jnp.float32), g_ref[...])


def _final(h, moe, g_final):
    t, d = h.shape
    tm = min(512, t)
    return pl.pallas_call(
        _final_kernel,
        grid=(t // tm,),
        in_specs=[pl.BlockSpec((tm, d), lambda i: (i, 0)), pl.BlockSpec((tm, d), lambda i: (i, 0)),
                  pl.BlockSpec((1, d), lambda i: (0, 0))],
        out_specs=pl.BlockSpec((tm, d), lambda i: (i, 0)),
        out_shape=jax.ShapeDtypeStruct((t, d), jnp.float32),
        compiler_params=_cparams(("parallel",)),
        name="final_norm",
    )(h, moe, g_final)


def _pad_heads_kv(w):
    d = w.shape[0]
    group = A_HEADS // A_KV_HEADS
    w = w.reshape(d, A_KV_HEADS, group, HEAD_DIM)
    parts = [jnp.pad(w[:, j], ((0, 0), (0, 0), (j * HEAD_DIM, LANES - (j + 1) * HEAD_DIM)))
             for j in range(A_KV_HEADS)]
    return jnp.concatenate(parts, axis=1).reshape(d, A_HEADS * LANES)


def _pack_w_in(w_in):
    parts, off = [], 0
    for n in IN_SPLITS:
        parts.append(w_in[:, off:off + n])
        off += n
    aq, ak, av, iq, ik, iw, bq, bk, bv, mq = parts
    d = w_in.shape[0]
    qscale = HEAD_DIM ** -0.5
    aq_x = _pad_heads_kv(aq * (qscale * math.log2(math.e)))
    iq_x = jnp.pad((iq * qscale).reshape(d, IDX_HEADS, HEAD_DIM),
                   ((0, 0), (0, 0), (0, LANES - HEAD_DIM))).reshape(d, IDX_HEADS * LANES)
    ik_x = jnp.pad(ik, ((0, 0), (0, LANES - ik.shape[1])))
    iw2 = jnp.concatenate([iw, iw], axis=1) * (IDX_HEADS ** -0.5)
    iw_x = jnp.pad(iw2, ((0, 0), (0, LANES - iw2.shape[1])))
    width = B_HEADS * HEAD_DIM
    bqs = bq * (qscale * math.log2(math.e))
    grp = lambda a, gi: a[:, gi * width:(gi + 1) * width]
    nat = jnp.concatenate([aq_x, iq_x, grp(bqs, 0), grp(bk, 0), ak, ik_x, av, iw_x, grp(bv, 0), mq], axis=1)
    dil = [jnp.concatenate([grp(bqs, gi), grp(bk, gi), grp(bv, gi)], axis=1) for gi in (1, 2)]
    return nat, dil


def _rope_tables(seq):
    half = HEAD_DIM // 2
    inv = ROPE_THETA ** (-jnp.arange(half, dtype=jnp.float32) / half)
    ang = jnp.arange(seq, dtype=jnp.float32)[:, None] * inv[None, :]
    cos = jnp.tile(jnp.cos(ang), (1, LANES // half))
    sign = jnp.tile(jnp.concatenate([-jnp.ones((half,), jnp.float32), jnp.ones((half,), jnp.float32)]),
                    LANES // HEAD_DIM)
    sin = jnp.tile(jnp.sin(ang), (1, LANES // half)) * sign[None, :]
    return cos, sin


def _layer(x3, mem, g_mix, g_mem, w_in, w_mem_kv, w_gate, b_gate, w_branch, w_out, g_ffn,
           w_group, b_group, w_sub, b_sub, w1, w3, w2, g_out):
    b, s, d = x3.shape
    bf = jnp.bfloat16
    cos_t, sin_t = _rope_tables(s)
    w_nat, w_dil = _pack_w_in(w_in)
    gm = g_mix[None, :]

    p4 = _in_proj(x3, gm, w_nat.astype(bf), cos_t, sin_t, P_MODES, 1)
    p3 = p4.reshape(b, s, p4.shape[3])
    srcs = [p4] + [_in_proj(x3, gm, w.astype(bf), cos_t, sin_t, D_MODES, B_PATTERNS[gi + 1][1])
                   for gi, w in enumerate(w_dil)]

    kv = _mem_kv(mem, g_mem[None, :], w_mem_kv.astype(bf))

    half = 256
    tri = (jnp.arange(half)[:, None] <= jnp.arange(half)[None, :]).astype(bf)
    ya = _dsa(p3, tri)

    obs, lss = [], []
    for gi, src in enumerate(srcs):
        tiles = (T_BQ, T_BK, T_BV) if gi == 0 else (0, 1, 2)
        o, l = _dilated(src, *tiles, gi)
        obs.append(o)
        lss.append(l)

    ym = _mem_attn(p3, kv)

    group = A_HEADS // A_KV_HEADS
    wa = jnp.swapaxes(w_branch[0].reshape(A_KV_HEADS, group, HEAD_DIM, d), 0, 1).reshape(-1, d).astype(bf)
    w_route = jnp.concatenate([w_group, jnp.moveaxis(w_sub, 0, 1).reshape(d, N_EXPERTS)], axis=1)
    w_route = jnp.pad(w_route, ((0, 0), (0, LANES - w_route.shape[1])))
    wrh = w_route.astype(bf)
    wrl = (w_route - wrh.astype(jnp.float32)).astype(bf)
    wr2 = jnp.concatenate([wrh, wrl], axis=1)
    r_bias = jnp.pad(jnp.concatenate([b_group, b_sub.reshape(-1)]), (0, LANES - MOE_GROUPS - N_EXPERTS))

    h, xn, rl = _merge(x3, gm, w_gate.astype(bf), b_gate[None, :], ya, wa, obs, lss,
                       w_branch[1].astype(bf), ym, w_branch[2].astype(bf), w_out.astype(bf),
                       g_ffn[None, :], wr2)
    t = b * s
    moe = _moe(xn.reshape(t, d), rl.reshape(t, LANES), r_bias[None, :],
               w1.astype(bf), w3.astype(bf), w2.astype(bf))
    return _final(h.reshape(t, d), moe, g_out[None, :]).reshape(b, s, d)


def kernel(x, mem, g_mix, g_mem, w_in, w_mem_kv, w_gate, b_gate, w_branch, w_out, g_ffn,
           w_group, b_group, w_sub, b_sub, w1, w3, w2, g_final):
    depth = g_mix.shape[0]
    assert depth == 1, "the final rmsnorm is applied right after the single layer"
    return _layer(x, mem, g_mix[0], g_mem[0], w_in[0], w_mem_kv[0], w_gate[0], b_gate[0], w_branch[0],
                  w_out[0], g_ffn[0], w_group[0], b_group[0], w_sub[0], b_sub[0], w1[0], w3[0], w2[0],
                  g_final)
```

```python
import functools
import math

import jax
import jax.numpy as jnp
from jax import lax
from jax.experimental import pallas as pl
from jax.experimental.pallas import tpu as pltpu

HEAD_DIM = 64
ROPE_THETA = 10000.0
RMS_EPS = 1e-6
A_HEADS = 8
A_KV_HEADS = 2
IDX_HEADS = 4
TOPK_MAX = 256
B_PATTERNS = ((128, 1), (512, 4), (2048, 16))
B_HEADS = 8
M_HEADS = 4
M_HEAD_DIM = 128
MOE_GROUPS = 4
EXPERTS_PER_GROUP = 4
N_EXPERTS = 16
IN_SPLITS = (512, 128, 128, 256, 64, 4, 1536, 1536, 1536, 512)

LANES = 128
PROJ_TILE = 512
SLABS = PROJ_TILE // LANES
VMEM_LIMIT = 56 * 1024 * 1024

T_AQ, T_IQ, T_BQ, T_BK, T_MIX, T_BV, T_MQ = 0, 2, 3, 4, 5, 6, 7
P_MODES = (("rope",) * 4,) * 5 + (("rope", "rope", "plain", "hi_lo"),) + (("plain",) * 4,) * 2
SLAB_AK = T_MIX * SLABS + 0
SLAB_IK = T_MIX * SLABS + 1
SLAB_AV = T_MIX * SLABS + 2
SLAB_IW = T_MIX * SLABS + 3
D_MODES = (("rope",) * 4, ("rope",) * 4, ("plain",) * 4)

INT_MIN = -(2 ** 31)
NEG_BIG = -1e30

R_SUB0 = MOE_GROUPS

MOE_BLK = 256
MOE_STEP = 4


def _cparams(sem):
    return pltpu.CompilerParams(dimension_semantics=sem, vmem_limit_bytes=VMEM_LIMIT)


def _rms(xf, g):
    return xf * lax.rsqrt(jnp.mean(xf * xf, axis=-1, keepdims=True) + RMS_EPS) * g


def _rope_slab(y, cos, sin_signed, first_half):
    partner = jnp.where(first_half, pltpu.roll(y, 96, 1), pltpu.roll(y, 32, 1))
    return y * cos + partner * sin_signed


def _in_proj_kernel(x_ref, g_ref, w_ref, cos_ref, sin_ref, o_ref, n_ref, *stage, modes, dil, rows):
    j = pl.program_id(2)
    tm = x_ref.shape[1]

    @pl.when(j == 0)
    def _():
        n_ref[...] = _rms(x_ref[0], g_ref[...]).astype(jnp.bfloat16)

    lane = lax.broadcasted_iota(jnp.int32, (rows, LANES), 1)
    first_half = (lane & (HEAD_DIM - 1)) < (HEAD_DIM // 2)

    def epilogue(slab_modes):
        for r in range(tm // rows):
            rs = pl.ds(r * rows, rows)
            acc = jnp.dot(n_ref[rs, :], w_ref[...], preferred_element_type=jnp.float32)
            cos = cos_ref[rs, :]
            sin = sin_ref[rs, :]
            for s in range(SLABS):
                y = acc[:, s * LANES:(s + 1) * LANES]
                if slab_modes[s] == "rope":
                    y = _rope_slab(y, cos, sin, first_half)
                elif slab_modes[s] == "hi_lo":
                    resid = y - y.astype(jnp.bfloat16).astype(jnp.float32)
                    y = jnp.where(lane < IDX_HEADS, y, resid)
                if dil == 1:
                    o_ref[0, 0, rs, s * LANES:(s + 1) * LANES] = y.astype(o_ref.dtype)
                else:
                    stage[0][s, rs, :] = y
        if dil > 1:
            for r in range(dil):
                for s in range(SLABS):
                    o_ref[0, r, :, s * LANES:(s + 1) * LANES] = (
                        stage[0][s, pl.ds(r, tm // dil, stride=dil), :].astype(o_ref.dtype))

    for pattern in sorted(set(modes)):
        tiles = [t for t, m in enumerate(modes) if m == pattern]
        cond = functools.reduce(jnp.logical_or, [j == t for t in tiles])
        pl.when(cond)(functools.partial(epilogue, pattern))


def _in_proj(x3, g_mix, wp, cos_t, sin_t, modes, dil):
    b, s, d = x3.shape
    tm = min(1024, s)
    ntiles = len(modes)
    scratch = [pltpu.VMEM((tm, d), jnp.bfloat16)]
    if dil > 1:
        scratch.append(pltpu.VMEM((SLABS, tm, LANES), jnp.float32))
    return pl.pallas_call(
        functools.partial(_in_proj_kernel, modes=modes, dil=dil, rows=128),
        grid=(b, s // tm, ntiles),
        in_specs=[
            pl.BlockSpec((1, tm, d), lambda bi, i, j: (bi, i, 0)),
            pl.BlockSpec((1, d), lambda bi, i, j: (0, 0)),
            pl.BlockSpec((d, PROJ_TILE), lambda bi, i, j: (0, j)),
            pl.BlockSpec((tm, LANES), lambda bi, i, j: (i, 0)),
            pl.BlockSpec((tm, LANES), lambda bi, i, j: (i, 0)),
        ],
        out_specs=pl.BlockSpec((1, dil, tm // dil, PROJ_TILE), lambda bi, i, j: (bi, 0, i, j)),
        out_shape=jax.ShapeDtypeStruct((b, dil, s // dil, ntiles * PROJ_TILE), jnp.bfloat16),
        scratch_shapes=scratch,
        compiler_params=_cparams(("parallel", "parallel", "arbitrary")),
        name=f"in_proj_d{dil}",
    )(x3, g_mix, wp, cos_t, sin_t)


def _mem_kv_kernel(m_ref, g_ref, w_ref, o_ref):
    n = _rms(m_ref[0], g_ref[...]).astype(jnp.bfloat16)
    o_ref[0] = jnp.dot(n, w_ref[...], preferred_element_type=jnp.float32).astype(o_ref.dtype)


def _mem_kv(mem, g_mem, w_kv):
    b, m, d = mem.shape
    n = w_kv.shape[1]
    return pl.pallas_call(
        _mem_kv_kernel,
        grid=(b,),
        in_specs=[
            pl.BlockSpec((1, m, d), lambda i: (i, 0, 0)),
            pl.BlockSpec((1, d), lambda i: (0, 0)),
            pl.BlockSpec((d, n), lambda i: (0, 0)),
        ],
        out_specs=pl.BlockSpec((1, m, n), lambda i: (i, 0, 0)),
        out_shape=jax.ShapeDtypeStruct((b, m, n), jnp.bfloat16),
        compiler_params=_cparams(("parallel",)),
        name="mem_kv",
    )(mem, g_mem, w_kv)


def _key_to_float(key):
    bits = jnp.where(key < 0, key ^ jnp.int32(0x7FFFFFFF), key)
    f = lax.bitcast_convert_type(bits, jnp.float32)
    return jnp.where((key < 0) & (f != f), -jnp.inf, f)


def _dsa_kernel(aq_ref, iq_ref, iw_ref, ik_ref, ak_ref, av_ref, tri_ref, o_ref,
                sc_ref, lists_ref, key_ref, thr_ref, need_ref, off_ref, m_ref, acc_ref,
                *, tq, kc, topk, depth, ins_rows, bis_rows, steps_per_chunk):
    i = pl.program_id(1)
    nt = pl.num_programs(1) - 1
    slot = i % 2
    pslot = 1 - slot
    nslab = kc // LANES
    group = A_HEADS // A_KV_HEADS
    nbits = 32
    blocks = [pl.ds(rb * bis_rows, bis_rows) for rb in range(tq // bis_rows)]

    def chunks_of(tile):
        return (tile * tq + tq + kc - 1) // kc

    @pl.when(i < nt)
    def _():
        q0 = i * tq
        iw = iw_ref[0].astype(jnp.float32)
        w_heads = [iw[:, h:h + 1] + iw[:, IDX_HEADS + h:IDX_HEADS + h + 1] for h in range(IDX_HEADS)]
        qpos = q0 + lax.broadcasted_iota(jnp.int32, (tq, kc), 0)
        kiota = lax.broadcasted_iota(jnp.int32, (tq, kc), 1)

        def score_chunk(c, carry):
            k0 = pl.multiple_of(c * kc, kc)
            ikc = ik_ref[0, pl.ds(k0, kc), :]
            s = jnp.zeros((tq, kc), jnp.float32)
            for h in range(IDX_HEADS):
                d = lax.dot_general(iq_ref[0, :, h * LANES:(h + 1) * LANES], ikc,
                                    (((1,), (1,)), ((), ())), preferred_element_type=jnp.float32)
                s = s + jnp.maximum(d, 0.0) * w_heads[h]
            s = jnp.where(kiota + k0 <= qpos, s, -jnp.inf)
            sc_ref[slot, c] = s
            for rg in range(tq // ins_rows):
                rows = slice(rg * ins_rows, (rg + 1) * ins_rows)
                tops = [lists_ref[rows, j * LANES:(j + 1) * LANES] for j in range(depth)]
                for sl in range(nslab):
                    x = s[rows, sl * LANES:(sl + 1) * LANES]
                    for j in range(depth):
                        hi = jnp.maximum(tops[j], x)
                        x = jnp.minimum(tops[j], x)
                        tops[j] = hi
                for j in range(depth):
                    lists_ref[rows, j * LANES:(j + 1) * LANES] = tops[j]
            return carry

        lists_ref[...] = jnp.full(lists_ref.shape, -jnp.inf, jnp.float32)
        lax.fori_loop(0, chunks_of(i), score_chunk, 0)
        key_ref[...] = jnp.full(key_ref.shape, INT_MIN, jnp.int32)

    def list_count(rs, cand, strict):
        acc = jnp.zeros((bis_rows, LANES), jnp.float32)
        for j in range(depth):
            v = lists_ref[rs, j * LANES:(j + 1) * LANES]
            acc = acc + jnp.where((v > cand) if strict else (v >= cand), 1.0, 0.0)
        return jnp.sum(acc, axis=-1, keepdims=True)

    def sorted_count(rs, cand):
        lv = [lists_ref[rs, j * LANES:(j + 1) * LANES] for j in range(depth)]
        a = lv[5] >= cand
        b = jnp.where(a, lv[8], lv[2]) >= cand
        c = jnp.where(a, jnp.where(b, lv[10], lv[6]), jnp.where(b, lv[3], lv[0])) >= cand
        d = jnp.where(a, jnp.where(b, jnp.where(c, lv[11], lv[9]), lv[7]),
                      jnp.where(b, lv[4], lv[1])) >= cand
        per_lane = (jnp.where(a, 6.0, 0.0) + jnp.where(b, 3.0, 0.0) + jnp.where(c, 1.0, 0.0)
                    + jnp.where(d, 1.0, 0.0) + jnp.where(a & b & c, 1.0, 0.0))
        return jnp.sum(per_lane, axis=-1, keepdims=True)

    def bit_step(it):
        for rs in blocks:
            pre = key_ref[rs, :]
            cand = pre + lax.shift_left(jnp.int32(1), nbits - 1 - it)
            cnt = sorted_count(rs, _key_to_float(cand))
            key_ref[rs, :] = jnp.where(cnt >= float(topk), cand, pre)

    half = tri_ref.shape[0]

    def attn_chunk(c):
        k0 = pl.multiple_of(c * kc, kc)
        akc = ak_ref[0, pl.ds(k0, kc), :]
        avc = av_ref[0, pl.ds(k0, kc), :]
        thr_b = thr_ref[pslot]
        need_b = need_ref[pslot]
        bias = []
        off = off_ref[...]
        for hh in range(kc // half):
            eq_parts = []
            for s in range(half // LANES):
                v = sc_ref[pslot, c, :, hh * half + s * LANES: hh * half + (s + 1) * LANES]
                eq_parts.append(jnp.where(v == thr_b, 1.0, 0.0).astype(jnp.bfloat16))
            eqf = jnp.concatenate(eq_parts, axis=1)
            pre = jnp.dot(eqf, tri_ref[...], preferred_element_type=jnp.float32)
            for s in range(half // LANES):
                v = sc_ref[pslot, c, :, hh * half + s * LANES: hh * half + (s + 1) * LANES]
                rank = pre[:, s * LANES:(s + 1) * LANES] + off
                tie_ok = (v == thr_b) & (rank <= need_b)
                sel = (v > thr_b) | tie_ok
                bias.append(jnp.where(sel, 0.0, NEG_BIG))
            off = off + jnp.sum(eqf.astype(jnp.float32), axis=-1, keepdims=True)
        off_ref[...] = off

        av_ones = jnp.concatenate([avc, jnp.ones_like(avc)], axis=1)
        for h in range(A_HEADS):
            logits = lax.dot_general(aq_ref[0, :, h * LANES:(h + 1) * LANES], akc,
                                     (((1,), (1,)), ((), ())), preferred_element_type=jnp.float32)
            lg = [logits[:, s * LANES:(s + 1) * LANES] + bias[s] for s in range(nslab)]
            m_old = m_ref[h]
            m_cur = functools.reduce(jnp.maximum, lg)
            m_new = jnp.maximum(m_old, jnp.max(m_cur, axis=-1, keepdims=True))
            alpha = jnp.exp2(m_old - m_new)
            p = jnp.concatenate([jnp.exp2((x - m_new).astype(jnp.bfloat16)) for x in lg], axis=1)
            m_ref[h] = m_new
            pv = jnp.dot(p, av_ones, preferred_element_type=jnp.float32)
            acc_ref[h] = jnp.concatenate([alpha, alpha], axis=1) * acc_ref[h] + pv

    nprev = chunks_of(i - 1)
    nride = jnp.minimum(nprev, nbits // steps_per_chunk)

    @pl.when(i >= 1)
    def _():
        off_ref[...] = jnp.zeros_like(off_ref)
        m_ref[...] = jnp.full(m_ref.shape, NEG_BIG, jnp.float32)
        acc_ref[...] = jnp.zeros_like(acc_ref)

        def chunk_with_steps(c, carry):
            attn_chunk(c)
            for j in range(steps_per_chunk):
                bit_step(c * steps_per_chunk + j)
            return carry

        def chunk_plain(c, carry):
            attn_chunk(c)
            return carry

        lax.fori_loop(0, nride, chunk_with_steps, 0)
        lax.fori_loop(nride, nprev, chunk_plain, 0)

        low = lax.broadcasted_iota(jnp.int32, (tq, LANES), 1) < HEAD_DIM
        for p in range(group):
            outs = [acc_ref[h, :, :LANES] / acc_ref[h, :, LANES:] for h in (p, group + p)]
            o_ref[0, :, p * LANES:(p + 1) * LANES] = jnp.where(low, outs[0], outs[1]).astype(o_ref.dtype)

    @pl.when(i < nt)
    def _():
        nchunks = chunks_of(i)
        done = jnp.where(i >= 1, nride * steps_per_chunk, 0)

        def rest(it, carry):
            bit_step(it)
            return carry

        lax.fori_loop(done, nbits, rest, 0)

        def finish(rs, thr, c_gt):
            thr_ref[slot, rs, :] = thr
            need_ref[slot, rs, :] = jnp.where(thr == -jnp.inf, 0.0, float(topk) - c_gt)

        def full_count(rs, cand, strict):
            def body(c, acc):
                for s in range(nslab):
                    v = sc_ref[slot, c, rs, s * LANES:(s + 1) * LANES]
                    acc = acc + jnp.where((v > cand) if strict else (v >= cand), 1.0, 0.0)
                return acc
            acc = lax.fori_loop(0, nchunks, body, jnp.zeros((bis_rows, LANES), jnp.float32))
            return jnp.sum(acc, axis=-1, keepdims=True)

        def redo_block(rs):
            def step(it, prefix):
                cand = prefix + lax.shift_left(jnp.int32(1), nbits - 1 - it)
                cnt = full_count(rs, _key_to_float(cand), False)
                return jnp.where(cnt >= float(topk), cand, prefix)
            key = lax.fori_loop(0, nbits, step, jnp.full((bis_rows, LANES), INT_MIN, jnp.int32))
            thr = _key_to_float(key)
            finish(rs, thr, full_count(rs, thr, True))

        for rs in blocks:
            thr = _key_to_float(key_ref[rs, :])
            finish(rs, thr, list_count(rs, thr, True))
            last = lists_ref[rs, (depth - 1) * LANES:depth * LANES]
            overflow = jnp.max(jnp.where(last > thr, 1.0, 0.0))
            pl.when(overflow > 0.0)(functools.partial(redo_block, rs))


def _dsa(p3, tri):
    b, s, _ = p3.shape
    tq = min(256, s)
    kc = min(512, s)
    nt = s // tq
    topk = min(TOPK_MAX, s // 4)
    depth = 12
    kern = functools.partial(_dsa_kernel, tq=tq, kc=kc, topk=topk, depth=depth,
                             ins_rows=16, bis_rows=min(64, tq), steps_per_chunk=4)
    wide = A_HEADS * LANES
    prev = lambda i: jnp.maximum(i - 1, 0)
    cur = lambda i: jnp.minimum(i, nt - 1)
    return pl.pallas_call(
        kern,
        grid=(b, nt + 1),
        in_specs=[
            pl.BlockSpec((1, tq, wide), lambda bi, i: (bi, prev(i), T_AQ)),
            pl.BlockSpec((1, tq, PROJ_TILE), lambda bi, i: (bi, cur(i), T_IQ)),
            pl.BlockSpec((1, tq, LANES), lambda bi, i: (bi, cur(i), SLAB_IW)),
            pl.BlockSpec((1, s, LANES), lambda bi, i: (bi, 0, SLAB_IK)),
            pl.BlockSpec((1, s, LANES), lambda bi, i: (bi, 0, SLAB_AK)),
            pl.BlockSpec((1, s, LANES), lambda bi, i: (bi, 0, SLAB_AV)),
            pl.BlockSpec(tri.shape, lambda bi, i: (0, 0)),
        ],
        out_specs=pl.BlockSpec((1, tq, wide // 2), lambda bi, i: (bi, prev(i), 0)),
        out_shape=jax.ShapeDtypeStruct((b, s, wide // 2), jnp.bfloat16),
        scratch_shapes=[
            pltpu.VMEM((2, s // kc, tq, kc), jnp.float32),
            pltpu.VMEM((tq, depth * LANES), jnp.float32),
            pltpu.VMEM((tq, LANES), jnp.int32),
            pltpu.VMEM((2, tq, LANES), jnp.float32),
            pltpu.VMEM((2, tq, LANES), jnp.float32),
            pltpu.VMEM((tq, LANES), jnp.float32),
            pltpu.VMEM((A_HEADS, tq, LANES), jnp.float32),
            pltpu.VMEM((A_HEADS, tq, 2 * LANES), jnp.float32),
        ],
        compiler_params=_cparams(("parallel", "arbitrary")),
        name="dsa",
    )(p3, p3, p3, p3, p3, p3, tri)


def _dil_kernel(q_ref, kp_ref, kc_ref, vp_ref, vc_ref, o_ref, lse_ref, *, tu, w):
    u = pl.program_id(2)
    qq = lax.broadcasted_iota(jnp.int32, (2 * w, 2 * w), 0) % w
    kk = lax.broadcasted_iota(jnp.int32, (2 * w, 2 * w), 1)
    band = (kk >= qq) & (kk <= qq + w)
    band_first = band & (kk >= jnp.where(u > 0, 0, w))
    lane = lax.broadcasted_iota(jnp.int32, (w, LANES), 1)
    low = lane < HEAD_DIM
    for sb in range(tu // w):
        rows = slice(sb * w, (sb + 1) * w)
        lse_tile = jnp.zeros((w, LANES), jnp.float32)
        for pr in range(B_HEADS // 2):
            sl = slice(pr * LANES, (pr + 1) * LANES)
            q = q_ref[0, 0, rows, sl]
            if sb == 0:
                k = jnp.concatenate([kp_ref[0, 0, :, sl], kc_ref[0, 0, :w, sl]], axis=0)
                v = jnp.concatenate([vp_ref[0, 0, :, sl], vc_ref[0, 0, :w, sl]], axis=0)
            else:
                k = kc_ref[0, 0, (sb - 1) * w:(sb + 1) * w, sl]
                v = vc_ref[0, 0, (sb - 1) * w:(sb + 1) * w, sl]
            zero = jnp.zeros_like(q)
            qm = jnp.concatenate([jnp.where(low, q, zero), jnp.where(low, zero, q)], axis=0)
            lg = lax.dot_general(qm, k, (((1,), (1,)), ((), ())), preferred_element_type=jnp.float32)
            lg = jnp.where(band_first if sb == 0 else band, lg, NEG_BIG)
            m = jnp.max(lg, axis=-1, keepdims=True)
            p = jnp.exp2(lg - m).astype(jnp.bfloat16)
            nd = jnp.dot(p, jnp.concatenate([v, jnp.ones_like(v)], axis=1),
                         preferred_element_type=jnp.float32)
            l = nd[:, LANES:]
            o = nd[:, :LANES] / l
            o_ref[0, 0, rows, sl] = jnp.where(low, o[:w], o[w:]).astype(o_ref.dtype)
            lse = (m + jnp.log2(l)) * math.log(2.0)
            lse_tile = jnp.where(lane == 2 * pr, lse[:w], lse_tile)
            lse_tile = jnp.where(lane == 2 * pr + 1, lse[w:], lse_tile)
        lse_ref[0, 0, rows, :] = lse_tile


def _dilated(src, tq_tile, tk_tile, tv_tile, gi):
    b, dil, su, _ = src.shape
    window, d2 = B_PATTERNS[gi]
    assert d2 == dil
    w = window // dil
    tu = min(512, su)
    ratio = tu // w
    width = B_HEADS * HEAD_DIM
    cur = lambda t: pl.BlockSpec((1, 1, tu, width), lambda bi, r, u: (bi, r, u, t))
    prev = lambda t: pl.BlockSpec((1, 1, w, width),
                                  lambda bi, r, u: (bi, r, jnp.maximum(u * ratio - 1, 0), t))
    return pl.pallas_call(
        functools.partial(_dil_kernel, tu=tu, w=w),
        grid=(b, dil, su // tu),
        in_specs=[cur(tq_tile), prev(tk_tile), cur(tk_tile), prev(tv_tile), cur(tv_tile)],
        out_specs=[pl.BlockSpec((1, 1, tu, width), lambda bi, r, u: (bi, r, u, 0)),
                   pl.BlockSpec((1, 1, tu, LANES), lambda bi, r, u: (bi, r, u, 0))],
        out_shape=[jax.ShapeDtypeStruct((b, dil, su, width), jnp.bfloat16),
                   jax.ShapeDtypeStruct((b, dil, su, LANES), jnp.float32)],
        compiler_params=_cparams(("parallel", "parallel", "arbitrary")),
        name=f"dilated{gi}",
    )(src, src, src, src, src)


def _mem_attn_kernel(q_ref, k_ref, v_ref, o_ref):
    scale = M_HEAD_DIM ** -0.5
    for h in range(M_HEADS):
        sl = slice(h * M_HEAD_DIM, (h + 1) * M_HEAD_DIM)
        lg = lax.dot_general(q_ref[0, :, sl], k_ref[0, :, sl], (((1,), (1,)), ((), ())),
                             preferred_element_type=jnp.float32) * scale
        m = jnp.max(lg, axis=-1, keepdims=True)
        p = jnp.exp(lg - m)
        l = jnp.sum(p, axis=-1, keepdims=True)
        o = jnp.dot(p.astype(jnp.bfloat16), v_ref[0, :, sl], preferred_element_type=jnp.float32)
        o_ref[0, :, sl] = (o / l).astype(o_ref.dtype)


def _mem_attn(p3, kv):
    b, s, _ = p3.shape
    m = kv.shape[1]
    width = M_HEADS * M_HEAD_DIM
    tq = min(512, s)
    return pl.pallas_call(
        _mem_attn_kernel,
        grid=(b, s // tq),
        in_specs=[
            pl.BlockSpec((1, tq, width), lambda bi, i: (bi, i, T_MQ)),
            pl.BlockSpec((1, m, width), lambda bi, i: (bi, 0, 0)),
            pl.BlockSpec((1, m, width), lambda bi, i: (bi, 0, 1)),
        ],
        out_specs=pl.BlockSpec((1, tq, width), lambda bi, i: (bi, i, 0)),
        out_shape=jax.ShapeDtypeStruct((b, s, width), jnp.bfloat16),
        compiler_params=_cparams(("parallel", "parallel")),
        name="mem_attn",
    )(p3, kv, kv)


def _split_bf16(v):
    hi = v.astype(jnp.bfloat16)
    lo = (v - hi.astype(jnp.float32)).astype(jnp.bfloat16)
    return hi, lo


def _merge_kernel(x_ref, gmix_ref, wg_ref, bg_ref, ya_ref, wa_ref, ob0_ref, ob1_ref, ob2_ref,
                  ls0_ref, ls1_ref, ls2_ref, wb_ref, ym_ref, wm_ref, wo_ref, gffn_ref,
                  wr2_ref, h_ref, xn_ref, rl_ref, ls_scr, ob_scr):
    d = x_ref.shape[2]
    tm = x_ref.shape[1]
    x = x_ref[0]
    n = _rms(x, gmix_ref[...]).astype(jnp.bfloat16)

    def gate(k):
        z = jnp.dot(n, wg_ref[:, k * d:(k + 1) * d], preferred_element_type=jnp.float32)
        return jax.nn.sigmoid(z + bg_ref[:, k * d:(k + 1) * d])

    merged = gate(0) * jnp.dot(ya_ref[0], wa_ref[...], preferred_element_type=jnp.float32)

    ob_refs = (ob0_ref, ob1_ref, ob2_ref)
    ls = []
    for gi, ls_ref in enumerate((ls0_ref, ls1_ref, ls2_ref)):
        dil = ls_ref.shape[1]
        if dil == 1:
            ls.append(ls_ref[0, 0])
        else:
            for r in range(dil):
                ls_scr[gi - 1, pl.ds(r, tm // dil, stride=dil), :] = ls_ref[0, r]
            ls.append(ls_scr[gi - 1])
    mx = jnp.maximum(jnp.maximum(ls[0], ls[1]), ls[2])
    es = [jnp.exp(v - mx) for v in ls]
    inv = 1.0 / (es[0] + es[1] + es[2])
    yb = [jnp.zeros((tm, LANES), jnp.float32) for _ in range(SLABS)]
    low = lax.broadcasted_iota(jnp.int32, (tm, LANES), 1) < HEAD_DIM
    for e, ob in zip(es, ob_refs):
        dil = ob.shape[1]
        a = e * inv
        for s in range(SLABS):
            sl = slice(s * LANES, (s + 1) * LANES)
            if dil == 1:
                o = ob[0, 0, :, sl].astype(jnp.float32)
            else:
                for r in range(dil):
                    ob_scr[s, pl.ds(r, tm // dil, stride=dil), :] = ob[0, r, :, sl].astype(jnp.float32)
                o = ob_scr[s]
            a_s = jnp.where(low, jnp.broadcast_to(a[:, 2 * s:2 * s + 1], (tm, LANES)),
                            jnp.broadcast_to(a[:, 2 * s + 1:2 * s + 2], (tm, LANES)))
            yb[s] = yb[s] + a_s * o
    yb = jnp.concatenate(yb, axis=1)
    merged = merged + gate(1) * jnp.dot(yb.astype(jnp.bfloat16), wb_ref[...],
                                        preferred_element_type=jnp.float32)
    merged = merged + gate(2) * jnp.dot(ym_ref[0], wm_ref[...], preferred_element_type=jnp.float32)

    h = x + jnp.dot(merged.astype(jnp.bfloat16), wo_ref[...], preferred_element_type=jnp.float32)
    h_ref[0] = h
    xn = _rms(h, gffn_ref[...])
    xn_ref[0] = xn.astype(jnp.bfloat16)
    hi, lo = _split_bf16(xn)
    rh = jnp.dot(hi, wr2_ref[...], preferred_element_type=jnp.float32)
    rlo = jnp.dot(lo, wr2_ref[...], preferred_element_type=jnp.float32)
    rl_ref[0] = rh[:, :LANES] + rh[:, LANES:] + rlo[:, :LANES]


def _merge(x3, g_mix, wg, bg, ya, wa, obs, lss, wb, ym, wm, wo, g_ffn, wr2):
    b, s, d = x3.shape
    tm = min(512, s)
    row = lambda w: pl.BlockSpec((1, tm, w), lambda bi, i: (bi, i, 0))
    full = lambda a: pl.BlockSpec(a.shape, lambda bi, i: (0,) * a.ndim)

    def dil_spec(a):
        dil, width = a.shape[1], a.shape[3]
        return pl.BlockSpec((1, dil, tm // dil, width), lambda bi, i: (bi, 0, i, 0))

    return pl.pallas_call(
        _merge_kernel,
        grid=(b, s // tm),
        in_specs=[row(d), full(g_mix), full(wg), full(bg), row(ya.shape[2]), full(wa),
                  dil_spec(obs[0]), dil_spec(obs[1]), dil_spec(obs[2]),
                  dil_spec(lss[0]), dil_spec(lss[1]), dil_spec(lss[2]), full(wb),
                  row(ym.shape[2]), full(wm), full(wo), full(g_ffn), full(wr2)],
        out_specs=[row(d), row(d), row(LANES)],
        out_shape=[jax.ShapeDtypeStruct((b, s, d), jnp.float32),
                   jax.ShapeDtypeStruct((b, s, d), jnp.bfloat16),
                   jax.ShapeDtypeStruct((b, s, LANES), jnp.float32)],
        scratch_shapes=[pltpu.VMEM((2, tm, LANES), jnp.float32),
                        pltpu.VMEM((SLABS, tm, LANES), jnp.float32)],
        compiler_params=_cparams(("parallel", "parallel")),
        name="merge",
    )(x3, g_mix, wg, bg, ya, wa, obs[0], obs[1], obs[2], lss[0], lss[1], lss[2], wb,
      ym, wm, wo, g_ffn, wr2)


def _first_lane_where(cond, lane):
    return jnp.min(jnp.where(cond, lane, float(LANES)), axis=-1, keepdims=True)


def _route(logits, rb):
    z = logits + rb
    lane = lax.broadcasted_iota(jnp.int32, z.shape, 1).astype(jnp.float32)
    is_g = lane < MOE_GROUPS
    zg = jnp.where(is_g, z, -jnp.inf)
    mg = jnp.max(zg, axis=-1, keepdims=True)
    eg = jnp.exp(zg - mg)
    gp = eg / jnp.sum(eg, axis=-1, keepdims=True)
    g_w = jnp.max(gp, axis=-1, keepdims=True)
    g_sel = _first_lane_where(is_g & (gp == g_w), lane)
    lo = R_SUB0 + g_sel * EXPERTS_PER_GROUP
    in_grp = (lane >= lo) & (lane < lo + EXPERTS_PER_GROUP)
    zs = jnp.where(in_grp, z, -jnp.inf)
    ms = jnp.max(zs, axis=-1, keepdims=True)
    es = jnp.exp(zs - ms)
    sp = es / jnp.sum(es, axis=-1, keepdims=True)
    p1 = jnp.max(sp, axis=-1, keepdims=True)
    i1 = _first_lane_where(in_grp & (sp == p1), lane)
    rest = in_grp & (lane != i1)
    sp2 = jnp.where(rest, sp, -1.0)
    p2 = jnp.max(sp2, axis=-1, keepdims=True)
    i2 = _first_lane_where(rest & (sp2 == p2), lane)
    tot = p1 + p2
    comb = jnp.where(lane == i1, g_w * (p1 / tot), jnp.where(lane == i2, g_w * (p2 / tot), 0.0))
    return comb, g_sel


def _moe_kernel(xn_ref, rl_ref, rb_ref, ltri_ref, w1_ref, w3_ref, w2_ref, o_ref,
                xs_ref, cs_ref, acc_ref, perm_ref, permt_ref, blk_ref):
    pr = pl.program_id(1)
    tm = xn_ref.shape[0]

    @pl.when(pr == 0)
    def _():
        comb, g_sel = _route(rl_ref[...], rb_ref[...])
        lane = lax.broadcasted_iota(jnp.int32, (tm, LANES), 1).astype(jnp.float32)
        onehot = jnp.where(lane == g_sel, 1.0, 0.0)
        seen = jnp.dot(ltri_ref[...], onehot.astype(jnp.bfloat16), preferred_element_type=jnp.float32)
        counts = seen[tm - 1:tm, :]
        lane_row = lane[0:1, :]
        off = 0.0
        off_row = jnp.zeros((1, LANES), jnp.float32)
        for g in range(MOE_GROUPS):
            n_g = jnp.sum(jnp.where(lane_row == g, counts, 0.0))
            off_row = jnp.where(lane_row == g, off, off_row)
            first = off.astype(jnp.int32) if g else jnp.int32(0)
            start = (first // 16) * 16
            span = first - start + n_g.astype(jnp.int32)
            tail = span % MOE_BLK
            blk_ref[g] = start
            blk_ref[MOE_GROUPS + g] = span // MOE_BLK + (tail > MOE_BLK // 2).astype(jnp.int32)
            blk_ref[2 * MOE_GROUPS + g] = ((tail > 0) & (tail <= MOE_BLK // 2)).astype(jnp.int32)
            off = off + n_g
        dest = jnp.sum(onehot * (off_row + seen - 1.0), axis=-1, keepdims=True)
        dest_row = jnp.transpose(jnp.broadcast_to(dest, (tm, LANES)))[0:1, :]
        row = lax.broadcasted_iota(jnp.int32, (tm, LANES), 0).astype(jnp.float32)
        for s in range(tm // LANES):
            sl = slice(s * LANES, (s + 1) * LANES)
            permt_ref[:, sl] = jnp.where(lane + float(s * LANES) == dest, 1.0, 0.0).astype(jnp.bfloat16)
            perm_ref[:, sl] = jnp.where(row == dest_row[:, sl], 1.0, 0.0).astype(jnp.bfloat16)
        perm = perm_ref[...]
        xs_ref[0:tm, :] = jnp.dot(perm, xn_ref[...], preferred_element_type=jnp.float32).astype(jnp.bfloat16)
        hi, lo = _split_bf16(comb)
        cs_ref[0:tm, :] = (jnp.dot(perm, hi, preferred_element_type=jnp.float32)
                           + jnp.dot(perm, lo, preferred_element_type=jnp.float32))
        xs_ref[tm:, :] = jnp.zeros((MOE_BLK, xs_ref.shape[1]), jnp.bfloat16)
        cs_ref[tm:, :] = jnp.zeros((MOE_BLK, LANES), jnp.float32)
        acc_ref[...] = jnp.zeros_like(acc_ref)

    g = pr // (EXPERTS_PER_GROUP // MOE_STEP)
    start = blk_ref[g]
    nfull = blk_ref[MOE_GROUPS + g]

    def run_experts(r0, nrows):
        rows = pl.ds(pl.multiple_of(r0, 16), nrows)
        xb = xs_ref[rows, :]
        cb = cs_ref[rows, :]
        lane_i = lax.broadcasted_iota(jnp.int32, (nrows, LANES), 1)
        y = jnp.zeros((nrows, o_ref.shape[1]), jnp.float32)
        for e2 in range(MOE_STEP):
            c = jnp.sum(jnp.where(lane_i == R_SUB0 + MOE_STEP * pr + e2, cb, 0.0), axis=-1, keepdims=True)
            a = jnp.dot(xb, w1_ref[e2], preferred_element_type=jnp.float32)
            u = jnp.dot(xb, w3_ref[e2], preferred_element_type=jnp.float32)
            hid = (jax.nn.silu(a) * u * c).astype(jnp.bfloat16)
            y = y + jnp.dot(hid, w2_ref[e2], preferred_element_type=jnp.float32)
        acc_ref[rows, :] += y

    def block(b, carry):
        run_experts(start + b * MOE_BLK, MOE_BLK)
        return carry

    lax.fori_loop(0, nfull, block, 0)

    @pl.when(blk_ref[2 * MOE_GROUPS + g] > 0)
    def _():
        run_experts(start + nfull * MOE_BLK, MOE_BLK // 2)

    @pl.when(pr == N_EXPERTS // MOE_STEP - 1)
    def _():
        o_ref[...] = jnp.dot(permt_ref[...], acc_ref[0:tm, :].astype(jnp.bfloat16),
                             preferred_element_type=jnp.float32).astype(o_ref.dtype)


def _moe(xn, rl, rb, w1, w3, w2):
    t, d = xn.shape
    hid = w1.shape[2]
    tm = min(1024, t)
    ltri = (jnp.arange(tm)[:, None] >= jnp.arange(tm)[None, :]).astype(jnp.bfloat16)
    return pl.pallas_call(
        _moe_kernel,
        grid=(t // tm, N_EXPERTS // MOE_STEP),
        in_specs=[
            pl.BlockSpec((tm, d), lambda i, p: (i, 0)),
            pl.BlockSpec((tm, LANES), lambda i, p: (i, 0)),
            pl.BlockSpec((1, LANES), lambda i, p: (0, 0)),
            pl.BlockSpec((tm, tm), lambda i, p: (0, 0)),
            pl.BlockSpec((MOE_STEP, d, hid), lambda i, p: (p, 0, 0)),
            pl.BlockSpec((MOE_STEP, d, hid), lambda i, p: (p, 0, 0)),
            pl.BlockSpec((MOE_STEP, hid, d), lambda i, p: (p, 0, 0)),
        ],
        out_specs=pl.BlockSpec((tm, d), lambda i, p: (i, 0)),
        out_shape=jax.ShapeDtypeStruct((t, d), jnp.bfloat16),
        scratch_shapes=[
            pltpu.VMEM((tm + MOE_BLK, d), jnp.bfloat16),
            pltpu.VMEM((tm + MOE_BLK, LANES), jnp.float32),
            pltpu.VMEM((tm + MOE_BLK, d), jnp.float32),
            pltpu.VMEM((tm, tm), jnp.bfloat16),
            pltpu.VMEM((tm, tm), jnp.bfloat16),
            pltpu.SMEM((3 * MOE_GROUPS,), jnp.int32),
        ],
        compiler_params=_cparams(("parallel", "arbitrary")),
        name="moe",
    )(xn, rl, rb, ltri, w1, w3, w2)


def _final_kernel(h_ref, m_ref, g_ref, o_ref):
    o_ref[...] = _rms(h_ref[...] + m_ref[...].astype(jnp.float32), g_ref[...])


def _final(h, moe, g_final):
    t, d = h.shape
    tm = min(512, t)
    return pl.pallas_call(
        _final_kernel,
        grid=(t // tm,),
        in_specs=[pl.BlockSpec((tm, d), lambda i: (i, 0)), pl.BlockSpec((tm, d), lambda i: (i, 0)),
                  pl.BlockSpec((1, d), lambda i: (0, 0))],
        out_specs=pl.BlockSpec((tm, d), lambda i: (i, 0)),
        out_shape=jax.ShapeDtypeStruct((t, d), jnp.float32),
        compiler_params=_cparams(("parallel",)),
        name="final_norm",
    )(h, moe, g_final)


def _pad_heads_kv(w):
    d = w.shape[0]
    group = A_HEADS // A_KV_HEADS
    w = w.reshape(d, A_KV_HEADS, group, HEAD_DIM)
    parts = [jnp.pad(w[:, j], ((0, 0), (0, 0), (j * HEAD_DIM, LANES - (j + 1) * HEAD_DIM)))
             for j in range(A_KV_HEADS)]
    return jnp.concatenate(parts, axis=1).reshape(d, A_HEADS * LANES)


def _pack_w_in(w_in):
    parts, off = [], 0
    for n in IN_SPLITS:
        parts.append(w_in[:, off:off + n])
        off += n
    aq, ak, av, iq, ik, iw, bq, bk, bv, mq = parts
    d = w_in.shape[0]
    qscale = HEAD_DIM ** -0.5
    aq_x = _pad_heads_kv(aq * (qscale * math.log2(math.e)))
    iq_x = jnp.pad((iq * qscale).reshape(d, IDX_HEADS, HEAD_DIM),
                   ((0, 0), (0, 0), (0, LANES - HEAD_DIM))).reshape(d, IDX_HEADS * LANES)
    ik_x = jnp.pad(ik, ((0, 0), (0, LANES - ik.shape[1])))
    iw2 = jnp.concatenate([iw, iw], axis=1) * (IDX_HEADS ** -0.5)
    iw_x = jnp.pad(iw2, ((0, 0), (0, LANES - iw2.shape[1])))
    width = B_HEADS * HEAD_DIM
    bqs = bq * (qscale * math.log2(math.e))
    grp = lambda a, gi: a[:, gi * width:(gi + 1) * width]
    nat = jnp.concatenate([aq_x, iq_x, grp(bqs, 0), grp(bk, 0), ak, ik_x, av, iw_x, grp(bv, 0), mq], axis=1)
    dil = [jnp.concatenate([grp(bqs, gi), grp(bk, gi), grp(bv, gi)], axis=1) for gi in (1, 2)]
    return nat, dil


def _rope_tables(seq):
    half = HEAD_DIM // 2
    inv = ROPE_THETA ** (-jnp.arange(half, dtype=jnp.float32) / half)
    ang = jnp.arange(seq, dtype=jnp.float32)[:, None] * inv[None, :]
    cos = jnp.tile(jnp.cos(ang), (1, LANES // half))
    sign = jnp.tile(jnp.concatenate([-jnp.ones((half,), jnp.float32), jnp.ones((half,), jnp.float32)]),
                    LANES // HEAD_DIM)
    sin = jnp.tile(jnp.sin(ang), (1, LANES // half)) * sign[None, :]
    return cos, sin


def _layer(x3, mem, g_mix, g_mem, w_in, w_mem_kv, w_gate, b_gate, w_branch, w_out, g_ffn,
           w_group, b_group, w_sub, b_sub, w1, w3, w2, g_out):
    b, s, d = x3.shape
    bf = jnp.bfloat16
    cos_t, sin_t = _rope_tables(s)
    w_nat, w_dil = _pack_w_in(w_in)
    gm = g_mix[None, :]

    p4 = _in_proj(x3, gm, w_nat.astype(bf), cos_t, sin_t, P_MODES, 1)
    p3 = p4.reshape(b, s, p4.shape[3])
    srcs = [p4] + [_in_proj(x3, gm, w.astype(bf), cos_t, sin_t, D_MODES, B_PATTERNS[gi + 1][1])
                   for gi, w in enumerate(w_dil)]

    kv = _mem_kv(mem, g_mem[None, :], w_mem_kv.astype(bf))

    half = 256
    tri = (jnp.arange(half)[:, None] <= jnp.arange(half)[None, :]).astype(bf)
    ya = _dsa(p3, tri)

    obs, lss = [], []
    for gi, src in enumerate(srcs):
        tiles = (T_BQ, T_BK, T_BV) if gi == 0 else (0, 1, 2)
        o, l = _dilated(src, *tiles, gi)
        obs.append(o)
        lss.append(l)

    ym = _mem_attn(p3, kv)

    group = A_HEADS // A_KV_HEADS
    wa = jnp.swapaxes(w_branch[0].reshape(A_KV_HEADS, group, HEAD_DIM, d), 0, 1).reshape(-1, d).astype(bf)
    w_route = jnp.concatenate([w_group, jnp.moveaxis(w_sub, 0, 1).reshape(d, N_EXPERTS)], axis=1)
    w_route = jnp.pad(w_route, ((0, 0), (0, LANES - w_route.shape[1])))
    wrh = w_route.astype(bf)
    wrl = (w_route - wrh.astype(jnp.float32)).astype(bf)
    wr2 = jnp.concatenate([wrh, wrl], axis=1)
    r_bias = jnp.pad(jnp.concatenate([b_group, b_sub.reshape(-1)]), (0, LANES - MOE_GROUPS - N_EXPERTS))

    h, xn, rl = _merge(x3, gm, w_gate.astype(bf), b_gate[None, :], ya, wa, obs, lss,
                       w_branch[1].astype(bf), ym, w_branch[2].astype(bf), w_out.astype(bf),
                       g_ffn[None, :], wr2)
    t = b * s
    moe = _moe(xn.reshape(t, d), rl.reshape(t, LANES), r_bias[None, :],
               w1.astype(bf), w3.astype(bf), w2.astype(bf))
    return _final(h.reshape(t, d), moe, g_out[None, :]).reshape(b, s, d)


def kernel(x, mem, g_mix, g_mem, w_in, w_mem_kv, w_gate, b_gate, w_branch, w_out, g_ffn,
           w_group, b_group, w_sub, b_sub, w1, w3, w2, g_final):
    depth = g_mix.shape[0]
    assert depth == 1, "the final rmsnorm is applied right after the single layer"
    return _layer(x, mem, g_mix[0], g_mem[0], w_in[0], w_mem_kv[0], w_gate[0], b_gate[0], w_branch[0],
                  w_out[0], g_ffn[0], w_group[0], b_group[0], w_sub[0], b_sub[0], w1[0], w3[0], w2[0],
                  g_final)
```

```python
import functools
import math

import jax
import jax.numpy as jnp
from jax import lax
from jax.experimental import pallas as pl
from jax.experimental.pallas import tpu as pltpu

HEAD_DIM = 64
ROPE_THETA = 10000.0
RMS_EPS = 1e-6
A_HEADS = 8
A_KV_HEADS = 2
IDX_HEADS = 4
TOPK_MAX = 256
B_PATTERNS = ((128, 1), (512, 4), (2048, 16))
B_HEADS = 8
M_HEADS = 4
M_HEAD_DIM = 128
MOE_GROUPS = 4
EXPERTS_PER_GROUP = 4
N_EXPERTS = 16
IN_SPLITS = (512, 128, 128, 256, 64, 4, 1536, 1536, 1536, 512)

LANES = 128
MXU_WIDTH = 256
PROJ_TILE = 512
SLABS = PROJ_TILE // LANES
VMEM_LIMIT = 56 * 1024 * 1024

T_AQ, T_IQ, T_BQ, T_BK, T_MIX, T_BV, T_MQ = 0, 2, 3, 4, 5, 6, 7
P_MODES = (("rope",) * 4,) * 5 + (("rope", "rope", "plain", "hi_lo"),) + (("plain",) * 4,) * 2
SLAB_AK = T_MIX * SLABS + 0
SLAB_IK = T_MIX * SLABS + 1
SLAB_AV = T_MIX * SLABS + 2
SLAB_IW = T_MIX * SLABS + 3
D_MODES = (("rope",) * 4, ("rope",) * 4, ("plain",) * 4)

INT_MIN = -(2 ** 31)
NEG_BIG = -1e30

R_SUB0 = MOE_GROUPS

MOE_BLK = 256
MOE_STEP = 4


def _cparams(sem):
    return pltpu.CompilerParams(dimension_semantics=sem, vmem_limit_bytes=VMEM_LIMIT)


def _rms(xf, g):
    return xf * lax.rsqrt(jnp.mean(xf * xf, axis=-1, keepdims=True) + RMS_EPS) * g


def _rope_slab(y, cos, sin_signed, first_half):
    half = HEAD_DIM // 2
    partner = jnp.where(first_half, pltpu.roll(y, LANES - half, 1), pltpu.roll(y, half, 1))
    return y * cos + partner * sin_signed


def _in_proj_kernel(x_ref, g_ref, w_ref, cos_ref, sin_ref, o_ref, n_ref, *stage, modes, dil, rows):
    j = pl.program_id(2)
    tm = x_ref.shape[1]

    @pl.when(j == 0)
    def _():
        n_ref[...] = _rms(x_ref[0], g_ref[...]).astype(jnp.bfloat16)

    lane = lax.broadcasted_iota(jnp.int32, (rows, LANES), 1)
    first_half = (lane & (HEAD_DIM - 1)) < (HEAD_DIM // 2)

    def epilogue(slab_modes):
        for r in range(tm // rows):
            rs = pl.ds(r * rows, rows)
            acc = jnp.dot(n_ref[rs, :], w_ref[...], preferred_element_type=jnp.float32)
            cos = cos_ref[rs, :]
            sin = sin_ref[rs, :]
            for s in range(SLABS):
                y = acc[:, s * LANES:(s + 1) * LANES]
                if slab_modes[s] == "rope":
                    y = _rope_slab(y, cos, sin, first_half)
                elif slab_modes[s] == "hi_lo":
                    resid = y - y.astype(jnp.bfloat16).astype(jnp.float32)
                    y = jnp.where(lane < IDX_HEADS, y, resid)
                if dil == 1:
                    o_ref[0, 0, rs, s * LANES:(s + 1) * LANES] = y.astype(o_ref.dtype)
                else:
                    stage[0][s, rs, :] = y
        if dil > 1:
            for r in range(dil):
                for s in range(SLABS):
                    o_ref[0, r, :, s * LANES:(s + 1) * LANES] = (
                        stage[0][s, pl.ds(r, tm // dil, stride=dil), :].astype(o_ref.dtype))

    for pattern in sorted(set(modes)):
        tiles = [t for t, m in enumerate(modes) if m == pattern]
        cond = functools.reduce(jnp.logical_or, [j == t for t in tiles])
        pl.when(cond)(functools.partial(epilogue, pattern))


def _in_proj(x3, g_mix, wp, cos_t, sin_t, modes, dil):
    b, s, d = x3.shape
    tm = min(1024, s)
    ntiles = len(modes)
    scratch = [pltpu.VMEM((tm, d), jnp.bfloat16)]
    if dil > 1:
        scratch.append(pltpu.VMEM((SLABS, tm, LANES), jnp.float32))
    return pl.pallas_call(
        functools.partial(_in_proj_kernel, modes=modes, dil=dil, rows=128),
        grid=(b, s // tm, ntiles),
        in_specs=[
            pl.BlockSpec((1, tm, d), lambda bi, i, j: (bi, i, 0)),
            pl.BlockSpec((1, d), lambda bi, i, j: (0, 0)),
            pl.BlockSpec((d, PROJ_TILE), lambda bi, i, j: (0, j)),
            pl.BlockSpec((tm, LANES), lambda bi, i, j: (i, 0)),
            pl.BlockSpec((tm, LANES), lambda bi, i, j: (i, 0)),
        ],
        out_specs=pl.BlockSpec((1, dil, tm // dil, PROJ_TILE), lambda bi, i, j: (bi, 0, i, j)),
        out_shape=jax.ShapeDtypeStruct((b, dil, s // dil, ntiles * PROJ_TILE), jnp.bfloat16),
        scratch_shapes=scratch,
        compiler_params=_cparams(("parallel", "parallel", "arbitrary")),
        name=f"in_proj_d{dil}",
    )(x3, g_mix, wp, cos_t, sin_t)


def _mem_kv_kernel(m_ref, g_ref, w_ref, o_ref):
    n = _rms(m_ref[0], g_ref[...]).astype(jnp.bfloat16)
    o_ref[0] = jnp.dot(n, w_ref[...], preferred_element_type=jnp.float32).astype(o_ref.dtype)


def _mem_kv(mem, g_mem, w_kv):
    b, m, d = mem.shape
    n = w_kv.shape[1]
    return pl.pallas_call(
        _mem_kv_kernel,
        grid=(b,),
        in_specs=[
            pl.BlockSpec((1, m, d), lambda i: (i, 0, 0)),
            pl.BlockSpec((1, d), lambda i: (0, 0)),
            pl.BlockSpec((d, n), lambda i: (0, 0)),
        ],
        out_specs=pl.BlockSpec((1, m, n), lambda i: (i, 0, 0)),
        out_shape=jax.ShapeDtypeStruct((b, m, n), jnp.bfloat16),
        compiler_params=_cparams(("parallel",)),
        name="mem_kv",
    )(mem, g_mem, w_kv)


def _key_to_float(key):
    bits = jnp.where(key < 0, key ^ jnp.int32(0x7FFFFFFF), key)
    f = lax.bitcast_convert_type(bits, jnp.float32)
    return jnp.where((key < 0) & (f != f), -jnp.inf, f)


def _dsa_kernel(aq_ref, iq_ref, iw_ref, ik_ref, ak_ref, av_ref, tri_ref, o_ref,
                sc_ref, lists_ref, key_ref, thr_ref, need_ref, off_ref, m_ref, acc_ref,
                *, tq, kc, topk, depth, ins_rows, bis_rows, steps_per_chunk):
    i = pl.program_id(1)
    nt = pl.num_programs(1) - 1
    slot = i % 2
    pslot = 1 - slot
    nslab = kc // LANES
    group = A_HEADS // A_KV_HEADS
    nbits = 32
    blocks = [pl.ds(rb * bis_rows, bis_rows) for rb in range(tq // bis_rows)]

    def chunks_of(tile):
        return (tile * tq + tq + kc - 1) // kc

    @pl.when(i < nt)
    def _():
        q0 = i * tq
        iw = iw_ref[0].astype(jnp.float32)
        w_heads = [iw[:, h:h + 1] + iw[:, IDX_HEADS + h:IDX_HEADS + h + 1] for h in range(IDX_HEADS)]
        qpos = q0 + lax.broadcasted_iota(jnp.int32, (tq, kc), 0)
        kiota = lax.broadcasted_iota(jnp.int32, (tq, kc), 1)

        def score_chunk(c, carry):
            k0 = pl.multiple_of(c * kc, kc)
            ikc = ik_ref[0, pl.ds(k0, kc), :]
            s = jnp.zeros((tq, kc), jnp.float32)
            for h in range(IDX_HEADS):
                d = lax.dot_general(iq_ref[0, :, h * LANES:(h + 1) * LANES], ikc,
                                    (((1,), (1,)), ((), ())), preferred_element_type=jnp.float32)
                s = s + jnp.maximum(d, 0.0) * w_heads[h]
            s = jnp.where(kiota + k0 <= qpos, s, -jnp.inf)
            sc_ref[slot, c] = s
            for rg in range(tq // ins_rows):
                rows = slice(rg * ins_rows, (rg + 1) * ins_rows)
                tops = [lists_ref[rows, j * LANES:(j + 1) * LANES] for j in range(depth)]
                for sl in range(nslab):
                    x = s[rows, sl * LANES:(sl + 1) * LANES]
                    for j in range(depth):
                        hi = jnp.maximum(tops[j], x)
                        x = jnp.minimum(tops[j], x)
                        tops[j] = hi
                for j in range(depth):
                    lists_ref[rows, j * LANES:(j + 1) * LANES] = tops[j]
            return carry

        lists_ref[...] = jnp.full(lists_ref.shape, -jnp.inf, jnp.float32)
        lax.fori_loop(0, chunks_of(i), score_chunk, 0)
        key_ref[...] = jnp.full(key_ref.shape, INT_MIN, jnp.int32)

    def list_count(rs, cand, strict):
        acc = jnp.zeros((bis_rows, LANES), jnp.float32)
        for j in range(depth):
            v = lists_ref[rs, j * LANES:(j + 1) * LANES]
            acc = acc + jnp.where((v > cand) if strict else (v >= cand), 1.0, 0.0)
        return jnp.sum(acc, axis=-1, keepdims=True)

    def sorted_count(rs, cand):
        lv = [lists_ref[rs, j * LANES:(j + 1) * LANES] for j in range(depth)]
        a = lv[5] >= cand
        b = jnp.where(a, lv[8], lv[2]) >= cand
        c = jnp.where(a, jnp.where(b, lv[10], lv[6]), jnp.where(b, lv[3], lv[0])) >= cand
        d = jnp.where(a, jnp.where(b, jnp.where(c, lv[11], lv[9]), lv[7]),
                      jnp.where(b, lv[4], lv[1])) >= cand
        per_lane = (jnp.where(a, 6.0, 0.0) + jnp.where(b, 3.0, 0.0) + jnp.where(c, 1.0, 0.0)
                    + jnp.where(d, 1.0, 0.0) + jnp.where(a & b & c, 1.0, 0.0))
        return jnp.sum(per_lane, axis=-1, keepdims=True)

    def bit_step(it):
        for rs in blocks:
            pre = key_ref[rs, :]
            cand = pre + lax.shift_left(jnp.int32(1), nbits - 1 - it)
            cnt = sorted_count(rs, _key_to_float(cand))
            key_ref[rs, :] = jnp.where(cnt >= float(topk), cand, pre)

    half = tri_ref.shape[0]

    def attn_chunk(c):
        k0 = pl.multiple_of(c * kc, kc)
        akc = ak_ref[0, pl.ds(k0, kc), :]
        avc = av_ref[0, pl.ds(k0, kc), :]
        thr_b = thr_ref[pslot]
        need_b = need_ref[pslot]
        bias = []
        off = off_ref[...]
        for hh in range(kc // half):
            eq_parts = []
            for s in range(half // LANES):
                v = sc_ref[pslot, c, :, hh * half + s * LANES: hh * half + (s + 1) * LANES]
                eq_parts.append(jnp.where(v == thr_b, 1.0, 0.0).astype(jnp.bfloat16))
            eqf = jnp.concatenate(eq_parts, axis=1)
            pre = jnp.dot(eqf, tri_ref[...], preferred_element_type=jnp.float32)
            for s in range(half // LANES):
                v = sc_ref[pslot, c, :, hh * half + s * LANES: hh * half + (s + 1) * LANES]
                rank = pre[:, s * LANES:(s + 1) * LANES] + off
                tie_ok = (v == thr_b) & (rank <= need_b)
                sel = (v > thr_b) | tie_ok
                bias.append(jnp.where(sel, 0.0, NEG_BIG))
            off = off + jnp.sum(eqf.astype(jnp.float32), axis=-1, keepdims=True)
        off_ref[...] = off

        av_ones = jnp.concatenate([avc, jnp.ones_like(avc)], axis=1)
        for h in range(A_HEADS):
            logits = lax.dot_general(aq_ref[0, :, h * LANES:(h + 1) * LANES], akc,
                                     (((1,), (1,)), ((), ())), preferred_element_type=jnp.float32)
            lg = [logits[:, s * LANES:(s + 1) * LANES] + bias[s] for s in range(nslab)]
            m_old = m_ref[h]
            m_cur = functools.reduce(jnp.maximum, lg)
            m_new = jnp.maximum(m_old, jnp.max(m_cur, axis=-1, keepdims=True))
            alpha = jnp.exp2(m_old - m_new)
            p = jnp.concatenate([jnp.exp2((x - m_new).astype(jnp.bfloat16)) for x in lg], axis=1)
            m_ref[h] = m_new
            pv = jnp.dot(p, av_ones, preferred_element_type=jnp.float32)
            acc_ref[h] = jnp.concatenate([alpha, alpha], axis=1) * acc_ref[h] + pv

    nprev = chunks_of(i - 1)
    nride = jnp.minimum(nprev, nbits // steps_per_chunk)

    @pl.when(i >= 1)
    def _():
        off_ref[...] = jnp.zeros_like(off_ref)
        m_ref[...] = jnp.full(m_ref.shape, NEG_BIG, jnp.float32)
        acc_ref[...] = jnp.zeros_like(acc_ref)

        def chunk_with_steps(c, carry):
            attn_chunk(c)
            for j in range(steps_per_chunk):
                bit_step(c * steps_per_chunk + j)
            return carry

        def chunk_plain(c, carry):
            attn_chunk(c)
            return carry

        lax.fori_loop(0, nride, chunk_with_steps, 0)
        lax.fori_loop(nride, nprev, chunk_plain, 0)

        low = lax.broadcasted_iota(jnp.int32, (tq, LANES), 1) < HEAD_DIM
        for p in range(group):
            outs = [acc_ref[h, :, :LANES] / acc_ref[h, :, LANES:] for h in (p, group + p)]
            o_ref[0, :, p * LANES:(p + 1) * LANES] = jnp.where(low, outs[0], outs[1]).astype(o_ref.dtype)

    @pl.when(i < nt)
    def _():
        nchunks = chunks_of(i)
        done = jnp.where(i >= 1, nride * steps_per_chunk, 0)

        def rest(it, carry):
            bit_step(it)
            return carry

        lax.fori_loop(done, nbits, rest, 0)

        def finish(rs, thr, c_gt):
            thr_ref[slot, rs, :] = thr
            need_ref[slot, rs, :] = jnp.where(thr == -jnp.inf, 0.0, float(topk) - c_gt)

        def full_count(rs, cand, strict):
            def body(c, acc):
                for s in range(nslab):
                    v = sc_ref[slot, c, rs, s * LANES:(s + 1) * LANES]
                    acc = acc + jnp.where((v > cand) if strict else (v >= cand), 1.0, 0.0)
                return acc
            acc = lax.fori_loop(0, nchunks, body, jnp.zeros((bis_rows, LANES), jnp.float32))
            return jnp.sum(acc, axis=-1, keepdims=True)

        def redo_block(rs):
            def step(it, prefix):
                cand = prefix + lax.shift_left(jnp.int32(1), nbits - 1 - it)
                cnt = full_count(rs, _key_to_float(cand), False)
                return jnp.where(cnt >= float(topk), cand, prefix)
            key = lax.fori_loop(0, nbits, step, jnp.full((bis_rows, LANES), INT_MIN, jnp.int32))
            thr = _key_to_float(key)
            finish(rs, thr, full_count(rs, thr, True))

        overflows = []
        for rs in blocks:
            thr = _key_to_float(key_ref[rs, :])
            finish(rs, thr, list_count(rs, thr, True))
            last = lists_ref[rs, (depth - 1) * LANES:depth * LANES]
            overflows.append(jnp.where(last > thr, 1.0, 0.0))

        @pl.when(jnp.max(functools.reduce(jnp.maximum, overflows)) > 0.0)
        def _():
            for rs, over in zip(blocks, overflows):
                pl.when(jnp.max(over) > 0.0)(functools.partial(redo_block, rs))


def _dsa(p3, tri):
    b, s, _ = p3.shape
    tq = min(256, s)
    kc = min(512, s)
    nt = s // tq
    topk = min(TOPK_MAX, s // 4)
    depth = 12
    kern = functools.partial(_dsa_kernel, tq=tq, kc=kc, topk=topk, depth=depth,
                             ins_rows=16, bis_rows=min(64, tq), steps_per_chunk=4)
    wide = A_HEADS * LANES
    prev = lambda i: jnp.maximum(i - 1, 0)
    cur = lambda i: jnp.minimum(i, nt - 1)
    return pl.pallas_call(
        kern,
        grid=(b, nt + 1),
        in_specs=[
            pl.BlockSpec((1, tq, wide), lambda bi, i: (bi, prev(i), T_AQ)),
            pl.BlockSpec((1, tq, PROJ_TILE), lambda bi, i: (bi, cur(i), T_IQ)),
            pl.BlockSpec((1, tq, LANES), lambda bi, i: (bi, cur(i), SLAB_IW)),
            pl.BlockSpec((1, s, LANES), lambda bi, i: (bi, 0, SLAB_IK)),
            pl.BlockSpec((1, s, LANES), lambda bi, i: (bi, 0, SLAB_AK)),
            pl.BlockSpec((1, s, LANES), lambda bi, i: (bi, 0, SLAB_AV)),
            pl.BlockSpec(tri.shape, lambda bi, i: (0, 0)),
        ],
        out_specs=pl.BlockSpec((1, tq, wide // 2), lambda bi, i: (bi, prev(i), 0)),
        out_shape=jax.ShapeDtypeStruct((b, s, wide // 2), jnp.bfloat16),
        scratch_shapes=[
            pltpu.VMEM((2, s // kc, tq, kc), jnp.float32),
            pltpu.VMEM((tq, depth * LANES), jnp.float32),
            pltpu.VMEM((tq, LANES), jnp.int32),
            pltpu.VMEM((2, tq, LANES), jnp.float32),
            pltpu.VMEM((2, tq, LANES), jnp.float32),
            pltpu.VMEM((tq, LANES), jnp.float32),
            pltpu.VMEM((A_HEADS, tq, LANES), jnp.float32),
            pltpu.VMEM((A_HEADS, tq, 2 * LANES), jnp.float32),
        ],
        compiler_params=_cparams(("parallel", "arbitrary")),
        name="dsa",
    )(p3, p3, p3, p3, p3, p3, tri)


def _dil_kernel(q_ref, kp_ref, kc_ref, vp_ref, vc_ref, o_ref, lse_ref, *, tu, w):
    u = pl.program_id(2)
    qq = lax.broadcasted_iota(jnp.int32, (2 * w, 2 * w), 0) % w
    kk = lax.broadcasted_iota(jnp.int32, (2 * w, 2 * w), 1)
    band = (kk >= qq) & (kk <= qq + w)
    band_first = band & (kk >= jnp.where(u > 0, 0, w))
    lane = lax.broadcasted_iota(jnp.int32, (w, LANES), 1)
    low = lane < HEAD_DIM
    for sb in range(tu // w):
        rows = slice(sb * w, (sb + 1) * w)
        lse_tile = jnp.zeros((w, LANES), jnp.float32)
        for pr in range(B_HEADS // 2):
            sl = slice(pr * LANES, (pr + 1) * LANES)
            q = q_ref[0, 0, rows, sl]
            if sb == 0:
                k = jnp.concatenate([kp_ref[0, 0, :, sl], kc_ref[0, 0, :w, sl]], axis=0)
                v = jnp.concatenate([vp_ref[0, 0, :, sl], vc_ref[0, 0, :w, sl]], axis=0)
            else:
                k = kc_ref[0, 0, (sb - 1) * w:(sb + 1) * w, sl]
                v = vc_ref[0, 0, (sb - 1) * w:(sb + 1) * w, sl]
            zero = jnp.zeros_like(q)
            qm = jnp.concatenate([jnp.where(low, q, zero), jnp.where(low, zero, q)], axis=0)
            lg = lax.dot_general(qm, k, (((1,), (1,)), ((), ())), preferred_element_type=jnp.float32)
            lg = jnp.where(band_first if sb == 0 else band, lg, NEG_BIG)
            m = jnp.max(lg, axis=-1, keepdims=True)
            p = jnp.exp2(lg - m).astype(jnp.bfloat16)
            nd = jnp.dot(p, jnp.concatenate([v, jnp.ones_like(v)], axis=1),
                         preferred_element_type=jnp.float32)
            l = nd[:, LANES:]
            o = nd[:, :LANES] / l
            o_ref[0, 0, rows, sl] = jnp.where(low, o[:w], o[w:]).astype(o_ref.dtype)
            lse = (m + jnp.log2(l)) * math.log(2.0)
            lse_tile = jnp.where(lane == 2 * pr, lse[:w], lse_tile)
            lse_tile = jnp.where(lane == 2 * pr + 1, lse[w:], lse_tile)
        lse_ref[0, 0, rows, :] = lse_tile


def _dilated(src, tq_tile, tk_tile, tv_tile, gi):
    b, dil, su, _ = src.shape
    window, d2 = B_PATTERNS[gi]
    assert d2 == dil
    w = window // dil
    tu = min(512, su)
    ratio = tu // w
    width = B_HEADS * HEAD_DIM
    cur = lambda t: pl.BlockSpec((1, 1, tu, width), lambda bi, r, u: (bi, r, u, t))
    prev = lambda t: pl.BlockSpec((1, 1, w, width),
                                  lambda bi, r, u: (bi, r, jnp.maximum(u * ratio - 1, 0), t))
    return pl.pallas_call(
        functools.partial(_dil_kernel, tu=tu, w=w),
        grid=(b, dil, su // tu),
        in_specs=[cur(tq_tile), prev(tk_tile), cur(tk_tile), prev(tv_tile), cur(tv_tile)],
        out_specs=[pl.BlockSpec((1, 1, tu, width), lambda bi, r, u: (bi, r, u, 0)),
                   pl.BlockSpec((1, 1, tu, LANES), lambda bi, r, u: (bi, r, u, 0))],
        out_shape=[jax.ShapeDtypeStruct((b, dil, su, width), jnp.bfloat16),
                   jax.ShapeDtypeStruct((b, dil, su, LANES), jnp.float32)],
        compiler_params=_cparams(("parallel", "parallel", "arbitrary")),
        name=f"dilated{gi}",
    )(src, src, src, src, src)


def _mem_attn_kernel(q_ref, k_ref, v_ref, o_ref):
    scale = M_HEAD_DIM ** -0.5
    for h in range(M_HEADS):
        sl = slice(h * M_HEAD_DIM, (h + 1) * M_HEAD_DIM)
        lg = lax.dot_general(q_ref[0, :, sl], k_ref[0, :, sl], (((1,), (1,)), ((), ())),
                             preferred_element_type=jnp.float32) * scale
        m = jnp.max(lg, axis=-1, keepdims=True)
        p = jnp.exp(lg - m)
        l = jnp.sum(p, axis=-1, keepdims=True)
        o = jnp.dot(p.astype(jnp.bfloat16), v_ref[0, :, sl], preferred_element_type=jnp.float32)
        o_ref[0, :, sl] = (o / l).astype(o_ref.dtype)


def _mem_attn(p3, kv):
    b, s, _ = p3.shape
    m = kv.shape[1]
    width = M_HEADS * M_HEAD_DIM
    tq = min(512, s)
    return pl.pallas_call(
        _mem_attn_kernel,
        grid=(b, s // tq),
        in_specs=[
            pl.BlockSpec((1, tq, width), lambda bi, i: (bi, i, T_MQ)),
            pl.BlockSpec((1, m, width), lambda bi, i: (bi, 0, 0)),
            pl.BlockSpec((1, m, width), lambda bi, i: (bi, 0, 1)),
        ],
        out_specs=pl.BlockSpec((1, tq, width), lambda bi, i: (bi, i, 0)),
        out_shape=jax.ShapeDtypeStruct((b, s, width), jnp.bfloat16),
        compiler_params=_cparams(("parallel", "parallel")),
        name="mem_attn",
    )(p3, kv, kv)


def _split_bf16(v):
    hi = v.astype(jnp.bfloat16)
    lo = (v - hi.astype(jnp.float32)).astype(jnp.bfloat16)
    return hi, lo


def _merge_kernel(x_ref, gmix_ref, wg_ref, bg_ref, ya_ref, wa_ref, ob0_ref, ob1_ref, ob2_ref,
                  ls0_ref, ls1_ref, ls2_ref, wb_ref, ym_ref, wm_ref, wo_ref, gffn_ref,
                  wr2_ref, h_ref, xn_ref, rl_ref, ls_scr, ob_scr):
    d = x_ref.shape[2]
    tm = x_ref.shape[1]
    x = x_ref[0]
    n = _rms(x, gmix_ref[...]).astype(jnp.bfloat16)

    def gate(k):
        z = jnp.dot(n, wg_ref[:, k * d:(k + 1) * d], preferred_element_type=jnp.float32)
        return jax.nn.sigmoid(z + bg_ref[:, k * d:(k + 1) * d])

    merged = gate(0) * jnp.dot(ya_ref[0], wa_ref[...], preferred_element_type=jnp.float32)

    ob_refs = (ob0_ref, ob1_ref, ob2_ref)
    ls = []
    for gi, ls_ref in enumerate((ls0_ref, ls1_ref, ls2_ref)):
        dil = ls_ref.shape[1]
        if dil == 1:
            ls.append(ls_ref[0, 0])
        else:
            for r in range(dil):
                ls_scr[gi - 1, pl.ds(r, tm // dil, stride=dil), :] = ls_ref[0, r]
            ls.append(ls_scr[gi - 1])
    mx = jnp.maximum(jnp.maximum(ls[0], ls[1]), ls[2])
    es = [jnp.exp(v - mx) for v in ls]
    inv = 1.0 / (es[0] + es[1] + es[2])
    yb = [jnp.zeros((tm, LANES), jnp.float32) for _ in range(SLABS)]
    low = lax.broadcasted_iota(jnp.int32, (tm, LANES), 1) < HEAD_DIM
    for e, ob in zip(es, ob_refs):
        dil = ob.shape[1]
        a = e * inv
        for s in range(SLABS):
            sl = slice(s * LANES, (s + 1) * LANES)
            if dil == 1:
                o = ob[0, 0, :, sl].astype(jnp.float32)
            else:
                for r in range(dil):
                    ob_scr[s, pl.ds(r, tm // dil, stride=dil), :] = ob[0, r, :, sl].astype(jnp.float32)
                o = ob_scr[s]
            a_s = jnp.where(low, jnp.broadcast_to(a[:, 2 * s:2 * s + 1], (tm, LANES)),
                            jnp.broadcast_to(a[:, 2 * s + 1:2 * s + 2], (tm, LANES)))
            yb[s] = yb[s] + a_s * o
    yb = jnp.concatenate(yb, axis=1)
    merged = merged + gate(1) * jnp.dot(yb.astype(jnp.bfloat16), wb_ref[...],
                                        preferred_element_type=jnp.float32)
    merged = merged + gate(2) * jnp.dot(ym_ref[0], wm_ref[...], preferred_element_type=jnp.float32)

    h = x + jnp.dot(merged.astype(jnp.bfloat16), wo_ref[...], preferred_element_type=jnp.float32)
    h_ref[0] = h
    xn = _rms(h, gffn_ref[...])
    xn_ref[0] = xn.astype(jnp.bfloat16)
    hi, lo = _split_bf16(xn)
    rh = jnp.dot(hi, wr2_ref[...], preferred_element_type=jnp.float32)
    rlo = jnp.dot(lo, wr2_ref[...], preferred_element_type=jnp.float32)
    rl_ref[0] = rh[:, :LANES] + rh[:, LANES:] + rlo[:, :LANES]


def _merge(x3, g_mix, wg, bg, ya, wa, obs, lss, wb, ym, wm, wo, g_ffn, wr2):
    b, s, d = x3.shape
    tm = min(512, s)
    row = lambda w: pl.BlockSpec((1, tm, w), lambda bi, i: (bi, i, 0))
    full = lambda a: pl.BlockSpec(a.shape, lambda bi, i: (0,) * a.ndim)

    def dil_spec(a):
        dil, width = a.shape[1], a.shape[3]
        return pl.BlockSpec((1, dil, tm // dil, width), lambda bi, i: (bi, 0, i, 0))

    return pl.pallas_call(
        _merge_kernel,
        grid=(b, s // tm),
        in_specs=[row(d), full(g_mix), full(wg), full(bg), row(ya.shape[2]), full(wa),
                  dil_spec(obs[0]), dil_spec(obs[1]), dil_spec(obs[2]),
                  dil_spec(lss[0]), dil_spec(lss[1]), dil_spec(lss[2]), full(wb),
                  row(ym.shape[2]), full(wm), full(wo), full(g_ffn), full(wr2)],
        out_specs=[row(d), row(d), row(LANES)],
        out_shape=[jax.ShapeDtypeStruct((b, s, d), jnp.float32),
                   jax.ShapeDtypeStruct((b, s, d), jnp.bfloat16),
                   jax.ShapeDtypeStruct((b, s, LANES), jnp.float32)],
        scratch_shapes=[pltpu.VMEM((2, tm, LANES), jnp.float32),
                        pltpu.VMEM((SLABS, tm, LANES), jnp.float32)],
        compiler_params=_cparams(("parallel", "parallel")),
        name="merge",
    )(x3, g_mix, wg, bg, ya, wa, obs[0], obs[1], obs[2], lss[0], lss[1], lss[2], wb,
      ym, wm, wo, g_ffn, wr2)


def _first_lane_where(cond, lane):
    return jnp.min(jnp.where(cond, lane, float(LANES)), axis=-1, keepdims=True)


def _route(logits, rb):
    z = logits + rb
    lane = lax.broadcasted_iota(jnp.int32, z.shape, 1).astype(jnp.float32)
    is_g = lane < MOE_GROUPS
    zg = jnp.where(is_g, z, -jnp.inf)
    mg = jnp.max(zg, axis=-1, keepdims=True)
    eg = jnp.exp(zg - mg)
    gp = eg / jnp.sum(eg, axis=-1, keepdims=True)
    g_w = jnp.max(gp, axis=-1, keepdims=True)
    g_sel = _first_lane_where(is_g & (gp == g_w), lane)
    lo = R_SUB0 + g_sel * EXPERTS_PER_GROUP
    in_grp = (lane >= lo) & (lane < lo + EXPERTS_PER_GROUP)
    zs = jnp.where(in_grp, z, -jnp.inf)
    ms = jnp.max(zs, axis=-1, keepdims=True)
    es = jnp.exp(zs - ms)
    sp = es / jnp.sum(es, axis=-1, keepdims=True)
    p1 = jnp.max(sp, axis=-1, keepdims=True)
    i1 = _first_lane_where(in_grp & (sp == p1), lane)
    rest = in_grp & (lane != i1)
    sp2 = jnp.where(rest, sp, -1.0)
    p2 = jnp.max(sp2, axis=-1, keepdims=True)
    i2 = _first_lane_where(rest & (sp2 == p2), lane)
    tot = p1 + p2
    comb = jnp.where(lane == i1, g_w * (p1 / tot), jnp.where(lane == i2, g_w * (p2 / tot), 0.0))
    return comb, g_sel


def _moe_kernel(xn_ref, rl_ref, rb_ref, ltri_ref, w1_ref, w3_ref, w2_ref, o_ref,
                xs_ref, cs_ref, acc_ref, perm_ref, permt_ref, blk_ref):
    pr = pl.program_id(1)
    tm = xn_ref.shape[0]

    @pl.when(pr == 0)
    def _():
        comb, g_sel = _route(rl_ref[...], rb_ref[...])
        lane = lax.broadcasted_iota(jnp.int32, (tm, LANES), 1).astype(jnp.float32)
        onehot = jnp.where(lane == g_sel, 1.0, 0.0)
        seen = jnp.dot(ltri_ref[...], onehot.astype(jnp.bfloat16), preferred_element_type=jnp.float32)
        counts = seen[tm - 1:tm, :]
        lane_row = lane[0:1, :]
        off = 0.0
        off_row = jnp.zeros((1, LANES), jnp.float32)
        for g in range(MOE_GROUPS):
            n_g = jnp.sum(jnp.where(lane_row == g, counts, 0.0))
            off_row = jnp.where(lane_row == g, off, off_row)
            first = off.astype(jnp.int32) if g else jnp.int32(0)
            start = (first // 16) * 16
            span = first - start + n_g.astype(jnp.int32)
            tail = span % MOE_BLK
            blk_ref[g] = start
            blk_ref[MOE_GROUPS + g] = span // MOE_BLK + (tail > MOE_BLK // 2).astype(jnp.int32)
            blk_ref[2 * MOE_GROUPS + g] = ((tail > 0) & (tail <= MOE_BLK // 2)).astype(jnp.int32)
            off = off + n_g
        dest = jnp.sum(onehot * (off_row + seen - 1.0), axis=-1, keepdims=True)
        dest_row = jnp.transpose(jnp.broadcast_to(dest, (tm, LANES)))[0:1, :]
        row = lax.broadcasted_iota(jnp.int32, (tm, LANES), 0).astype(jnp.float32)
        for s in range(tm // LANES):
            sl = slice(s * LANES, (s + 1) * LANES)
            permt_ref[:, sl] = jnp.where(lane + float(s * LANES) == dest, 1.0, 0.0).astype(jnp.bfloat16)
            perm_ref[:, sl] = jnp.where(row == dest_row[:, sl], 1.0, 0.0).astype(jnp.bfloat16)
        perm = perm_ref[...]
        xs_ref[0:tm, :] = jnp.dot(perm, xn_ref[...], preferred_element_type=jnp.float32).astype(jnp.bfloat16)
        hi, lo = _split_bf16(comb)
        cs_ref[0:tm, :] = (jnp.dot(perm, hi, preferred_element_type=jnp.float32)
                           + jnp.dot(perm, lo, preferred_element_type=jnp.float32))
        xs_ref[tm:, :] = jnp.zeros((MOE_BLK, xs_ref.shape[1]), jnp.bfloat16)
        cs_ref[tm:, :] = jnp.zeros((MOE_BLK, LANES), jnp.float32)
        acc_ref[...] = jnp.zeros_like(acc_ref)

    g = pr // (EXPERTS_PER_GROUP // MOE_STEP)
    start = blk_ref[g]
    nfull = blk_ref[MOE_GROUPS + g]

    def run_experts(r0, nrows):
        rows = pl.ds(pl.multiple_of(r0, 16), nrows)
        xb = xs_ref[rows, :]
        cb = cs_ref[rows, :]
        lane_i = lax.broadcasted_iota(jnp.int32, (nrows, LANES), 1)
        y = jnp.zeros((nrows, o_ref.shape[1]), jnp.float32)
        for e2 in range(MOE_STEP):
            c = jnp.sum(jnp.where(lane_i == R_SUB0 + MOE_STEP * pr + e2, cb, 0.0), axis=-1, keepdims=True)
            a = jnp.dot(xb, w1_ref[e2], preferred_element_type=jnp.float32)
            u = jnp.dot(xb, w3_ref[e2], preferred_element_type=jnp.float32)
            hid = (jax.nn.silu(a) * u * c).astype(jnp.bfloat16)
            y = y + jnp.dot(hid, w2_ref[e2], preferred_element_type=jnp.float32)
        acc_ref[rows, :] += y

    def block(b, carry):
        run_experts(start + b * MOE_BLK, MOE_BLK)
        return carry

    lax.fori_loop(0, nfull, block, 0)

    @pl.when(blk_ref[2 * MOE_GROUPS + g] > 0)
    def _():
        run_experts(start + nfull * MOE_BLK, MOE_BLK // 2)

    @pl.when(pr == N_EXPERTS // MOE_STEP - 1)
    def _():
        o_ref[...] = jnp.dot(permt_ref[...], acc_ref[0:tm, :].astype(jnp.bfloat16),
                             preferred_element_type=jnp.float32).astype(o_ref.dtype)


def _moe(xn, rl, rb, w1, w3, w2):
    t, d = xn.shape
    hid = w1.shape[2]
    tm = min(1024, t)
    ltri = (jnp.arange(tm)[:, None] >= jnp.arange(tm)[None, :]).astype(jnp.bfloat16)
    return pl.pallas_call(
        _moe_kernel,
        grid=(t // tm, N_EXPERTS // MOE_STEP),
        in_specs=[
            pl.BlockSpec((tm, d), lambda i, p: (i, 0)),
            pl.BlockSpec((tm, LANES), lambda i, p: (i, 0)),
            pl.BlockSpec((1, LANES), lambda i, p: (0, 0)),
            pl.BlockSpec((tm, tm), lambda i, p: (0, 0)),
            pl.BlockSpec((MOE_STEP, d, hid), lambda i, p: (p, 0, 0)),
            pl.BlockSpec((MOE_STEP, d, hid), lambda i, p: (p, 0, 0)),
            pl.BlockSpec((MOE_STEP, hid, d), lambda i, p: (p, 0, 0)),
        ],
        out_specs=pl.BlockSpec((tm, d), lambda i, p: (i, 0)),
        out_shape=jax.ShapeDtypeStruct((t, d), jnp.bfloat16),
        scratch_shapes=[
            pltpu.VMEM((tm + MOE_BLK, d), jnp.bfloat16),
            pltpu.VMEM((tm + MOE_BLK, LANES), jnp.float32),
            pltpu.VMEM((tm + MOE_BLK, d), jnp.float32),
            pltpu.VMEM((tm, tm), jnp.bfloat16),
            pltpu.VMEM((tm, tm), jnp.bfloat16),
            pltpu.SMEM((3 * MOE_GROUPS,), jnp.int32),
        ],
        compiler_params=_cparams(("parallel", "arbitrary")),
        name="moe",
    )(xn, rl, rb, ltri, w1, w3, w2)


def _final_kernel(h_ref, m_ref, g_ref, o_ref):
    o_ref[...] = _rms(h_ref[...] + m_ref[...].astype(jnp.float32), g_ref[...])


def _final(h, moe, g_final):
    t, d = h.shape
    tm = min(512, t)
    return pl.pallas_call(
        _final_kernel,
        grid=(t // tm,),
        in_specs=[pl.BlockSpec((tm, d), lambda i: (i, 0)), pl.BlockSpec((tm, d), lambda i: (i, 0)),
                  pl.BlockSpec((1, d), lambda i: (0, 0))],
        out_specs=pl.BlockSpec((tm, d), lambda i: (i, 0)),
        out_shape=jax.ShapeDtypeStruct((t, d), jnp.float32),
        compiler_params=_cparams(("parallel",)),
        name="final_norm",
    )(h, moe, g_final)


def _pad_heads_kv(w):
    d = w.shape[0]
    group = A_HEADS // A_KV_HEADS
    w = w.reshape(d, A_KV_HEADS, group, HEAD_DIM)
    parts = [jnp.pad(w[:, j], ((0, 0), (0, 0), (j * HEAD_DIM, LANES - (j + 1) * HEAD_DIM)))
             for j in range(A_KV_HEADS)]
    return jnp.concatenate(parts, axis=1).reshape(d, A_HEADS * LANES)


def _pack_w_in(w_in):
    parts, off = [], 0
    for n in IN_SPLITS:
        parts.append(w_in[:, off:off + n])
        off += n
    aq, ak, av, iq, ik, iw, bq, bk, bv, mq = parts
    d = w_in.shape[0]
    qscale = HEAD_DIM ** -0.5
    aq_x = _pad_heads_kv(aq * (qscale * math.log2(math.e)))
    iq_x = jnp.pad((iq * qscale).reshape(d, IDX_HEADS, HEAD_DIM),
                   ((0, 0), (0, 0), (0, LANES - HEAD_DIM))).reshape(d, IDX_HEADS * LANES)
    ik_x = jnp.pad(ik, ((0, 0), (0, LANES - ik.shape[1])))
    iw2 = jnp.concatenate([iw, iw], axis=1) * (IDX_HEADS ** -0.5)
    iw_x = jnp.pad(iw2, ((0, 0), (0, LANES - iw2.shape[1])))
    width = B_HEADS * HEAD_DIM
    bqs = bq * (qscale * math.log2(math.e))
    grp = lambda a, gi: a[:, gi * width:(gi + 1) * width]
    nat = jnp.concatenate([aq_x, iq_x, grp(bqs, 0), grp(bk, 0), ak, ik_x, av, iw_x, grp(bv, 0), mq], axis=1)
    dil = [jnp.concatenate([grp(bqs, gi), grp(bk, gi), grp(bv, gi)], axis=1) for gi in (1, 2)]
    return nat, dil


def _rope_tables(seq):
    half = HEAD_DIM // 2
    inv = ROPE_THETA ** (-jnp.arange(half, dtype=jnp.float32) / half)
    ang = jnp.arange(seq, dtype=jnp.float32)[:, None] * inv[None, :]
    cos = jnp.tile(jnp.cos(ang), (1, LANES // half))
    sign = jnp.tile(jnp.concatenate([-jnp.ones((half,), jnp.float32), jnp.ones((half,), jnp.float32)]),
                    LANES // HEAD_DIM)
    sin = jnp.tile(jnp.sin(ang), (1, LANES // half)) * sign[None, :]
    return cos, sin


def _layer(x3, mem, g_mix, g_mem, w_in, w_mem_kv, w_gate, b_gate, w_branch, w_out, g_ffn,
           w_group, b_group, w_sub, b_sub, w1, w3, w2, g_out):
    b, s, d = x3.shape
    bf = jnp.bfloat16
    cos_t, sin_t = _rope_tables(s)
    w_nat, w_dil = _pack_w_in(w_in)
    gm = g_mix[None, :]

    p4 = _in_proj(x3, gm, w_nat.astype(bf), cos_t, sin_t, P_MODES, 1)
    p3 = p4.reshape(b, s, p4.shape[3])
    srcs = [p4] + [_in_proj(x3, gm, w.astype(bf), cos_t, sin_t, D_MODES, B_PATTERNS[gi + 1][1])
                   for gi, w in enumerate(w_dil)]

    kv = _mem_kv(mem, g_mem[None, :], w_mem_kv.astype(bf))

    tri = (jnp.arange(MXU_WIDTH)[:, None] <= jnp.arange(MXU_WIDTH)[None, :]).astype(bf)
    ya = _dsa(p3, tri)

    obs, lss = [], []
    for gi, src in enumerate(srcs):
        tiles = (T_BQ, T_BK, T_BV) if gi == 0 else (0, 1, 2)
        o, l = _dilated(src, *tiles, gi)
        obs.append(o)
        lss.append(l)

    ym = _mem_attn(p3, kv)

    group = A_HEADS // A_KV_HEADS
    wa = jnp.swapaxes(w_branch[0].reshape(A_KV_HEADS, group, HEAD_DIM, d), 0, 1).reshape(-1, d).astype(bf)
    w_route = jnp.concatenate([w_group, jnp.moveaxis(w_sub, 0, 1).reshape(d, N_EXPERTS)], axis=1)
    w_route = jnp.pad(w_route, ((0, 0), (0, LANES - w_route.shape[1])))
    wrh = w_route.astype(bf)
    wrl = (w_route - wrh.astype(jnp.float32)).astype(bf)
    wr2 = jnp.concatenate([wrh, wrl], axis=1)
    r_bias = jnp.pad(jnp.concatenate([b_group, b_sub.reshape(-1)]), (0, LANES - MOE_GROUPS - N_EXPERTS))

    h, xn, rl = _merge(x3, gm, w_gate.astype(bf), b_gate[None, :], ya, wa, obs, lss,
                       w_branch[1].astype(bf), ym, w_branch[2].astype(bf), w_out.astype(bf),
                       g_ffn[None, :], wr2)
    t = b * s
    moe = _moe(xn.reshape(t, d), rl.reshape(t, LANES), r_bias[None, :],
               w1.astype(bf), w3.astype(bf), w2.astype(bf))
    return _final(h.reshape(t, d), moe, g_out[None, :]).reshape(b, s, d)


def kernel(x, mem, g_mix, g_mem, w_in, w_mem_kv, w_gate, b_gate, w_branch, w_out, g_ffn,
           w_group, b_group, w_sub, b_sub, w1, w3, w2, g_final):
    depth = g_mix.shape[0]
    assert depth == 1, "the final rmsnorm is applied right after the single layer"
    return _layer(x, mem, g_mix[0], g_mem[0], w_in[0], w_mem_kv[0], w_gate[0], b_gate[0], w_branch[0],
                  w_out[0], g_ffn[0], w_group[0], b_group[0], w_sub[0], b_sub[0], w1[0], w3[0], w2[0],
                  g_final)
```

```python
import functools
import math

import jax
import jax.numpy as jnp
from jax import lax
from jax.experimental import pallas as pl
from jax.experimental.pallas import tpu as pltpu

HEAD_DIM = 64
ROPE_THETA = 10000.0
RMS_EPS = 1e-6
A_HEADS = 8
A_KV_HEADS = 2
IDX_HEADS = 4
TOPK_MAX = 256
B_PATTERNS = ((128, 1), (512, 4), (2048, 16))
B_HEADS = 8
M_HEADS = 4
M_HEAD_DIM = 128
MOE_GROUPS = 4
EXPERTS_PER_GROUP = 4
N_EXPERTS = 16
IN_SPLITS = (512, 128, 128, 256, 64, 4, 1536, 1536, 1536, 512)

LANES = 128
MXU_WIDTH = 256
PROJ_TILE = 512
SLABS = PROJ_TILE // LANES
VMEM_LIMIT = 56 * 1024 * 1024

T_AQ, T_IQ, T_BQ, T_BK, T_MIX, T_BV, T_MQ = 0, 2, 3, 4, 5, 6, 7
P_MODES = (("rope",) * 4,) * 5 + (("rope", "rope", "plain", "hi_lo"),) + (("plain",) * 4,) * 2
SLAB_AK = T_MIX * SLABS + 0
SLAB_IK = T_MIX * SLABS + 1
SLAB_AV = T_MIX * SLABS + 2
SLAB_IW = T_MIX * SLABS + 3
D_MODES = (("rope",) * 4, ("rope",) * 4, ("plain",) * 4)

INT_MIN = -(2 ** 31)
NEG_BIG = -1e30

R_SUB0 = MOE_GROUPS

MOE_BLK = 256
MOE_STEP = 4


def _cparams(sem):
    return pltpu.CompilerParams(dimension_semantics=sem, vmem_limit_bytes=VMEM_LIMIT)


def _rms(xf, g):
    return xf * lax.rsqrt(jnp.mean(xf * xf, axis=-1, keepdims=True) + RMS_EPS) * g


def _rope_slab(y, cos, sin_signed, first_half):
    half = HEAD_DIM // 2
    partner = jnp.where(first_half, pltpu.roll(y, LANES - half, 1), pltpu.roll(y, half, 1))
    return y * cos + partner * sin_signed


def _in_proj_kernel(x_ref, g_ref, w_ref, cos_ref, sin_ref, o_ref, n_ref, *stage, modes, dil, rows):
    j = pl.program_id(2)
    tm = x_ref.shape[1]

    @pl.when(j == 0)
    def _():
        n_ref[...] = _rms(x_ref[0], g_ref[...]).astype(jnp.bfloat16)

    lane = lax.broadcasted_iota(jnp.int32, (rows, LANES), 1)
    first_half = (lane & (HEAD_DIM - 1)) < (HEAD_DIM // 2)

    def epilogue(slab_modes):
        for r in range(tm // rows):
            rs = pl.ds(r * rows, rows)
            acc = jnp.dot(n_ref[rs, :], w_ref[...], preferred_element_type=jnp.float32)
            cos = cos_ref[rs, :]
            sin = sin_ref[rs, :]
            for s in range(SLABS):
                y = acc[:, s * LANES:(s + 1) * LANES]
                if slab_modes[s] == "rope":
                    y = _rope_slab(y, cos, sin, first_half)
                elif slab_modes[s] == "hi_lo":
                    resid = y - y.astype(jnp.bfloat16).astype(jnp.float32)
                    y = jnp.where(lane < IDX_HEADS, y, resid)
                if dil == 1:
                    o_ref[0, 0, rs, s * LANES:(s + 1) * LANES] = y.astype(o_ref.dtype)
                else:
                    stage[0][s, rs, :] = y
        if dil > 1:
            for r in range(dil):
                for s in range(SLABS):
                    o_ref[0, r, :, s * LANES:(s + 1) * LANES] = (
                        stage[0][s, pl.ds(r, tm // dil, stride=dil), :].astype(o_ref.dtype))

    for pattern in sorted(set(modes)):
        tiles = [t for t, m in enumerate(modes) if m == pattern]
        cond = functools.reduce(jnp.logical_or, [j == t for t in tiles])
        pl.when(cond)(functools.partial(epilogue, pattern))


def _in_proj(x3, g_mix, wp, cos_t, sin_t, modes, dil):
    b, s, d = x3.shape
    tm = min(1024, s)
    ntiles = len(modes)
    scratch = [pltpu.VMEM((tm, d), jnp.bfloat16)]
    if dil > 1:
        scratch.append(pltpu.VMEM((SLABS, tm, LANES), jnp.float32))
    return pl.pallas_call(
        functools.partial(_in_proj_kernel, modes=modes, dil=dil, rows=128),
        grid=(b, s // tm, ntiles),
        in_specs=[
            pl.BlockSpec((1, tm, d), lambda bi, i, j: (bi, i, 0)),
            pl.BlockSpec((1, d), lambda bi, i, j: (0, 0)),
            pl.BlockSpec((d, PROJ_TILE), lambda bi, i, j: (0, j)),
            pl.BlockSpec((tm, LANES), lambda bi, i, j: (i, 0)),
            pl.BlockSpec((tm, LANES), lambda bi, i, j: (i, 0)),
        ],
        out_specs=pl.BlockSpec((1, dil, tm // dil, PROJ_TILE), lambda bi, i, j: (bi, 0, i, j)),
        out_shape=jax.ShapeDtypeStruct((b, dil, s // dil, ntiles * PROJ_TILE), jnp.bfloat16),
        scratch_shapes=scratch,
        compiler_params=_cparams(("parallel", "parallel", "arbitrary")),
        name=f"in_proj_d{dil}",
    )(x3, g_mix, wp, cos_t, sin_t)


def _mem_kv_kernel(m_ref, g_ref, w_ref, o_ref):
    n = _rms(m_ref[0], g_ref[...]).astype(jnp.bfloat16)
    o_ref[0] = jnp.dot(n, w_ref[...], preferred_element_type=jnp.float32).astype(o_ref.dtype)


def _mem_kv(mem, g_mem, w_kv):
    b, m, d = mem.shape
    n = w_kv.shape[1]
    return pl.pallas_call(
        _mem_kv_kernel,
        grid=(b,),
        in_specs=[
            pl.BlockSpec((1, m, d), lambda i: (i, 0, 0)),
            pl.BlockSpec((1, d), lambda i: (0, 0)),
            pl.BlockSpec((d, n), lambda i: (0, 0)),
        ],
        out_specs=pl.BlockSpec((1, m, n), lambda i: (i, 0, 0)),
        out_shape=jax.ShapeDtypeStruct((b, m, n), jnp.bfloat16),
        compiler_params=_cparams(("parallel",)),
        name="mem_kv",
    )(mem, g_mem, w_kv)


def _key_to_float(key):
    bits = jnp.where(key < 0, key ^ jnp.int32(0x7FFFFFFF), key)
    f = lax.bitcast_convert_type(bits, jnp.float32)
    return jnp.where((key < 0) & (f != f), -jnp.inf, f)


def _cmp_swap(v, i, j):
    v[i], v[j] = jnp.maximum(v[i], v[j]), jnp.minimum(v[i], v[j])


def _merge_top12(tops, new):
    assert len(tops) == 12 and len(new) == 4
    y = list(new)
    for i, j in ((0, 1), (2, 3), (0, 2), (1, 3), (1, 2)):
        _cmp_swap(y, i, j)
    v = list(tops) + y[::-1]
    for i in range(4, 8):
        _cmp_swap(v, i, i + 8)
    for stride in (4, 2, 1):
        for i in range(8):
            if i % (2 * stride) < stride:
                _cmp_swap(v, i, i + stride)
    for i in range(8, 12):
        v[i] = jnp.maximum(v[i], v[i + 4])
    for stride in (2, 1):
        for i in range(8, 12):
            if (i - 8) % (2 * stride) < stride:
                _cmp_swap(v, i, i + stride)
    return v[:12]


def _dsa_kernel(aq_ref, iq_ref, iw_ref, ik_ref, ak_ref, av_ref, tri_ref, o_ref,
                sc_ref, lists_ref, key_ref, thr_ref, need_ref, off_ref, m_ref, acc_ref,
                *, tq, kc, topk, depth, ins_rows, bis_rows, steps_per_chunk):
    i = pl.program_id(1)
    nt = pl.num_programs(1) - 1
    slot = i % 2
    pslot = 1 - slot
    nslab = kc // LANES
    group = A_HEADS // A_KV_HEADS
    nbits = 32
    blocks = [pl.ds(rb * bis_rows, bis_rows) for rb in range(tq // bis_rows)]

    def chunks_of(tile):
        return (tile * tq + tq + kc - 1) // kc

    @pl.when(i < nt)
    def _():
        q0 = i * tq
        iw = iw_ref[0].astype(jnp.float32)
        w_heads = [iw[:, h:h + 1] + iw[:, IDX_HEADS + h:IDX_HEADS + h + 1] for h in range(IDX_HEADS)]
        qpos = q0 + lax.broadcasted_iota(jnp.int32, (tq, kc), 0)
        kiota = lax.broadcasted_iota(jnp.int32, (tq, kc), 1)

        def score_chunk(c, carry):
            k0 = pl.multiple_of(c * kc, kc)
            ikc = ik_ref[0, pl.ds(k0, kc), :]
            s = jnp.zeros((tq, kc), jnp.float32)
            for h in range(IDX_HEADS):
                d = lax.dot_general(iq_ref[0, :, h * LANES:(h + 1) * LANES], ikc,
                                    (((1,), (1,)), ((), ())), preferred_element_type=jnp.float32)
                s = s + jnp.maximum(d, 0.0) * w_heads[h]
            s = jnp.where(kiota + k0 <= qpos, s, -jnp.inf)
            sc_ref[slot, c] = s
            for rg in range(tq // ins_rows):
                rows = slice(rg * ins_rows, (rg + 1) * ins_rows)
                tops = [lists_ref[rows, j * LANES:(j + 1) * LANES] for j in range(depth)]
                new = [s[rows, sl * LANES:(sl + 1) * LANES] for sl in range(nslab)]
                tops = _merge_top12(tops, new)
                for j in range(depth):
                    lists_ref[rows, j * LANES:(j + 1) * LANES] = tops[j]
            return carry

        lists_ref[...] = jnp.full(lists_ref.shape, -jnp.inf, jnp.float32)
        lax.fori_loop(0, chunks_of(i), score_chunk, 0)
        key_ref[...] = jnp.full(key_ref.shape, INT_MIN, jnp.int32)

    def list_count(rs, cand, strict):
        acc = jnp.zeros((bis_rows, LANES), jnp.float32)
        for j in range(depth):
            v = lists_ref[rs, j * LANES:(j + 1) * LANES]
            acc = acc + jnp.where((v > cand) if strict else (v >= cand), 1.0, 0.0)
        return jnp.sum(acc, axis=-1, keepdims=True)

    def sorted_count(rs, cand):
        lv = [lists_ref[rs, j * LANES:(j + 1) * LANES] for j in range(depth)]
        a = lv[5] >= cand
        b = jnp.where(a, lv[8], lv[2]) >= cand
        c = jnp.where(a, jnp.where(b, lv[10], lv[6]), jnp.where(b, lv[3], lv[0])) >= cand
        d = jnp.where(a, jnp.where(b, jnp.where(c, lv[11], lv[9]), lv[7]),
                      jnp.where(b, lv[4], lv[1])) >= cand
        per_lane = (jnp.where(a, 6.0, 0.0) + jnp.where(b, 3.0, 0.0) + jnp.where(c, 1.0, 0.0)
                    + jnp.where(d, 1.0, 0.0) + jnp.where(a & b & c, 1.0, 0.0))
        return jnp.sum(per_lane, axis=-1, keepdims=True)

    def bit_step(it):
        for rs in blocks:
            pre = key_ref[rs, :]
            cand = pre + lax.shift_left(jnp.int32(1), nbits - 1 - it)
            cnt = sorted_count(rs, _key_to_float(cand))
            key_ref[rs, :] = jnp.where(cnt >= float(topk), cand, pre)

    half = tri_ref.shape[0]

    def attn_chunk(c):
        k0 = pl.multiple_of(c * kc, kc)
        akc = ak_ref[0, pl.ds(k0, kc), :]
        avc = av_ref[0, pl.ds(k0, kc), :]
        thr_b = thr_ref[pslot]
        need_b = need_ref[pslot]
        bias = []
        off = off_ref[...]
        for hh in range(kc // half):
            eq_parts = []
            for s in range(half // LANES):
                v = sc_ref[pslot, c, :, hh * half + s * LANES: hh * half + (s + 1) * LANES]
                eq_parts.append(jnp.where(v == thr_b, 1.0, 0.0).astype(jnp.bfloat16))
            eqf = jnp.concatenate(eq_parts, axis=1)
            pre = jnp.dot(eqf, tri_ref[...], preferred_element_type=jnp.float32)
            for s in range(half // LANES):
                v = sc_ref[pslot, c, :, hh * half + s * LANES: hh * half + (s + 1) * LANES]
                rank = pre[:, s * LANES:(s + 1) * LANES] + off
                tie_ok = (v == thr_b) & (rank <= need_b)
                sel = (v > thr_b) | tie_ok
                bias.append(jnp.where(sel, 0.0, NEG_BIG))
            off = off + jnp.sum(eqf.astype(jnp.float32), axis=-1, keepdims=True)
        off_ref[...] = off

        av_ones = jnp.concatenate([avc, jnp.ones_like(avc)], axis=1)
        for h in range(A_HEADS):
            logits = lax.dot_general(aq_ref[0, :, h * LANES:(h + 1) * LANES], akc,
                                     (((1,), (1,)), ((), ())), preferred_element_type=jnp.float32)
            lg = [logits[:, s * LANES:(s + 1) * LANES] + bias[s] for s in range(nslab)]
            m_old = m_ref[h]
            m_cur = functools.reduce(jnp.maximum, lg)
            m_new = jnp.maximum(m_old, jnp.max(m_cur, axis=-1, keepdims=True))
            alpha = jnp.exp2(m_old - m_new)
            p = jnp.concatenate([jnp.exp2((x - m_new).astype(jnp.bfloat16)) for x in lg], axis=1)
            m_ref[h] = m_new
            pv = jnp.dot(p, av_ones, preferred_element_type=jnp.float32)
            acc_ref[h] = jnp.concatenate([alpha, alpha], axis=1) * acc_ref[h] + pv

    nprev = chunks_of(i - 1)
    nride = jnp.minimum(nprev, nbits // steps_per_chunk)

    @pl.when(i >= 1)
    def _():
        off_ref[...] = jnp.zeros_like(off_ref)
        m_ref[...] = jnp.full(m_ref.shape, NEG_BIG, jnp.float32)
        acc_ref[...] = jnp.zeros_like(acc_ref)

        def chunk_with_steps(c, carry):
            attn_chunk(c)
            for j in range(steps_per_chunk):
                bit_step(c * steps_per_chunk + j)
            return carry

        def chunk_plain(c, carry):
            attn_chunk(c)
            return carry

        lax.fori_loop(0, nride, chunk_with_steps, 0)
        lax.fori_loop(nride, nprev, chunk_plain, 0)

        low = lax.broadcasted_iota(jnp.int32, (tq, LANES), 1) < HEAD_DIM
        for p in range(group):
            outs = [acc_ref[h, :, :LANES] / acc_ref[h, :, LANES:] for h in (p, group + p)]
            o_ref[0, :, p * LANES:(p + 1) * LANES] = jnp.where(low, outs[0], outs[1]).astype(o_ref.dtype)

    @pl.when(i < nt)
    def _():
        nchunks = chunks_of(i)
        done = jnp.where(i >= 1, nride * steps_per_chunk, 0)

        def rest(it, carry):
            bit_step(it)
            return carry

        lax.fori_loop(done, nbits, rest, 0)

        def finish(rs, thr, c_gt):
            thr_ref[slot, rs, :] = thr
            need_ref[slot, rs, :] = jnp.where(thr == -jnp.inf, 0.0, float(topk) - c_gt)

        def full_count(rs, cand, strict):
            def body(c, acc):
                for s in range(nslab):
                    v = sc_ref[slot, c, rs, s * LANES:(s + 1) * LANES]
                    acc = acc + jnp.where((v > cand) if strict else (v >= cand), 1.0, 0.0)
                return acc
            acc = lax.fori_loop(0, nchunks, body, jnp.zeros((bis_rows, LANES), jnp.float32))
            return jnp.sum(acc, axis=-1, keepdims=True)

        def redo_block(rs):
            def step(it, prefix):
                cand = prefix + lax.shift_left(jnp.int32(1), nbits - 1 - it)
                cnt = full_count(rs, _key_to_float(cand), False)
                return jnp.where(cnt >= float(topk), cand, prefix)
            key = lax.fori_loop(0, nbits, step, jnp.full((bis_rows, LANES), INT_MIN, jnp.int32))
            thr = _key_to_float(key)
            finish(rs, thr, full_count(rs, thr, True))

        overflows = []
        for rs in blocks:
            thr = _key_to_float(key_ref[rs, :])
            finish(rs, thr, list_count(rs, thr, True))
            last = lists_ref[rs, (depth - 1) * LANES:depth * LANES]
            overflows.append(jnp.where(last > thr, 1.0, 0.0))

        @pl.when(jnp.max(functools.reduce(jnp.maximum, overflows)) > 0.0)
        def _():
            for rs, over in zip(blocks, overflows):
                pl.when(jnp.max(over) > 0.0)(functools.partial(redo_block, rs))


def _dsa(p3, tri):
    b, s, _ = p3.shape
    tq = min(256, s)
    kc = min(512, s)
    nt = s // tq
    topk = min(TOPK_MAX, s // 4)
    depth = 12
    kern = functools.partial(_dsa_kernel, tq=tq, kc=kc, topk=topk, depth=depth,
                             ins_rows=16, bis_rows=min(64, tq), steps_per_chunk=4)
    wide = A_HEADS * LANES
    prev = lambda i: jnp.maximum(i - 1, 0)
    cur = lambda i: jnp.minimum(i, nt - 1)
    return pl.pallas_call(
        kern,
        grid=(b, nt + 1),
        in_specs=[
            pl.BlockSpec((1, tq, wide), lambda bi, i: (bi, prev(i), T_AQ)),
            pl.BlockSpec((1, tq, PROJ_TILE), lambda bi, i: (bi, cur(i), T_IQ)),
            pl.BlockSpec((1, tq, LANES), lambda bi, i: (bi, cur(i), SLAB_IW)),
            pl.BlockSpec((1, s, LANES), lambda bi, i: (bi, 0, SLAB_IK)),
            pl.BlockSpec((1, s, LANES), lambda bi, i: (bi, 0, SLAB_AK)),
            pl.BlockSpec((1, s, LANES), lambda bi, i: (bi, 0, SLAB_AV)),
            pl.BlockSpec(tri.shape, lambda bi, i: (0, 0)),
        ],
        out_specs=pl.BlockSpec((1, tq, wide // 2), lambda bi, i: (bi, prev(i), 0)),
        out_shape=jax.ShapeDtypeStruct((b, s, wide // 2), jnp.bfloat16),
        scratch_shapes=[
            pltpu.VMEM((2, s // kc, tq, kc), jnp.float32),
            pltpu.VMEM((tq, depth * LANES), jnp.float32),
            pltpu.VMEM((tq, LANES), jnp.int32),
            pltpu.VMEM((2, tq, LANES), jnp.float32),
            pltpu.VMEM((2, tq, LANES), jnp.float32),
            pltpu.VMEM((tq, LANES), jnp.float32),
            pltpu.VMEM((A_HEADS, tq, LANES), jnp.float32),
            pltpu.VMEM((A_HEADS, tq, 2 * LANES), jnp.float32),
        ],
        compiler_params=_cparams(("parallel", "arbitrary")),
        name="dsa",
    )(p3, p3, p3, p3, p3, p3, tri)


def _dil_kernel(q_ref, kp_ref, kc_ref, vp_ref, vc_ref, o_ref, lse_ref, *, tu, w):
    u = pl.program_id(2)
    qq = lax.broadcasted_iota(jnp.int32, (2 * w, 2 * w), 0) % w
    kk = lax.broadcasted_iota(jnp.int32, (2 * w, 2 * w), 1)
    band = (kk >= qq) & (kk <= qq + w)
    band_first = band & (kk >= jnp.where(u > 0, 0, w))
    lane = lax.broadcasted_iota(jnp.int32, (w, LANES), 1)
    low = lane < HEAD_DIM
    for sb in range(tu // w):
        rows = slice(sb * w, (sb + 1) * w)
        lse_tile = jnp.zeros((w, LANES), jnp.float32)
        for pr in range(B_HEADS // 2):
            sl = slice(pr * LANES, (pr + 1) * LANES)
            q = q_ref[0, 0, rows, sl]
            if sb == 0:
                k = jnp.concatenate([kp_ref[0, 0, :, sl], kc_ref[0, 0, :w, sl]], axis=0)
                v = jnp.concatenate([vp_ref[0, 0, :, sl], vc_ref[0, 0, :w, sl]], axis=0)
            else:
                k = kc_ref[0, 0, (sb - 1) * w:(sb + 1) * w, sl]
                v = vc_ref[0, 0, (sb - 1) * w:(sb + 1) * w, sl]
            zero = jnp.zeros_like(q)
            qm = jnp.concatenate([jnp.where(low, q, zero), jnp.where(low, zero, q)], axis=0)
            lg = lax.dot_general(qm, k, (((1,), (1,)), ((), ())), preferred_element_type=jnp.float32)
            lg = jnp.where(band_first if sb == 0 else band, lg, NEG_BIG)
            m = jnp.max(lg, axis=-1, keepdims=True)
            p = jnp.exp2(lg - m).astype(jnp.bfloat16)
            nd = jnp.dot(p, jnp.concatenate([v, jnp.ones_like(v)], axis=1),
                         preferred_element_type=jnp.float32)
            l = nd[:, LANES:]
            o = nd[:, :LANES] / l
            o_ref[0, 0, rows, sl] = jnp.where(low, o[:w], o[w:]).astype(o_ref.dtype)
            lse = (m + jnp.log2(l)) * math.log(2.0)
            lse_tile = jnp.where(lane == 2 * pr, lse[:w], lse_tile)
            lse_tile = jnp.where(lane == 2 * pr + 1, lse[w:], lse_tile)
        lse_ref[0, 0, rows, :] = lse_tile


def _dilated(src, tq_tile, tk_tile, tv_tile, gi):
    b, dil, su, _ = src.shape
    window, d2 = B_PATTERNS[gi]
    assert d2 == dil
    w = window // dil
    tu = min(512, su)
    ratio = tu // w
    width = B_HEADS * HEAD_DIM
    cur = lambda t: pl.BlockSpec((1, 1, tu, width), lambda bi, r, u: (bi, r, u, t))
    prev = lambda t: pl.BlockSpec((1, 1, w, width),
                                  lambda bi, r, u: (bi, r, jnp.maximum(u * ratio - 1, 0), t))
    return pl.pallas_call(
        functools.partial(_dil_kernel, tu=tu, w=w),
        grid=(b, dil, su // tu),
        in_specs=[cur(tq_tile), prev(tk_tile), cur(tk_tile), prev(tv_tile), cur(tv_tile)],
        out_specs=[pl.BlockSpec((1, 1, tu, width), lambda bi, r, u: (bi, r, u, 0)),
                   pl.BlockSpec((1, 1, tu, LANES), lambda bi, r, u: (bi, r, u, 0))],
        out_shape=[jax.ShapeDtypeStruct((b, dil, su, width), jnp.bfloat16),
                   jax.ShapeDtypeStruct((b, dil, su, LANES), jnp.float32)],
        compiler_params=_cparams(("parallel", "parallel", "arbitrary")),
        name=f"dilated{gi}",
    )(src, src, src, src, src)


def _mem_attn_kernel(q_ref, k_ref, v_ref, o_ref):
    scale = M_HEAD_DIM ** -0.5
    for h in range(M_HEADS):
        sl = slice(h * M_HEAD_DIM, (h + 1) * M_HEAD_DIM)
        lg = lax.dot_general(q_ref[0, :, sl], k_ref[0, :, sl], (((1,), (1,)), ((), ())),
                             preferred_element_type=jnp.float32) * scale
        m = jnp.max(lg, axis=-1, keepdims=True)
        p = jnp.exp(lg - m)
        l = jnp.sum(p, axis=-1, keepdims=True)
        o = jnp.dot(p.astype(jnp.bfloat16), v_ref[0, :, sl], preferred_element_type=jnp.float32)
        o_ref[0, :, sl] = (o / l).astype(o_ref.dtype)


def _mem_attn(p3, kv):
    b, s, _ = p3.shape
    m = kv.shape[1]
    width = M_HEADS * M_HEAD_DIM
    tq = min(512, s)
    return pl.pallas_call(
        _mem_attn_kernel,
        grid=(b, s // tq),
        in_specs=[
            pl.BlockSpec((1, tq, width), lambda bi, i: (bi, i, T_MQ)),
            pl.BlockSpec((1, m, width), lambda bi, i: (bi, 0, 0)),
            pl.BlockSpec((1, m, width), lambda bi, i: (bi, 0, 1)),
        ],
        out_specs=pl.BlockSpec((1, tq, width), lambda bi, i: (bi, i, 0)),
        out_shape=jax.ShapeDtypeStruct((b, s, width), jnp.bfloat16),
        compiler_params=_cparams(("parallel", "parallel")),
        name="mem_attn",
    )(p3, kv, kv)


def _split_bf16(v):
    hi = v.astype(jnp.bfloat16)
    lo = (v - hi.astype(jnp.float32)).astype(jnp.bfloat16)
    return hi, lo


def _merge_kernel(x_ref, gmix_ref, wg_ref, bg_ref, ya_ref, wa_ref, ob0_ref, ob1_ref, ob2_ref,
                  ls0_ref, ls1_ref, ls2_ref, wb_ref, ym_ref, wm_ref, wo_ref, gffn_ref,
                  wr2_ref, h_ref, xn_ref, rl_ref, ls_scr, ob_scr):
    d = x_ref.shape[2]
    tm = x_ref.shape[1]
    x = x_ref[0]
    n = _rms(x, gmix_ref[...]).astype(jnp.bfloat16)

    def gate(k):
        z = jnp.dot(n, wg_ref[:, k * d:(k + 1) * d], preferred_element_type=jnp.float32)
        return jax.nn.sigmoid(z + bg_ref[:, k * d:(k + 1) * d])

    merged = gate(0) * jnp.dot(ya_ref[0], wa_ref[...], preferred_element_type=jnp.float32)

    ob_refs = (ob0_ref, ob1_ref, ob2_ref)
    ls = []
    for gi, ls_ref in enumerate((ls0_ref, ls1_ref, ls2_ref)):
        dil = ls_ref.shape[1]
        if dil == 1:
            ls.append(ls_ref[0, 0])
        else:
            for r in range(dil):
                ls_scr[gi - 1, pl.ds(r, tm // dil, stride=dil), :] = ls_ref[0, r]
            ls.append(ls_scr[gi - 1])
    mx = jnp.maximum(jnp.maximum(ls[0], ls[1]), ls[2])
    es = [jnp.exp(v - mx) for v in ls]
    inv = 1.0 / (es[0] + es[1] + es[2])
    yb = [jnp.zeros((tm, LANES), jnp.float32) for _ in range(SLABS)]
    low = lax.broadcasted_iota(jnp.int32, (tm, LANES), 1) < HEAD_DIM
    for e, ob in zip(es, ob_refs):
        dil = ob.shape[1]
        a = e * inv
        for s in range(SLABS):
            sl = slice(s * LANES, (s + 1) * LANES)
            if dil == 1:
                o = ob[0, 0, :, sl].astype(jnp.float32)
            else:
                for r in range(dil):
                    ob_scr[s, pl.ds(r, tm // dil, stride=dil), :] = ob[0, r, :, sl].astype(jnp.float32)
                o = ob_scr[s]
            a_s = jnp.where(low, jnp.broadcast_to(a[:, 2 * s:2 * s + 1], (tm, LANES)),
                            jnp.broadcast_to(a[:, 2 * s + 1:2 * s + 2], (tm, LANES)))
            yb[s] = yb[s] + a_s * o
    yb = jnp.concatenate(yb, axis=1)
    merged = merged + gate(1) * jnp.dot(yb.astype(jnp.bfloat16), wb_ref[...],
                                        preferred_element_type=jnp.float32)
    merged = merged + gate(2) * jnp.dot(ym_ref[0], wm_ref[...], preferred_element_type=jnp.float32)

    h = x + jnp.dot(merged.astype(jnp.bfloat16), wo_ref[...], preferred_element_type=jnp.float32)
    h_ref[0] = h
    xn = _rms(h, gffn_ref[...])
    xn_ref[0] = xn.astype(jnp.bfloat16)
    hi, lo = _split_bf16(xn)
    rh = jnp.dot(hi, wr2_ref[...], preferred_element_type=jnp.float32)
    rlo = jnp.dot(lo, wr2_ref[...], preferred_element_type=jnp.float32)
    rl_ref[0] = rh[:, :LANES] + rh[:, LANES:] + rlo[:, :LANES]


def _merge(x3, g_mix, wg, bg, ya, wa, obs, lss, wb, ym, wm, wo, g_ffn, wr2):
    b, s, d = x3.shape
    tm = min(512, s)
    row = lambda w: pl.BlockSpec((1, tm, w), lambda bi, i: (bi, i, 0))
    full = lambda a: pl.BlockSpec(a.shape, lambda bi, i: (0,) * a.ndim)

    def dil_spec(a):
        dil, width = a.shape[1], a.shape[3]
        return pl.BlockSpec((1, dil, tm // dil, width), lambda bi, i: (bi, 0, i, 0))

    return pl.pallas_call(
        _merge_kernel,
        grid=(b, s // tm),
        in_specs=[row(d), full(g_mix), full(wg), full(bg), row(ya.shape[2]), full(wa),
                  dil_spec(obs[0]), dil_spec(obs[1]), dil_spec(obs[2]),
                  dil_spec(lss[0]), dil_spec(lss[1]), dil_spec(lss[2]), full(wb),
                  row(ym.shape[2]), full(wm), full(wo), full(g_ffn), full(wr2)],
        out_specs=[row(d), row(d), row(LANES)],
        out_shape=[jax.ShapeDtypeStruct((b, s, d), jnp.float32),
                   jax.ShapeDtypeStruct((b, s, d), jnp.bfloat16),
                   jax.ShapeDtypeStruct((b, s, LANES), jnp.float32)],
        scratch_shapes=[pltpu.VMEM((2, tm, LANES), jnp.float32),
                        pltpu.VMEM((SLABS, tm, LANES), jnp.float32)],
        compiler_params=_cparams(("parallel", "parallel")),
        name="merge",
    )(x3, g_mix, wg, bg, ya, wa, obs[0], obs[1], obs[2], lss[0], lss[1], lss[2], wb,
      ym, wm, wo, g_ffn, wr2)


def _first_lane_where(cond, lane):
    return jnp.min(jnp.where(cond, lane, float(LANES)), axis=-1, keepdims=True)


def _route(logits, rb):
    z = logits + rb
    lane = lax.broadcasted_iota(jnp.int32, z.shape, 1).astype(jnp.float32)
    is_g = lane < MOE_GROUPS
    zg = jnp.where(is_g, z, -jnp.inf)
    mg = jnp.max(zg, axis=-1, keepdims=True)
    eg = jnp.exp(zg - mg)
    gp = eg / jnp.sum(eg, axis=-1, keepdims=True)
    g_w = jnp.max(gp, axis=-1, keepdims=True)
    g_sel = _first_lane_where(is_g & (gp == g_w), lane)
    lo = R_SUB0 + g_sel * EXPERTS_PER_GROUP
    in_grp = (lane >= lo) & (lane < lo + EXPERTS_PER_GROUP)
    zs = jnp.where(in_grp, z, -jnp.inf)
    ms = jnp.max(zs, axis=-1, keepdims=True)
    es = jnp.exp(zs - ms)
    sp = es / jnp.sum(es, axis=-1, keepdims=True)
    p1 = jnp.max(sp, axis=-1, keepdims=True)
    i1 = _first_lane_where(in_grp & (sp == p1), lane)
    rest = in_grp & (lane != i1)
    sp2 = jnp.where(rest, sp, -1.0)
    p2 = jnp.max(sp2, axis=-1, keepdims=True)
    i2 = _first_lane_where(rest & (sp2 == p2), lane)
    tot = p1 + p2
    comb = jnp.where(lane == i1, g_w * (p1 / tot), jnp.where(lane == i2, g_w * (p2 / tot), 0.0))
    return comb, g_sel


def _moe_kernel(xn_ref, rl_ref, rb_ref, ltri_ref, w1_ref, w3_ref, w2_ref, o_ref,
                xs_ref, cs_ref, acc_ref, perm_ref, permt_ref, blk_ref):
    pr = pl.program_id(1)
    tm = xn_ref.shape[0]

    @pl.when(pr == 0)
    def _():
        comb, g_sel = _route(rl_ref[...], rb_ref[...])
        lane = lax.broadcasted_iota(jnp.int32, (tm, LANES), 1).astype(jnp.float32)
        onehot = jnp.where(lane == g_sel, 1.0, 0.0)
        seen = jnp.dot(ltri_ref[...], onehot.astype(jnp.bfloat16), preferred_element_type=jnp.float32)
        counts = seen[tm - 1:tm, :]
        lane_row = lane[0:1, :]
        off = 0.0
        off_row = jnp.zeros((1, LANES), jnp.float32)
        for g in range(MOE_GROUPS):
            n_g = jnp.sum(jnp.where(lane_row == g, counts, 0.0))
            off_row = jnp.where(lane_row == g, off, off_row)
            first = off.astype(jnp.int32) if g else jnp.int32(0)
            start = (first // 16) * 16
            span = first - start + n_g.astype(jnp.int32)
            tail = span % MOE_BLK
            blk_ref[g] = start
            blk_ref[MOE_GROUPS + g] = span // MOE_BLK + (tail > MOE_BLK // 2).astype(jnp.int32)
            blk_ref[2 * MOE_GROUPS + g] = ((tail > 0) & (tail <= MOE_BLK // 2)).astype(jnp.int32)
            off = off + n_g
        dest = jnp.sum(onehot * (off_row + seen - 1.0), axis=-1, keepdims=True)
        dest_row = jnp.transpose(jnp.broadcast_to(dest, (tm, LANES)))[0:1, :]
        row = lax.broadcasted_iota(jnp.int32, (tm, LANES), 0).astype(jnp.float32)
        for s in range(tm // LANES):
            sl = slice(s * LANES, (s + 1) * LANES)
            permt_ref[:, sl] = jnp.where(lane + float(s * LANES) == dest, 1.0, 0.0).astype(jnp.bfloat16)
            perm_ref[:, sl] = jnp.where(row == dest_row[:, sl], 1.0, 0.0).astype(jnp.bfloat16)
        perm = perm_ref[...]
        xs_ref[0:tm, :] = jnp.dot(perm, xn_ref[...], preferred_element_type=jnp.float32).astype(jnp.bfloat16)
        hi, lo = _split_bf16(comb)
        cs_ref[0:tm, :] = (jnp.dot(perm, hi, preferred_element_type=jnp.float32)
                           + jnp.dot(perm, lo, preferred_element_type=jnp.float32))
        xs_ref[tm:, :] = jnp.zeros((MOE_BLK, xs_ref.shape[1]), jnp.bfloat16)
        cs_ref[tm:, :] = jnp.zeros((MOE_BLK, LANES), jnp.float32)
        acc_ref[...] = jnp.zeros_like(acc_ref)

    g = pr // (EXPERTS_PER_GROUP // MOE_STEP)
    start = blk_ref[g]
    nfull = blk_ref[MOE_GROUPS + g]

    def run_experts(r0, nrows):
        rows = pl.ds(pl.multiple_of(r0, 16), nrows)
        xb = xs_ref[rows, :]
        cb = cs_ref[rows, :]
        lane_i = lax.broadcasted_iota(jnp.int32, (nrows, LANES), 1)
        y = jnp.zeros((nrows, o_ref.shape[1]), jnp.float32)
        for e2 in range(MOE_STEP):
            c = jnp.sum(jnp.where(lane_i == R_SUB0 + MOE_STEP * pr + e2, cb, 0.0), axis=-1, keepdims=True)
            a = jnp.dot(xb, w1_ref[e2], preferred_element_type=jnp.float32)
            u = jnp.dot(xb, w3_ref[e2], preferred_element_type=jnp.float32)
            hid = (jax.nn.silu(a) * u * c).astype(jnp.bfloat16)
            y = y + jnp.dot(hid, w2_ref[e2], preferred_element_type=jnp.float32)
        acc_ref[rows, :] += y

    def block(b, carry):
        run_experts(start + b * MOE_BLK, MOE_BLK)
        return carry

    lax.fori_loop(0, nfull, block, 0)

    @pl.when(blk_ref[2 * MOE_GROUPS + g] > 0)
    def _():
        run_experts(start + nfull * MOE_BLK, MOE_BLK // 2)

    @pl.when(pr == N_EXPERTS // MOE_STEP - 1)
    def _():
        o_ref[...] = jnp.dot(permt_ref[...], acc_ref[0:tm, :].astype(jnp.bfloat16),
                             preferred_element_type=jnp.float32).astype(o_ref.dtype)


def _moe(xn, rl, rb, w1, w3, w2):
    t, d = xn.shape
    hid = w1.shape[2]
    tm = min(1024, t)
    ltri = (jnp.arange(tm)[:, None] >= jnp.arange(tm)[None, :]).astype(jnp.bfloat16)
    return pl.pallas_call(
        _moe_kernel,
        grid=(t // tm, N_EXPERTS // MOE_STEP),
        in_specs=[
            pl.BlockSpec((tm, d), lambda i, p: (i, 0)),
            pl.BlockSpec((tm, LANES), lambda i, p: (i, 0)),
            pl.BlockSpec((1, LANES), lambda i, p: (0, 0)),
            pl.BlockSpec((tm, tm), lambda i, p: (0, 0)),
            pl.BlockSpec((MOE_STEP, d, hid), lambda i, p: (p, 0, 0)),
            pl.BlockSpec((MOE_STEP, d, hid), lambda i, p: (p, 0, 0)),
            pl.BlockSpec((MOE_STEP, hid, d), lambda i, p: (p, 0, 0)),
        ],
        out_specs=pl.BlockSpec((tm, d), lambda i, p: (i, 0)),
        out_shape=jax.ShapeDtypeStruct((t, d), jnp.bfloat16),
        scratch_shapes=[
            pltpu.VMEM((tm + MOE_BLK, d), jnp.bfloat16),
            pltpu.VMEM((tm + MOE_BLK, LANES), jnp.float32),
            pltpu.VMEM((tm + MOE_BLK, d), jnp.float32),
            pltpu.VMEM((tm, tm), jnp.bfloat16),
            pltpu.VMEM((tm, tm), jnp.bfloat16),
            pltpu.SMEM((3 * MOE_GROUPS,), jnp.int32),
        ],
        compiler_params=_cparams(("parallel", "arbitrary")),
        name="moe",
    )(xn, rl, rb, ltri, w1, w3, w2)


def _final_kernel(h_ref, m_ref, g_ref, o_ref):
    o_ref[...] = _rms(h_ref[...] + m_ref[...].astype(jnp.float32), g_ref[...])


def _final(h, moe, g_final):
    t, d = h.shape
    tm = min(512, t)
    return pl.pallas_call(
        _final_kernel,
        grid=(t // tm,),
        in_specs=[pl.BlockSpec((tm, d), lambda i: (i, 0)), pl.BlockSpec((tm, d), lambda i: (i, 0)),
                  pl.BlockSpec((1, d), lambda i: (0, 0))],
        out_specs=pl.BlockSpec((tm, d), lambda i: (i, 0)),
        out_shape=jax.ShapeDtypeStruct((t, d), jnp.float32),
        compiler_params=_cparams(("parallel",)),
        name="final_norm",
    )(h, moe, g_final)


def _pad_heads_kv(w):
    d = w.shape[0]
    group = A_HEADS // A_KV_HEADS
    w = w.reshape(d, A_KV_HEADS, group, HEAD_DIM)
    parts = [jnp.pad(w[:, j], ((0, 0), (0, 0), (j * HEAD_DIM, LANES - (j + 1) * HEAD_DIM)))
             for j in range(A_KV_HEADS)]
    return jnp.concatenate(parts, axis=1).reshape(d, A_HEADS * LANES)


def _pack_w_in(w_in):
    parts, off = [], 0
    for n in IN_SPLITS:
        parts.append(w_in[:, off:off + n])
        off += n
    aq, ak, av, iq, ik, iw, bq, bk, bv, mq = parts
    d = w_in.shape[0]
    qscale = HEAD_DIM ** -0.5
    aq_x = _pad_heads_kv(aq * (qscale * math.log2(math.e)))
    iq_x = jnp.pad((iq * qscale).reshape(d, IDX_HEADS, HEAD_DIM),
                   ((0, 0), (0, 0), (0, LANES - HEAD_DIM))).reshape(d, IDX_HEADS * LANES)
    ik_x = jnp.pad(ik, ((0, 0), (0, LANES - ik.shape[1])))
    iw2 = jnp.concatenate([iw, iw], axis=1) * (IDX_HEADS ** -0.5)
    iw_x = jnp.pad(iw2, ((0, 0), (0, LANES - iw2.shape[1])))
    width = B_HEADS * HEAD_DIM
    bqs = bq * (qscale * math.log2(math.e))
    grp = lambda a, gi: a[:, gi * width:(gi + 1) * width]
    nat = jnp.concatenate([aq_x, iq_x, grp(bqs, 0), grp(bk, 0), ak, ik_x, av, iw_x, grp(bv, 0), mq], axis=1)
    dil = [jnp.concatenate([grp(bqs, gi), grp(bk, gi), grp(bv, gi)], axis=1) for gi in (1, 2)]
    return nat, dil


def _rope_tables(seq):
    half = HEAD_DIM // 2
    inv = ROPE_THETA ** (-jnp.arange(half, dtype=jnp.float32) / half)
    ang = jnp.arange(seq, dtype=jnp.float32)[:, None] * inv[None, :]
    cos = jnp.tile(jnp.cos(ang), (1, LANES // half))
    sign = jnp.tile(jnp.concatenate([-jnp.ones((half,), jnp.float32), jnp.ones((half,), jnp.float32)]),
                    LANES // HEAD_DIM)
    sin = jnp.tile(jnp.sin(ang), (1, LANES // half)) * sign[None, :]
    return cos, sin


def _layer(x3, mem, g_mix, g_mem, w_in, w_mem_kv, w_gate, b_gate, w_branch, w_out, g_ffn,
           w_group, b_group, w_sub, b_sub, w1, w3, w2, g_out):
    b, s, d = x3.shape
    bf = jnp.bfloat16
    cos_t, sin_t = _rope_tables(s)
    w_nat, w_dil = _pack_w_in(w_in)
    gm = g_mix[None, :]

    p4 = _in_proj(x3, gm, w_nat.astype(bf), cos_t, sin_t, P_MODES, 1)
    p3 = p4.reshape(b, s, p4.shape[3])
    srcs = [p4] + [_in_proj(x3, gm, w.astype(bf), cos_t, sin_t, D_MODES, B_PATTERNS[gi + 1][1])
                   for gi, w in enumerate(w_dil)]

    kv = _mem_kv(mem, g_mem[None, :], w_mem_kv.astype(bf))

    tri = (jnp.arange(MXU_WIDTH)[:, None] <= jnp.arange(MXU_WIDTH)[None, :]).astype(bf)
    ya = _dsa(p3, tri)

    obs, lss = [], []
    for gi, src in enumerate(srcs):
        tiles = (T_BQ, T_BK, T_BV) if gi == 0 else (0, 1, 2)
        o, l = _dilated(src, *tiles, gi)
        obs.append(o)
        lss.append(l)

    ym = _mem_attn(p3, kv)

    group = A_HEADS // A_KV_HEADS
    wa = jnp.swapaxes(w_branch[0].reshape(A_KV_HEADS, group, HEAD_DIM, d), 0, 1).reshape(-1, d).astype(bf)
    w_route = jnp.concatenate([w_group, jnp.moveaxis(w_sub, 0, 1).reshape(d, N_EXPERTS)], axis=1)
    w_route = jnp.pad(w_route, ((0, 0), (0, LANES - w_route.shape[1])))
    wrh = w_route.astype(bf)
    wrl = (w_route - wrh.astype(jnp.float32)).astype(bf)
    wr2 = jnp.concatenate([wrh, wrl], axis=1)
    r_bias = jnp.pad(jnp.concatenate([b_group, b_sub.reshape(-1)]), (0, LANES - MOE_GROUPS - N_EXPERTS))

    h, xn, rl = _merge(x3, gm, w_gate.astype(bf), b_gate[None, :], ya, wa, obs, lss,
                       w_branch[1].astype(bf), ym, w_branch[2].astype(bf), w_out.astype(bf),
                       g_ffn[None, :], wr2)
    t = b * s
    moe = _moe(xn.reshape(t, d), rl.reshape(t, LANES), r_bias[None, :],
               w1.astype(bf), w3.astype(bf), w2.astype(bf))
    return _final(h.reshape(t, d), moe, g_out[None, :]).reshape(b, s, d)


def kernel(x, mem, g_mix, g_mem, w_in, w_mem_kv, w_gate, b_gate, w_branch, w_out, g_ffn,
           w_group, b_group, w_sub, b_sub, w1, w3, w2, g_final):
    depth = g_mix.shape[0]
    assert depth == 1, "the final rmsnorm is applied right after the single layer"
    return _layer(x, mem, g_mix[0], g_mem[0], w_in[0], w_mem_kv[0], w_gate[0], b_gate[0], w_branch[0],
                  w_out[0], g_ffn[0], w_group[0], b_group[0], w_sub[0], b_sub[0], w1[0], w3[0], w2[0],
                  g_final)
```

```python
import functools
import math

import jax
import jax.numpy as jnp
from jax import lax
from jax.experimental import pallas as pl
from jax.experimental.pallas import tpu as pltpu

HEAD_DIM = 64
ROPE_THETA = 10000.0
RMS_EPS = 1e-6
A_HEADS = 8
A_KV_HEADS = 2
IDX_HEADS = 4
TOPK_MAX = 256
B_PATTERNS = ((128, 1), (512, 4), (2048, 16))
B_HEADS = 8
M_HEADS = 4
M_HEAD_DIM = 128
MOE_GROUPS = 4
EXPERTS_PER_GROUP = 4
N_EXPERTS = 16
IN_SPLITS = (512, 128, 128, 256, 64, 4, 1536, 1536, 1536, 512)

LANES = 128
MXU_WIDTH = 256
PROJ_TILE = 512
SLABS = PROJ_TILE // LANES
VMEM_LIMIT = 56 * 1024 * 1024

T_AQ, T_IQ, T_BQ, T_BK, T_MIX, T_BV, T_MQ = 0, 2, 3, 4, 5, 6, 7
P_MODES = (("rope",) * 4,) * 5 + (("rope", "rope", "plain", "hi_lo"),) + (("plain",) * 4,) * 2
SLAB_AK = T_MIX * SLABS + 0
SLAB_IK = T_MIX * SLABS + 1
SLAB_AV = T_MIX * SLABS + 2
SLAB_IW = T_MIX * SLABS + 3
D_MODES = (("rope",) * 4, ("rope",) * 4, ("plain",) * 4)

INT_MIN = -(2 ** 31)
NEG_BIG = -1e30

R_SUB0 = MOE_GROUPS

MOE_BLK = 256
MOE_STEP = 4


def _cparams(sem):
    return pltpu.CompilerParams(dimension_semantics=sem, vmem_limit_bytes=VMEM_LIMIT)


def _rms(xf, g):
    return xf * lax.rsqrt(jnp.mean(xf * xf, axis=-1, keepdims=True) + RMS_EPS) * g


def _rope_slab(y, cos, sin_signed, first_half):
    half = HEAD_DIM // 2
    partner = jnp.where(first_half, pltpu.roll(y, LANES - half, 1), pltpu.roll(y, half, 1))
    return y * cos + partner * sin_signed


def _in_proj_kernel(x_ref, g_ref, w_ref, cos_ref, sin_ref, o_ref, n_ref, *stage, modes, dil, rows):
    j = pl.program_id(2)
    tm = x_ref.shape[1]

    @pl.when(j == 0)
    def _():
        n_ref[...] = _rms(x_ref[0], g_ref[...]).astype(jnp.bfloat16)

    lane = lax.broadcasted_iota(jnp.int32, (rows, LANES), 1)
    first_half = (lane & (HEAD_DIM - 1)) < (HEAD_DIM // 2)

    def epilogue(slab_modes):
        for r in range(tm // rows):
            rs = pl.ds(r * rows, rows)
            acc = jnp.dot(n_ref[rs, :], w_ref[...], preferred_element_type=jnp.float32)
            cos = cos_ref[rs, :]
            sin = sin_ref[rs, :]
            for s in range(SLABS):
                y = acc[:, s * LANES:(s + 1) * LANES]
                if slab_modes[s] == "rope":
                    y = _rope_slab(y, cos, sin, first_half)
                elif slab_modes[s] == "hi_lo":
                    resid = y - y.astype(jnp.bfloat16).astype(jnp.float32)
                    y = jnp.where(lane < IDX_HEADS, y, resid)
                if dil == 1:
                    o_ref[0, 0, rs, s * LANES:(s + 1) * LANES] = y.astype(o_ref.dtype)
                else:
                    stage[0][s, rs, :] = y
        if dil > 1:
            for r in range(dil):
                for s in range(SLABS):
                    o_ref[0, r, :, s * LANES:(s + 1) * LANES] = (
                        stage[0][s, pl.ds(r, tm // dil, stride=dil), :].astype(o_ref.dtype))

    for pattern in sorted(set(modes)):
        tiles = [t for t, m in enumerate(modes) if m == pattern]
        cond = functools.reduce(jnp.logical_or, [j == t for t in tiles])
        pl.when(cond)(functools.partial(epilogue, pattern))


def _in_proj(x3, g_mix, wp, cos_t, sin_t, modes, dil):
    b, s, d = x3.shape
    tm = min(1024, s)
    ntiles = len(modes)
    scratch = [pltpu.VMEM((tm, d), jnp.bfloat16)]
    if dil > 1:
        scratch.append(pltpu.VMEM((SLABS, tm, LANES), jnp.float32))
    return pl.pallas_call(
        functools.partial(_in_proj_kernel, modes=modes, dil=dil, rows=128),
        grid=(b, s // tm, ntiles),
        in_specs=[
            pl.BlockSpec((1, tm, d), lambda bi, i, j: (bi, i, 0)),
            pl.BlockSpec((1, d), lambda bi, i, j: (0, 0)),
            pl.BlockSpec((d, PROJ_TILE), lambda bi, i, j: (0, j)),
            pl.BlockSpec((tm, LANES), lambda bi, i, j: (i, 0)),
            pl.BlockSpec((tm, LANES), lambda bi, i, j: (i, 0)),
        ],
        out_specs=pl.BlockSpec((1, dil, tm // dil, PROJ_TILE), lambda bi, i, j: (bi, 0, i, j)),
        out_shape=jax.ShapeDtypeStruct((b, dil, s // dil, ntiles * PROJ_TILE), jnp.bfloat16),
        scratch_shapes=scratch,
        compiler_params=_cparams(("parallel", "parallel", "arbitrary")),
        name=f"in_proj_d{dil}",
    )(x3, g_mix, wp, cos_t, sin_t)


def _mem_kv_kernel(m_ref, g_ref, w_ref, o_ref):
    n = _rms(m_ref[0], g_ref[...]).astype(jnp.bfloat16)
    o_ref[0] = jnp.dot(n, w_ref[...], preferred_element_type=jnp.float32).astype(o_ref.dtype)


def _mem_kv(mem, g_mem, w_kv):
    b, m, d = mem.shape
    n = w_kv.shape[1]
    return pl.pallas_call(
        _mem_kv_kernel,
        grid=(b,),
        in_specs=[
            pl.BlockSpec((1, m, d), lambda i: (i, 0, 0)),
            pl.BlockSpec((1, d), lambda i: (0, 0)),
            pl.BlockSpec((d, n), lambda i: (0, 0)),
        ],
        out_specs=pl.BlockSpec((1, m, n), lambda i: (i, 0, 0)),
        out_shape=jax.ShapeDtypeStruct((b, m, n), jnp.bfloat16),
        compiler_params=_cparams(("parallel",)),
        name="mem_kv",
    )(mem, g_mem, w_kv)


def _key_to_float(key):
    bits = jnp.where(key < 0, key ^ jnp.int32(0x7FFFFFFF), key)
    f = lax.bitcast_convert_type(bits, jnp.float32)
    return jnp.where((key < 0) & (f != f), -jnp.inf, f)


def _cmp_swap(v, i, j):
    v[i], v[j] = jnp.maximum(v[i], v[j]), jnp.minimum(v[i], v[j])


def _merge_top12(tops, new):
    assert len(tops) == 12 and len(new) == 4
    y = list(new)
    for i, j in ((0, 1), (2, 3), (0, 2), (1, 3), (1, 2)):
        _cmp_swap(y, i, j)
    v = list(tops) + y[::-1]
    for i in range(4, 8):
        _cmp_swap(v, i, i + 8)
    for stride in (4, 2, 1):
        for i in range(8):
            if i % (2 * stride) < stride:
                _cmp_swap(v, i, i + stride)
    for i in range(8, 12):
        v[i] = jnp.maximum(v[i], v[i + 4])
    for stride in (2, 1):
        for i in range(8, 12):
            if (i - 8) % (2 * stride) < stride:
                _cmp_swap(v, i, i + stride)
    return v[:12]


def _dsa_kernel(aq_ref, iq_ref, iw_ref, ik_ref, ak_ref, av_ref, tri_ref, o_ref,
                sc_ref, lists_ref, key_ref, thr_ref, need_ref, off_ref, m_ref, acc_ref,
                *, tq, kc, topk, depth, ins_rows, bis_rows, steps_per_chunk):
    i = pl.program_id(1)
    nt = pl.num_programs(1) - 1
    slot = i % 2
    pslot = 1 - slot
    nslab = kc // LANES
    group = A_HEADS // A_KV_HEADS
    nbits = 32
    blocks = [pl.ds(rb * bis_rows, bis_rows) for rb in range(tq // bis_rows)]

    def chunks_of(tile):
        return (tile * tq + tq + kc - 1) // kc

    @pl.when(i < nt)
    def _():
        q0 = i * tq
        iw = iw_ref[0].astype(jnp.float32)
        w_heads = [iw[:, h:h + 1] + iw[:, IDX_HEADS + h:IDX_HEADS + h + 1] for h in range(IDX_HEADS)]
        qpos = q0 + lax.broadcasted_iota(jnp.int32, (tq, kc), 0)
        kiota = lax.broadcasted_iota(jnp.int32, (tq, kc), 1)

        def score_chunk(c, carry):
            k0 = pl.multiple_of(c * kc, kc)
            ikc = ik_ref[0, pl.ds(k0, kc), :]
            s = jnp.zeros((tq, kc), jnp.float32)
            for h in range(IDX_HEADS):
                d = lax.dot_general(iq_ref[0, :, h * LANES:(h + 1) * LANES], ikc,
                                    (((1,), (1,)), ((), ())), preferred_element_type=jnp.float32)
                s = s + jnp.maximum(d, 0.0) * w_heads[h]
            s = jnp.where(kiota + k0 <= qpos, s, -jnp.inf)
            sc_ref[slot, c] = s
            for rg in range(tq // ins_rows):
                rows = slice(rg * ins_rows, (rg + 1) * ins_rows)
                tops = [lists_ref[rows, j * LANES:(j + 1) * LANES] for j in range(depth)]
                new = [s[rows, sl * LANES:(sl + 1) * LANES] for sl in range(nslab)]
                tops = _merge_top12(tops, new)
                for j in range(depth):
                    lists_ref[rows, j * LANES:(j + 1) * LANES] = tops[j]
            return carry

        lists_ref[...] = jnp.full(lists_ref.shape, -jnp.inf, jnp.float32)
        lax.fori_loop(0, chunks_of(i), score_chunk, 0)
        key_ref[...] = jnp.full(key_ref.shape, INT_MIN, jnp.int32)

    def list_count(rs, cand, strict):
        acc = jnp.zeros((bis_rows, LANES), jnp.float32)
        for j in range(depth):
            v = lists_ref[rs, j * LANES:(j + 1) * LANES]
            acc = acc + jnp.where((v > cand) if strict else (v >= cand), 1.0, 0.0)
        return jnp.sum(acc, axis=-1, keepdims=True)

    def sorted_count(rs, cand):
        lv = [lists_ref[rs, j * LANES:(j + 1) * LANES] for j in range(depth)]
        a = lv[5] >= cand
        b = jnp.where(a, lv[8], lv[2]) >= cand
        c = jnp.where(a, jnp.where(b, lv[10], lv[6]), jnp.where(b, lv[3], lv[0])) >= cand
        d = jnp.where(a, jnp.where(b, jnp.where(c, lv[11], lv[9]), lv[7]),
                      jnp.where(b, lv[4], lv[1])) >= cand
        per_lane = (jnp.where(a, 6.0, 0.0) + jnp.where(b, 3.0, 0.0) + jnp.where(c, 1.0, 0.0)
                    + jnp.where(d, 1.0, 0.0) + jnp.where(a & b & c, 1.0, 0.0))
        return jnp.sum(per_lane, axis=-1, keepdims=True)

    def bit_step(it):
        for rs in blocks:
            pre = key_ref[rs, :]
            cand = pre + lax.shift_left(jnp.int32(1), nbits - 1 - it)
            cnt = sorted_count(rs, _key_to_float(cand))
            key_ref[rs, :] = jnp.where(cnt >= float(topk), cand, pre)

    half = tri_ref.shape[0]

    def attn_chunk(c):
        k0 = pl.multiple_of(c * kc, kc)
        akc = ak_ref[0, pl.ds(k0, kc), :]
        avc = av_ref[0, pl.ds(k0, kc), :]
        thr_b = thr_ref[pslot]
        need_b = need_ref[pslot]
        bias = []
        off = off_ref[...]
        for hh in range(kc // half):
            eq_parts = []
            for s in range(half // LANES):
                v = sc_ref[pslot, c, :, hh * half + s * LANES: hh * half + (s + 1) * LANES]
                eq_parts.append(jnp.where(v == thr_b, 1.0, 0.0).astype(jnp.bfloat16))
            eqf = jnp.concatenate(eq_parts, axis=1)
            pre = jnp.dot(eqf, tri_ref[...], preferred_element_type=jnp.float32)
            for s in range(half // LANES):
                v = sc_ref[pslot, c, :, hh * half + s * LANES: hh * half + (s + 1) * LANES]
                rank = pre[:, s * LANES:(s + 1) * LANES] + off
                tie_ok = (v == thr_b) & (rank <= need_b)
                sel = (v > thr_b) | tie_ok
                bias.append(jnp.where(sel, 0.0, NEG_BIG))
            off = off + jnp.sum(eqf.astype(jnp.float32), axis=-1, keepdims=True)
        off_ref[...] = off

        av_ones = jnp.concatenate([avc, jnp.ones_like(avc)], axis=1)
        for h in range(A_HEADS):
            logits = lax.dot_general(aq_ref[0, :, h * LANES:(h + 1) * LANES], akc,
                                     (((1,), (1,)), ((), ())), preferred_element_type=jnp.float32)
            lg = [logits[:, s * LANES:(s + 1) * LANES] + bias[s] for s in range(nslab)]
            m_old = m_ref[h]
            m_cur = functools.reduce(jnp.maximum, lg)
            m_new = jnp.maximum(m_old, jnp.max(m_cur, axis=-1, keepdims=True))
            alpha = jnp.exp2(m_old - m_new)
            p = jnp.concatenate([jnp.exp2((x - m_new).astype(jnp.bfloat16)) for x in lg], axis=1)
            m_ref[h] = m_new
            pv = jnp.dot(p, av_ones, preferred_element_type=jnp.float32)
            acc_ref[h] = jnp.concatenate([alpha, alpha], axis=1) * acc_ref[h] + pv

    nprev = chunks_of(i - 1)
    nride = jnp.minimum(nprev, nbits // steps_per_chunk)

    @pl.when(i >= 1)
    def _():
        off_ref[...] = jnp.zeros_like(off_ref)
        m_ref[...] = jnp.full(m_ref.shape, NEG_BIG, jnp.float32)
        acc_ref[...] = jnp.zeros_like(acc_ref)

        def chunk_with_steps(c, carry):
            attn_chunk(c)
            for j in range(steps_per_chunk):
                bit_step(c * steps_per_chunk + j)
            return carry

        def chunk_plain(c, carry):
            attn_chunk(c)
            return carry

        lax.fori_loop(0, nride, chunk_with_steps, 0)
        lax.fori_loop(nride, nprev, chunk_plain, 0)

        low = lax.broadcasted_iota(jnp.int32, (tq, LANES), 1) < HEAD_DIM
        for p in range(group):
            outs = [acc_ref[h, :, :LANES] / acc_ref[h, :, LANES:] for h in (p, group + p)]
            o_ref[0, :, p * LANES:(p + 1) * LANES] = jnp.where(low, outs[0], outs[1]).astype(o_ref.dtype)

    @pl.when(i < nt)
    def _():
        nchunks = chunks_of(i)
        done = jnp.where(i >= 1, nride * steps_per_chunk, 0)

        def rest(it, carry):
            bit_step(it)
            return carry

        lax.fori_loop(done, nbits, rest, 0)

        def finish(rs, thr, c_gt):
            thr_ref[slot, rs, :] = thr
            need_ref[slot, rs, :] = jnp.where(thr == -jnp.inf, 0.0, float(topk) - c_gt)

        def full_count(rs, cand, strict):
            def body(c, acc):
                for s in range(nslab):
                    v = sc_ref[slot, c, rs, s * LANES:(s + 1) * LANES]
                    acc = acc + jnp.where((v > cand) if strict else (v >= cand), 1.0, 0.0)
                return acc
            acc = lax.fori_loop(0, nchunks, body, jnp.zeros((bis_rows, LANES), jnp.float32))
            return jnp.sum(acc, axis=-1, keepdims=True)

        def redo_block(rs):
            def step(it, prefix):
                cand = prefix + lax.shift_left(jnp.int32(1), nbits - 1 - it)
                cnt = full_count(rs, _key_to_float(cand), False)
                return jnp.where(cnt >= float(topk), cand, prefix)
            key = lax.fori_loop(0, nbits, step, jnp.full((bis_rows, LANES), INT_MIN, jnp.int32))
            thr = _key_to_float(key)
            finish(rs, thr, full_count(rs, thr, True))

        overflows = []
        for rs in blocks:
            thr = _key_to_float(key_ref[rs, :])
            finish(rs, thr, list_count(rs, thr, True))
            last = lists_ref[rs, (depth - 1) * LANES:depth * LANES]
            overflows.append(jnp.where(last > thr, 1.0, 0.0))

        @pl.when(jnp.max(functools.reduce(jnp.maximum, overflows)) > 0.0)
        def _():
            for rs, over in zip(blocks, overflows):
                pl.when(jnp.max(over) > 0.0)(functools.partial(redo_block, rs))


def _dsa(p3, tri):
    b, s, _ = p3.shape
    tq = min(256, s)
    kc = min(512, s)
    nt = s // tq
    topk = min(TOPK_MAX, s // 4)
    depth = 12
    kern = functools.partial(_dsa_kernel, tq=tq, kc=kc, topk=topk, depth=depth,
                             ins_rows=16, bis_rows=min(64, tq), steps_per_chunk=4)
    wide = A_HEADS * LANES
    prev = lambda i: jnp.maximum(i - 1, 0)
    cur = lambda i: jnp.minimum(i, nt - 1)
    return pl.pallas_call(
        kern,
        grid=(b, nt + 1),
        in_specs=[
            pl.BlockSpec((1, tq, wide), lambda bi, i: (bi, prev(i), T_AQ)),
            pl.BlockSpec((1, tq, PROJ_TILE), lambda bi, i: (bi, cur(i), T_IQ)),
            pl.BlockSpec((1, tq, LANES), lambda bi, i: (bi, cur(i), SLAB_IW)),
            pl.BlockSpec((1, s, LANES), lambda bi, i: (bi, 0, SLAB_IK)),
            pl.BlockSpec((1, s, LANES), lambda bi, i: (bi, 0, SLAB_AK)),
            pl.BlockSpec((1, s, LANES), lambda bi, i: (bi, 0, SLAB_AV)),
            pl.BlockSpec(tri.shape, lambda bi, i: (0, 0)),
        ],
        out_specs=pl.BlockSpec((1, tq, wide // 2), lambda bi, i: (bi, prev(i), 0)),
        out_shape=jax.ShapeDtypeStruct((b, s, wide // 2), jnp.bfloat16),
        scratch_shapes=[
            pltpu.VMEM((2, s // kc, tq, kc), jnp.float32),
            pltpu.VMEM((tq, depth * LANES), jnp.float32),
            pltpu.VMEM((tq, LANES), jnp.int32),
            pltpu.VMEM((2, tq, LANES), jnp.float32),
            pltpu.VMEM((2, tq, LANES), jnp.float32),
            pltpu.VMEM((tq, LANES), jnp.float32),
            pltpu.VMEM((A_HEADS, tq, LANES), jnp.float32),
            pltpu.VMEM((A_HEADS, tq, 2 * LANES), jnp.float32),
        ],
        compiler_params=_cparams(("parallel", "arbitrary")),
        name="dsa",
    )(p3, p3, p3, p3, p3, p3, tri)


def _dil_kernel(q_ref, kp_ref, kc_ref, vp_ref, vc_ref, o_ref, lse_ref, *, tu, w):
    u = pl.program_id(2)
    qq = lax.broadcasted_iota(jnp.int32, (2 * w, 2 * w), 0) % w
    kk = lax.broadcasted_iota(jnp.int32, (2 * w, 2 * w), 1)
    band = (kk >= qq) & (kk <= qq + w)
    band_first = band & (kk >= jnp.where(u > 0, 0, w))
    lane = lax.broadcasted_iota(jnp.int32, (w, LANES), 1)
    low = lane < HEAD_DIM
    for sb in range(tu // w):
        rows = slice(sb * w, (sb + 1) * w)
        lse_tile = jnp.zeros((w, LANES), jnp.float32)
        for pr in range(B_HEADS // 2):
            sl = slice(pr * LANES, (pr + 1) * LANES)
            q = q_ref[0, 0, rows, sl]
            if sb == 0:
                k = jnp.concatenate([kp_ref[0, 0, :, sl], kc_ref[0, 0, :w, sl]], axis=0)
                v = jnp.concatenate([vp_ref[0, 0, :, sl], vc_ref[0, 0, :w, sl]], axis=0)
            else:
                k = kc_ref[0, 0, (sb - 1) * w:(sb + 1) * w, sl]
                v = vc_ref[0, 0, (sb - 1) * w:(sb + 1) * w, sl]
            zero = jnp.zeros_like(q)
            qm = jnp.concatenate([jnp.where(low, q, zero), jnp.where(low, zero, q)], axis=0)
            lg = lax.dot_general(qm, k, (((1,), (1,)), ((), ())), preferred_element_type=jnp.float32)
            lg = jnp.where(band_first if sb == 0 else band, lg, NEG_BIG)
            m = jnp.max(lg, axis=-1, keepdims=True)
            p = jnp.exp2(lg - m).astype(jnp.bfloat16)
            nd = jnp.dot(p, jnp.concatenate([v, jnp.ones_like(v)], axis=1),
                         preferred_element_type=jnp.float32)
            l = nd[:, LANES:]
            o = nd[:, :LANES] / l
            o_ref[0, 0, rows, sl] = jnp.where(low, o[:w], o[w:]).astype(o_ref.dtype)
            lse = (m + jnp.log2(l)) * math.log(2.0)
            lse_tile = jnp.where(lane == 2 * pr, lse[:w], lse_tile)
            lse_tile = jnp.where(lane == 2 * pr + 1, lse[w:], lse_tile)
        lse_ref[0, 0, rows, :] = lse_tile


def _dilated(src, tq_tile, tk_tile, tv_tile, gi):
    b, dil, su, _ = src.shape
    window, d2 = B_PATTERNS[gi]
    assert d2 == dil
    w = window // dil
    tu = min(512, su)
    ratio = tu // w
    width = B_HEADS * HEAD_DIM
    cur = lambda t: pl.BlockSpec((1, 1, tu, width), lambda bi, r, u: (bi, r, u, t))
    prev = lambda t: pl.BlockSpec((1, 1, w, width),
                                  lambda bi, r, u: (bi, r, jnp.maximum(u * ratio - 1, 0), t))
    return pl.pallas_call(
        functools.partial(_dil_kernel, tu=tu, w=w),
        grid=(b, dil, su // tu),
        in_specs=[cur(tq_tile), prev(tk_tile), cur(tk_tile), prev(tv_tile), cur(tv_tile)],
        out_specs=[pl.BlockSpec((1, 1, tu, width), lambda bi, r, u: (bi, r, u, 0)),
                   pl.BlockSpec((1, 1, tu, LANES), lambda bi, r, u: (bi, r, u, 0))],
        out_shape=[jax.ShapeDtypeStruct((b, dil, su, width), jnp.bfloat16),
                   jax.ShapeDtypeStruct((b, dil, su, LANES), jnp.float32)],
        compiler_params=_cparams(("parallel", "parallel", "arbitrary")),
        name=f"dilated{gi}",
    )(src, src, src, src, src)


def _mem_attn_kernel(q_ref, k_ref, v_ref, o_ref):
    scale = M_HEAD_DIM ** -0.5
    for h in range(M_HEADS):
        sl = slice(h * M_HEAD_DIM, (h + 1) * M_HEAD_DIM)
        lg = lax.dot_general(q_ref[0, :, sl], k_ref[0, :, sl], (((1,), (1,)), ((), ())),
                             preferred_element_type=jnp.float32) * scale
        m = jnp.max(lg, axis=-1, keepdims=True)
        p = jnp.exp(lg - m)
        l = jnp.sum(p, axis=-1, keepdims=True)
        o = jnp.dot(p.astype(jnp.bfloat16), v_ref[0, :, sl], preferred_element_type=jnp.float32)
        o_ref[0, :, sl] = (o / l).astype(o_ref.dtype)


def _mem_attn(p3, kv):
    b, s, _ = p3.shape
    m = kv.shape[1]
    width = M_HEADS * M_HEAD_DIM
    tq = min(512, s)
    return pl.pallas_call(
        _mem_attn_kernel,
        grid=(b, s // tq),
        in_specs=[
            pl.BlockSpec((1, tq, width), lambda bi, i: (bi, i, T_MQ)),
            pl.BlockSpec((1, m, width), lambda bi, i: (bi, 0, 0)),
            pl.BlockSpec((1, m, width), lambda bi, i: (bi, 0, 1)),
        ],
        out_specs=pl.BlockSpec((1, tq, width), lambda bi, i: (bi, i, 0)),
        out_shape=jax.ShapeDtypeStruct((b, s, width), jnp.bfloat16),
        compiler_params=_cparams(("parallel", "parallel")),
        name="mem_attn",
    )(p3, kv, kv)


def _split_bf16(v):
    hi = v.astype(jnp.bfloat16)
    lo = (v - hi.astype(jnp.float32)).astype(jnp.bfloat16)
    return hi, lo


def _merge_kernel(x_ref, gmix_ref, wg_ref, bg_ref, ya_ref, wa_ref, ob0_ref, ob1_ref, ob2_ref,
                  ls0_ref, ls1_ref, ls2_ref, wb_ref, ym_ref, wm_ref, wo_ref, gffn_ref,
                  wr2_ref, h_ref, xn_ref, rl_ref, ls_scr, ob_scr):
    d = x_ref.shape[2]
    tm = x_ref.shape[1]
    x = x_ref[0]
    n = _rms(x, gmix_ref[...]).astype(jnp.bfloat16)

    def gate(k):
        z = jnp.dot(n, wg_ref[:, k * d:(k + 1) * d], preferred_element_type=jnp.float32)
        return jax.nn.sigmoid(z + bg_ref[:, k * d:(k + 1) * d])

    merged = gate(0) * jnp.dot(ya_ref[0], wa_ref[...], preferred_element_type=jnp.float32)

    ob_refs = (ob0_ref, ob1_ref, ob2_ref)
    ls = []
    for gi, ls_ref in enumerate((ls0_ref, ls1_ref, ls2_ref)):
        dil = ls_ref.shape[1]
        if dil == 1:
            ls.append(ls_ref[0, 0])
        else:
            for r in range(dil):
                ls_scr[gi - 1, pl.ds(r, tm // dil, stride=dil), :] = ls_ref[0, r]
            ls.append(ls_scr[gi - 1])
    mx = jnp.maximum(jnp.maximum(ls[0], ls[1]), ls[2])
    es = [jnp.exp(v - mx) for v in ls]
    inv = 1.0 / (es[0] + es[1] + es[2])
    yb = [jnp.zeros((tm, LANES), jnp.float32) for _ in range(SLABS)]
    low = lax.broadcasted_iota(jnp.int32, (tm, LANES), 1) < HEAD_DIM
    for e, ob in zip(es, ob_refs):
        dil = ob.shape[1]
        a = e * inv
        for s in range(SLABS):
            sl = slice(s * LANES, (s + 1) * LANES)
            if dil == 1:
                o = ob[0, 0, :, sl].astype(jnp.float32)
            else:
                for r in range(dil):
                    ob_scr[s, pl.ds(r, tm // dil, stride=dil), :] = ob[0, r, :, sl].astype(jnp.float32)
                o = ob_scr[s]
            a_s = jnp.where(low, jnp.broadcast_to(a[:, 2 * s:2 * s + 1], (tm, LANES)),
                            jnp.broadcast_to(a[:, 2 * s + 1:2 * s + 2], (tm, LANES)))
            yb[s] = yb[s] + a_s * o
    yb = jnp.concatenate(yb, axis=1)
    merged = merged + gate(1) * jnp.dot(yb.astype(jnp.bfloat16), wb_ref[...],
                                        preferred_element_type=jnp.float32)
    merged = merged + gate(2) * jnp.dot(ym_ref[0], wm_ref[...], preferred_element_type=jnp.float32)

    h = x + jnp.dot(merged.astype(jnp.bfloat16), wo_ref[...], preferred_element_type=jnp.float32)
    h_ref[0] = h
    xn = _rms(h, gffn_ref[...])
    xn_ref[0] = xn.astype(jnp.bfloat16)
    hi, lo = _split_bf16(xn)
    rh = jnp.dot(hi, wr2_ref[...], preferred_element_type=jnp.float32)
    rlo = jnp.dot(lo, wr2_ref[...], preferred_element_type=jnp.float32)
    rl_ref[0] = rh[:, :LANES] + rh[:, LANES:] + rlo[:, :LANES]


def _merge(x3, g_mix, wg, bg, ya, wa, obs, lss, wb, ym, wm, wo, g_ffn, wr2):
    b, s, d = x3.shape
    tm = min(512, s)
    row = lambda w: pl.BlockSpec((1, tm, w), lambda bi, i: (bi, i, 0))
    full = lambda a: pl.BlockSpec(a.shape, lambda bi, i: (0,) * a.ndim)

    def dil_spec(a):
        dil, width = a.shape[1], a.shape[3]
        return pl.BlockSpec((1, dil, tm // dil, width), lambda bi, i: (bi, 0, i, 0))

    return pl.pallas_call(
        _merge_kernel,
        grid=(b, s // tm),
        in_specs=[row(d), full(g_mix), full(wg), full(bg), row(ya.shape[2]), full(wa),
                  dil_spec(obs[0]), dil_spec(obs[1]), dil_spec(obs[2]),
                  dil_spec(lss[0]), dil_spec(lss[1]), dil_spec(lss[2]), full(wb),
                  row(ym.shape[2]), full(wm), full(wo), full(g_ffn), full(wr2)],
        out_specs=[row(d), row(d), row(LANES)],
        out_shape=[jax.ShapeDtypeStruct((b, s, d), jnp.float32),
                   jax.ShapeDtypeStruct((b, s, d), jnp.bfloat16),
                   jax.ShapeDtypeStruct((b, s, LANES), jnp.float32)],
        scratch_shapes=[pltpu.VMEM((2, tm, LANES), jnp.float32),
                        pltpu.VMEM((SLABS, tm, LANES), jnp.float32)],
        compiler_params=_cparams(("parallel", "parallel")),
        name="merge",
    )(x3, g_mix, wg, bg, ya, wa, obs[0], obs[1], obs[2], lss[0], lss[1], lss[2], wb,
      ym, wm, wo, g_ffn, wr2)


def _first_lane_where(cond, lane):
    return jnp.min(jnp.where(cond, lane, float(LANES)), axis=-1, keepdims=True)


def _route(logits, rb):
    z = logits + rb
    lane = lax.broadcasted_iota(jnp.int32, z.shape, 1).astype(jnp.float32)
    is_g = lane < MOE_GROUPS
    zg = jnp.where(is_g, z, -jnp.inf)
    mg = jnp.max(zg, axis=-1, keepdims=True)
    eg = jnp.exp(zg - mg)
    gp = eg / jnp.sum(eg, axis=-1, keepdims=True)
    g_w = jnp.max(gp, axis=-1, keepdims=True)
    g_sel = _first_lane_where(is_g & (gp == g_w), lane)
    lo = R_SUB0 + g_sel * EXPERTS_PER_GROUP
    in_grp = (lane >= lo) & (lane < lo + EXPERTS_PER_GROUP)
    zs = jnp.where(in_grp, z, -jnp.inf)
    ms = jnp.max(zs, axis=-1, keepdims=True)
    es = jnp.exp(zs - ms)
    sp = es / jnp.sum(es, axis=-1, keepdims=True)
    p1 = jnp.max(sp, axis=-1, keepdims=True)
    i1 = _first_lane_where(in_grp & (sp == p1), lane)
    rest = in_grp & (lane != i1)
    sp2 = jnp.where(rest, sp, -1.0)
    p2 = jnp.max(sp2, axis=-1, keepdims=True)
    i2 = _first_lane_where(rest & (sp2 == p2), lane)
    tot = p1 + p2
    comb = jnp.where(lane == i1, g_w * (p1 / tot), jnp.where(lane == i2, g_w * (p2 / tot), 0.0))
    return comb, g_sel


def _moe_kernel(xn_ref, rl_ref, rb_ref, ltri_ref, w1_ref, w3_ref, w2_ref, h_ref, gf_ref, o_ref,
                xs_ref, cs_ref, acc_ref, perm_ref, permt_ref, blk_ref, moe_ref):
    i = pl.program_id(0)
    nt = pl.num_programs(0) - 1
    pr = pl.program_id(1)
    tm, d = xn_ref.shape
    qrows = o_ref.shape[0]

    @pl.when(i >= 1)
    def _():
        prev = moe_ref[pl.ds(pl.multiple_of(pr * qrows, qrows), qrows), :].astype(jnp.float32)
        o_ref[...] = _rms(h_ref[...] + prev, gf_ref[...])

    @pl.when(i < nt)
    def _():
        _moe_tile(xn_ref, rl_ref, rb_ref, ltri_ref, w1_ref, w3_ref, w2_ref, moe_ref,
                  xs_ref, cs_ref, acc_ref, perm_ref, permt_ref, blk_ref, pr)


def _moe_tile(xn_ref, rl_ref, rb_ref, ltri_ref, w1_ref, w3_ref, w2_ref, moe_ref,
              xs_ref, cs_ref, acc_ref, perm_ref, permt_ref, blk_ref, pr):
    tm, d = xn_ref.shape

    @pl.when(pr == 0)
    def _():
        comb, g_sel = _route(rl_ref[...], rb_ref[...])
        lane = lax.broadcasted_iota(jnp.int32, (tm, LANES), 1).astype(jnp.float32)
        onehot = jnp.where(lane == g_sel, 1.0, 0.0)
        seen = jnp.dot(ltri_ref[...], onehot.astype(jnp.bfloat16), preferred_element_type=jnp.float32)
        counts = seen[tm - 1:tm, :]
        lane_row = lane[0:1, :]
        off = 0.0
        off_row = jnp.zeros((1, LANES), jnp.float32)
        for g in range(MOE_GROUPS):
            n_g = jnp.sum(jnp.where(lane_row == g, counts, 0.0))
            off_row = jnp.where(lane_row == g, off, off_row)
            first = off.astype(jnp.int32) if g else jnp.int32(0)
            start = (first // 16) * 16
            span = first - start + n_g.astype(jnp.int32)
            tail = span % MOE_BLK
            blk_ref[g] = start
            blk_ref[MOE_GROUPS + g] = span // MOE_BLK + (tail > MOE_BLK // 2).astype(jnp.int32)
            blk_ref[2 * MOE_GROUPS + g] = ((tail > 0) & (tail <= MOE_BLK // 2)).astype(jnp.int32)
            off = off + n_g
        dest = jnp.sum(onehot * (off_row + seen - 1.0), axis=-1, keepdims=True)
        dest_row = jnp.transpose(jnp.broadcast_to(dest, (tm, LANES)))[0:1, :]
        row = lax.broadcasted_iota(jnp.int32, (tm, LANES), 0).astype(jnp.float32)
        for s in range(tm // LANES):
            sl = slice(s * LANES, (s + 1) * LANES)
            permt_ref[:, sl] = jnp.where(lane + float(s * LANES) == dest, 1.0, 0.0).astype(jnp.bfloat16)
            perm_ref[:, sl] = jnp.where(row == dest_row[:, sl], 1.0, 0.0).astype(jnp.bfloat16)
        perm = perm_ref[...]
        xs_ref[0:tm, :] = jnp.dot(perm, xn_ref[...], preferred_element_type=jnp.float32).astype(jnp.bfloat16)
        hi, lo = _split_bf16(comb)
        cs_ref[0:tm, :] = (jnp.dot(perm, hi, preferred_element_type=jnp.float32)
                           + jnp.dot(perm, lo, preferred_element_type=jnp.float32))
        xs_ref[tm:, :] = jnp.zeros((MOE_BLK, xs_ref.shape[1]), jnp.bfloat16)
        cs_ref[tm:, :] = jnp.zeros((MOE_BLK, LANES), jnp.float32)
        acc_ref[...] = jnp.zeros_like(acc_ref)

    g = pr // (EXPERTS_PER_GROUP // MOE_STEP)
    start = blk_ref[g]
    nfull = blk_ref[MOE_GROUPS + g]

    def run_experts(r0, nrows):
        rows = pl.ds(pl.multiple_of(r0, 16), nrows)
        xb = xs_ref[rows, :]
        cb = cs_ref[rows, :]
        lane_i = lax.broadcasted_iota(jnp.int32, (nrows, LANES), 1)
        y = jnp.zeros((nrows, d), jnp.float32)
        for e2 in range(MOE_STEP):
            c = jnp.sum(jnp.where(lane_i == R_SUB0 + MOE_STEP * pr + e2, cb, 0.0), axis=-1, keepdims=True)
            a = jnp.dot(xb, w1_ref[e2], preferred_element_type=jnp.float32)
            u = jnp.dot(xb, w3_ref[e2], preferred_element_type=jnp.float32)
            hid = (jax.nn.silu(a) * u * c).astype(jnp.bfloat16)
            y = y + jnp.dot(hid, w2_ref[e2], preferred_element_type=jnp.float32)
        acc_ref[rows, :] += y

    def block(b, carry):
        run_experts(start + b * MOE_BLK, MOE_BLK)
        return carry

    lax.fori_loop(0, nfull, block, 0)

    @pl.when(blk_ref[2 * MOE_GROUPS + g] > 0)
    def _():
        run_experts(start + nfull * MOE_BLK, MOE_BLK // 2)

    @pl.when(pr == N_EXPERTS // MOE_STEP - 1)
    def _():
        moe_ref[...] = jnp.dot(permt_ref[...], acc_ref[0:tm, :].astype(jnp.bfloat16),
                               preferred_element_type=jnp.float32).astype(moe_ref.dtype)


def _moe(xn, rl, rb, h, w1, w3, w2, g_final):
    t, d = xn.shape
    hid = w1.shape[2]
    tm = min(1024, t)
    nt = t // tm
    nq = N_EXPERTS // MOE_STEP
    qrows = tm // nq
    ltri = (jnp.arange(tm)[:, None] >= jnp.arange(tm)[None, :]).astype(jnp.bfloat16)
    tile = lambda i, p: (jnp.minimum(i, nt - 1), 0)
    wblk = lambda i, p: (jnp.where(i < nt, p, nq - 1), 0, 0)
    lagged = lambda i, p: (jnp.where(i == 0, 0, (i - 1) * nq + p), 0)
    return pl.pallas_call(
        _moe_kernel,
        grid=(nt + 1, nq),
        in_specs=[
            pl.BlockSpec((tm, d), tile),
            pl.BlockSpec((tm, LANES), tile),
            pl.BlockSpec((1, LANES), lambda i, p: (0, 0)),
            pl.BlockSpec((tm, tm), lambda i, p: (0, 0)),
            pl.BlockSpec((MOE_STEP, d, hid), wblk),
            pl.BlockSpec((MOE_STEP, d, hid), wblk),
            pl.BlockSpec((MOE_STEP, hid, d), wblk),
            pl.BlockSpec((qrows, d), lagged),
            pl.BlockSpec((1, d), lambda i, p: (0, 0)),
        ],
        out_specs=pl.BlockSpec((qrows, d), lagged),
        out_shape=jax.ShapeDtypeStruct((t, d), jnp.float32),
        scratch_shapes=[
            pltpu.VMEM((tm + MOE_BLK, d), jnp.bfloat16),
            pltpu.VMEM((tm + MOE_BLK, LANES), jnp.float32),
            pltpu.VMEM((tm + MOE_BLK, d), jnp.float32),
            pltpu.VMEM((tm, tm), jnp.bfloat16),
            pltpu.VMEM((tm, tm), jnp.bfloat16),
            pltpu.SMEM((3 * MOE_GROUPS,), jnp.int32),
            pltpu.VMEM((tm, d), jnp.bfloat16),
        ],
        compiler_params=_cparams(("arbitrary", "arbitrary")),
        name="moe",
    )(xn, rl, rb, ltri, w1, w3, w2, h, g_final)


def _pad_heads_kv(w):
    d = w.shape[0]
    group = A_HEADS // A_KV_HEADS
    w = w.reshape(d, A_KV_HEADS, group, HEAD_DIM)
    parts = [jnp.pad(w[:, j], ((0, 0), (0, 0), (j * HEAD_DIM, LANES - (j + 1) * HEAD_DIM)))
             for j in range(A_KV_HEADS)]
    return jnp.concatenate(parts, axis=1).reshape(d, A_HEADS * LANES)


def _pack_w_in(w_in):
    parts, off = [], 0
    for n in IN_SPLITS:
        parts.append(w_in[:, off:off + n])
        off += n
    aq, ak, av, iq, ik, iw, bq, bk, bv, mq = parts
    d = w_in.shape[0]
    qscale = HEAD_DIM ** -0.5
    aq_x = _pad_heads_kv(aq * (qscale * math.log2(math.e)))
    iq_x = jnp.pad((iq * qscale).reshape(d, IDX_HEADS, HEAD_DIM),
                   ((0, 0), (0, 0), (0, LANES - HEAD_DIM))).reshape(d, IDX_HEADS * LANES)
    ik_x = jnp.pad(ik, ((0, 0), (0, LANES - ik.shape[1])))
    iw2 = jnp.concatenate([iw, iw], axis=1) * (IDX_HEADS ** -0.5)
    iw_x = jnp.pad(iw2, ((0, 0), (0, LANES - iw2.shape[1])))
    width = B_HEADS * HEAD_DIM
    bqs = bq * (qscale * math.log2(math.e))
    grp = lambda a, gi: a[:, gi * width:(gi + 1) * width]
    nat = jnp.concatenate([aq_x, iq_x, grp(bqs, 0), grp(bk, 0), ak, ik_x, av, iw_x, grp(bv, 0), mq], axis=1)
    dil = [jnp.concatenate([grp(bqs, gi), grp(bk, gi), grp(bv, gi)], axis=1) for gi in (1, 2)]
    return nat, dil


def _rope_tables(seq):
    half = HEAD_DIM // 2
    inv = ROPE_THETA ** (-jnp.arange(half, dtype=jnp.float32) / half)
    ang = jnp.arange(seq, dtype=jnp.float32)[:, None] * inv[None, :]
    cos = jnp.tile(jnp.cos(ang), (1, LANES // half))
    sign = jnp.tile(jnp.concatenate([-jnp.ones((half,), jnp.float32), jnp.ones((half,), jnp.float32)]),
                    LANES // HEAD_DIM)
    sin = jnp.tile(jnp.sin(ang), (1, LANES // half)) * sign[None, :]
    return cos, sin


def _layer(x3, mem, g_mix, g_mem, w_in, w_mem_kv, w_gate, b_gate, w_branch, w_out, g_ffn,
           w_group, b_group, w_sub, b_sub, w1, w3, w2, g_out):
    b, s, d = x3.shape
    bf = jnp.bfloat16
    cos_t, sin_t = _rope_tables(s)
    w_nat, w_dil = _pack_w_in(w_in)
    gm = g_mix[None, :]

    p4 = _in_proj(x3, gm, w_nat.astype(bf), cos_t, sin_t, P_MODES, 1)
    p3 = p4.reshape(b, s, p4.shape[3])
    srcs = [p4] + [_in_proj(x3, gm, w.astype(bf), cos_t, sin_t, D_MODES, B_PATTERNS[gi + 1][1])
                   for gi, w in enumerate(w_dil)]

    kv = _mem_kv(mem, g_mem[None, :], w_mem_kv.astype(bf))

    tri = (jnp.arange(MXU_WIDTH)[:, None] <= jnp.arange(MXU_WIDTH)[None, :]).astype(bf)
    ya = _dsa(p3, tri)

    obs, lss = [], []
    for gi, src in enumerate(srcs):
        tiles = (T_BQ, T_BK, T_BV) if gi == 0 else (0, 1, 2)
        o, l = _dilated(src, *tiles, gi)
        obs.append(o)
        lss.append(l)

    ym = _mem_attn(p3, kv)

    group = A_HEADS // A_KV_HEADS
    wa = jnp.swapaxes(w_branch[0].reshape(A_KV_HEADS, group, HEAD_DIM, d), 0, 1).reshape(-1, d).astype(bf)
    w_route = jnp.concatenate([w_group, jnp.moveaxis(w_sub, 0, 1).reshape(d, N_EXPERTS)], axis=1)
    w_route = jnp.pad(w_route, ((0, 0), (0, LANES - w_route.shape[1])))
    wrh = w_route.astype(bf)
    wrl = (w_route - wrh.astype(jnp.float32)).astype(bf)
    wr2 = jnp.concatenate([wrh, wrl], axis=1)
    r_bias = jnp.pad(jnp.concatenate([b_group, b_sub.reshape(-1)]), (0, LANES - MOE_GROUPS - N_EXPERTS))

    h, xn, rl = _merge(x3, gm, w_gate.astype(bf), b_gate[None, :], ya, wa, obs, lss,
                       w_branch[1].astype(bf), ym, w_branch[2].astype(bf), w_out.astype(bf),
                       g_ffn[None, :], wr2)
    t = b * s
    out = _moe(xn.reshape(t, d), rl.reshape(t, LANES), r_bias[None, :], h.reshape(t, d),
               w1.astype(bf), w3.astype(bf), w2.astype(bf), g_out[None, :])
    return out.reshape(b, s, d)


def kernel(x, mem, g_mix, g_mem, w_in, w_mem_kv, w_gate, b_gate, w_branch, w_out, g_ffn,
           w_group, b_group, w_sub, b_sub, w1, w3, w2, g_final):
    depth = g_mix.shape[0]
    assert depth == 1, "the final rmsnorm is applied right after the single layer"
    return _layer(x, mem, g_mix[0], g_mem[0], w_in[0], w_mem_kv[0], w_gate[0], b_gate[0], w_branch[0],
                  w_out[0], g_ffn[0], w_group[0], b_group[0], w_sub[0], b_sub[0], w1[0], w3[0], w2[0],
                  g_final)
```

```python
import functools
import math

import jax
import jax.numpy as jnp
from jax import lax
from jax.experimental import pallas as pl
from jax.experimental.pallas import tpu as pltpu

HEAD_DIM = 64
ROPE_THETA = 10000.0
RMS_EPS = 1e-6
A_HEADS = 8
A_KV_HEADS = 2
IDX_HEADS = 4
TOPK_MAX = 256
B_PATTERNS = ((128, 1), (512, 4), (2048, 16))
B_HEADS = 8
M_HEADS = 4
M_HEAD_DIM = 128
MOE_GROUPS = 4
EXPERTS_PER_GROUP = 4
N_EXPERTS = 16
IN_SPLITS = (512, 128, 128, 256, 64, 4, 1536, 1536, 1536, 512)

LANES = 128
MXU_WIDTH = 256
PROJ_TILE = 512
SLABS = PROJ_TILE // LANES
VMEM_LIMIT = 56 * 1024 * 1024

T_AQ, T_IQ, T_BQ, T_BK, T_MIX, T_BV, T_MQ = 0, 2, 3, 4, 5, 6, 7
P_MODES = (("rope",) * 4,) * 5 + (("rope", "rope", "plain", "hi_lo"),) + (("plain",) * 4,) * 2
SLAB_AK = T_MIX * SLABS + 0
SLAB_IK = T_MIX * SLABS + 1
SLAB_AV = T_MIX * SLABS + 2
SLAB_IW = T_MIX * SLABS + 3
D_MODES = (("rope",) * 4, ("rope",) * 4, ("plain",) * 4)

INT_MIN = -(2 ** 31)
NEG_BIG = -1e30

R_SUB0 = MOE_GROUPS

MOE_BLK = 256
MOE_STEP = 4


def _cparams(sem):
    return pltpu.CompilerParams(dimension_semantics=sem, vmem_limit_bytes=VMEM_LIMIT)


def _rms(xf, g):
    return xf * lax.rsqrt(jnp.mean(xf * xf, axis=-1, keepdims=True) + RMS_EPS) * g


def _rope_slab(y, cos, sin_signed, first_half):
    half = HEAD_DIM // 2
    partner = jnp.where(first_half, pltpu.roll(y, LANES - half, 1), pltpu.roll(y, half, 1))
    return y * cos + partner * sin_signed


def _in_proj_kernel(x_ref, g_ref, w_ref, cos_ref, sin_ref, *rest, modes, dil, rows, emit_n, normed):
    if emit_n:
        (o_ref, nout_ref, n_ref), stage = rest[:3], rest[3:]
    else:
        (o_ref, n_ref), stage = rest[:2], rest[2:]
    j = pl.program_id(2)
    tm = x_ref.shape[1]

    @pl.when(j == 0)
    def _():
        if normed:
            n_ref[...] = x_ref[0]
        else:
            n_ref[...] = _rms(x_ref[0], g_ref[...]).astype(jnp.bfloat16)
        if emit_n:
            nout_ref[0] = n_ref[...]

    lane = lax.broadcasted_iota(jnp.int32, (rows, LANES), 1)
    first_half = (lane & (HEAD_DIM - 1)) < (HEAD_DIM // 2)

    def epilogue(slab_modes):
        for r in range(tm // rows):
            rs = pl.ds(r * rows, rows)
            acc = jnp.dot(n_ref[rs, :], w_ref[...], preferred_element_type=jnp.float32)
            cos = cos_ref[rs, :]
            sin = sin_ref[rs, :]
            for s in range(SLABS):
                y = acc[:, s * LANES:(s + 1) * LANES]
                if slab_modes[s] == "rope":
                    y = _rope_slab(y, cos, sin, first_half)
                elif slab_modes[s] == "hi_lo":
                    resid = y - y.astype(jnp.bfloat16).astype(jnp.float32)
                    y = jnp.where(lane < IDX_HEADS, y, resid)
                if dil == 1:
                    o_ref[0, 0, rs, s * LANES:(s + 1) * LANES] = y.astype(o_ref.dtype)
                else:
                    stage[0][s, rs, :] = y
        if dil > 1:
            for r in range(dil):
                for s in range(SLABS):
                    o_ref[0, r, :, s * LANES:(s + 1) * LANES] = (
                        stage[0][s, pl.ds(r, tm // dil, stride=dil), :].astype(o_ref.dtype))

    for pattern in sorted(set(modes)):
        tiles = [t for t, m in enumerate(modes) if m == pattern]
        cond = functools.reduce(jnp.logical_or, [j == t for t in tiles])
        pl.when(cond)(functools.partial(epilogue, pattern))


def _in_proj(x3, g_mix, wp, cos_t, sin_t, modes, dil, emit_n=False):
    b, s, d = x3.shape
    tm = min(1024, s)
    ntiles = len(modes)
    normed = x3.dtype == jnp.bfloat16
    scratch = [pltpu.VMEM((tm, d), jnp.bfloat16)]
    if dil > 1:
        scratch.append(pltpu.VMEM((SLABS, tm, LANES), jnp.float32))
    out_specs = pl.BlockSpec((1, dil, tm // dil, PROJ_TILE), lambda bi, i, j: (bi, 0, i, j))
    out_shape = jax.ShapeDtypeStruct((b, dil, s // dil, ntiles * PROJ_TILE), jnp.bfloat16)
    if emit_n:
        out_specs = [out_specs, pl.BlockSpec((1, tm, d), lambda bi, i, j: (bi, i, 0))]
        out_shape = [out_shape, jax.ShapeDtypeStruct((b, s, d), jnp.bfloat16)]
    return pl.pallas_call(
        functools.partial(_in_proj_kernel, modes=modes, dil=dil, rows=128, emit_n=emit_n, normed=normed),
        grid=(b, s // tm, ntiles),
        in_specs=[
            pl.BlockSpec((1, tm, d), lambda bi, i, j: (bi, i, 0)),
            pl.BlockSpec((1, d), lambda bi, i, j: (0, 0)),
            pl.BlockSpec((d, PROJ_TILE), lambda bi, i, j: (0, j)),
            pl.BlockSpec((tm, LANES), lambda bi, i, j: (i, 0)),
            pl.BlockSpec((tm, LANES), lambda bi, i, j: (i, 0)),
        ],
        out_specs=out_specs,
        out_shape=out_shape,
        scratch_shapes=scratch,
        compiler_params=_cparams(("parallel", "parallel", "arbitrary")),
        name=f"in_proj_d{dil}",
    )(x3, g_mix, wp, cos_t, sin_t)


def _mem_kv_kernel(m_ref, g_ref, w_ref, o_ref):
    n = _rms(m_ref[0], g_ref[...]).astype(jnp.bfloat16)
    o_ref[0] = jnp.dot(n, w_ref[...], preferred_element_type=jnp.float32).astype(o_ref.dtype)


def _mem_kv(mem, g_mem, w_kv):
    b, m, d = mem.shape
    n = w_kv.shape[1]
    return pl.pallas_call(
        _mem_kv_kernel,
        grid=(b,),
        in_specs=[
            pl.BlockSpec((1, m, d), lambda i: (i, 0, 0)),
            pl.BlockSpec((1, d), lambda i: (0, 0)),
            pl.BlockSpec((d, n), lambda i: (0, 0)),
        ],
        out_specs=pl.BlockSpec((1, m, n), lambda i: (i, 0, 0)),
        out_shape=jax.ShapeDtypeStruct((b, m, n), jnp.bfloat16),
        compiler_params=_cparams(("parallel",)),
        name="mem_kv",
    )(mem, g_mem, w_kv)


def _key_to_float(key):
    bits = jnp.where(key < 0, key ^ jnp.int32(0x7FFFFFFF), key)
    f = lax.bitcast_convert_type(bits, jnp.float32)
    return jnp.where((key < 0) & (f != f), -jnp.inf, f)


def _cmp_swap(v, i, j):
    v[i], v[j] = jnp.maximum(v[i], v[j]), jnp.minimum(v[i], v[j])


def _merge_top12(tops, new):
    assert len(tops) == 12 and len(new) == 4
    y = list(new)
    for i, j in ((0, 1), (2, 3), (0, 2), (1, 3), (1, 2)):
        _cmp_swap(y, i, j)
    v = list(tops) + y[::-1]
    for i in range(4, 8):
        _cmp_swap(v, i, i + 8)
    for stride in (4, 2, 1):
        for i in range(8):
            if i % (2 * stride) < stride:
                _cmp_swap(v, i, i + stride)
    for i in range(8, 12):
        v[i] = jnp.maximum(v[i], v[i + 4])
    for stride in (2, 1):
        for i in range(8, 12):
            if (i - 8) % (2 * stride) < stride:
                _cmp_swap(v, i, i + stride)
    return v[:12]


def _dsa_kernel(aq_ref, iq_ref, iw_ref, ik_ref, ak_ref, av_ref, tri_ref, o_ref,
                sc_ref, lists_ref, key_ref, thr_ref, need_ref, off_ref, m_ref, acc_ref,
                *, tq, kc, topk, depth, ins_rows, bis_rows, steps_per_chunk):
    i = pl.program_id(1)
    nt = pl.num_programs(1) - 1
    slot = i % 2
    pslot = 1 - slot
    nslab = kc // LANES
    group = A_HEADS // A_KV_HEADS
    nbits = 32
    blocks = [pl.ds(rb * bis_rows, bis_rows) for rb in range(tq // bis_rows)]

    def chunks_of(tile):
        return (tile * tq + tq + kc - 1) // kc

    @pl.when(i < nt)
    def _():
        q0 = i * tq
        iw = iw_ref[0].astype(jnp.float32)
        w_heads = [iw[:, h:h + 1] + iw[:, IDX_HEADS + h:IDX_HEADS + h + 1] for h in range(IDX_HEADS)]
        qpos = q0 + lax.broadcasted_iota(jnp.int32, (tq, kc), 0)
        kiota = lax.broadcasted_iota(jnp.int32, (tq, kc), 1)

        def score_chunk(c, carry):
            k0 = pl.multiple_of(c * kc, kc)
            ikc = ik_ref[0, pl.ds(k0, kc), :]
            s = jnp.zeros((tq, kc), jnp.float32)
            for h in range(IDX_HEADS):
                d = lax.dot_general(iq_ref[0, :, h * LANES:(h + 1) * LANES], ikc,
                                    (((1,), (1,)), ((), ())), preferred_element_type=jnp.float32)
                s = s + jnp.maximum(d, 0.0) * w_heads[h]
            s = jnp.where(kiota + k0 <= qpos, s, -jnp.inf)
            sc_ref[slot, c] = s
            for rg in range(tq // ins_rows):
                rows = slice(rg * ins_rows, (rg + 1) * ins_rows)
                tops = [lists_ref[rows, j * LANES:(j + 1) * LANES] for j in range(depth)]
                new = [s[rows, sl * LANES:(sl + 1) * LANES] for sl in range(nslab)]
                tops = _merge_top12(tops, new)
                for j in range(depth):
                    lists_ref[rows, j * LANES:(j + 1) * LANES] = tops[j]
            return carry

        lists_ref[...] = jnp.full(lists_ref.shape, -jnp.inf, jnp.float32)
        lax.fori_loop(0, chunks_of(i), score_chunk, 0)
        key_ref[...] = jnp.full(key_ref.shape, INT_MIN, jnp.int32)

    def list_count(rs, cand, strict):
        acc = jnp.zeros((bis_rows, LANES), jnp.float32)
        for j in range(depth):
            v = lists_ref[rs, j * LANES:(j + 1) * LANES]
            acc = acc + jnp.where((v > cand) if strict else (v >= cand), 1.0, 0.0)
        return jnp.sum(acc, axis=-1, keepdims=True)

    def sorted_count(rs, cand):
        lv = [lists_ref[rs, j * LANES:(j + 1) * LANES] for j in range(depth)]
        a = lv[5] >= cand
        b = jnp.where(a, lv[8], lv[2]) >= cand
        c = jnp.where(a, jnp.where(b, lv[10], lv[6]), jnp.where(b, lv[3], lv[0])) >= cand
        d = jnp.where(a, jnp.where(b, jnp.where(c, lv[11], lv[9]), lv[7]),
                      jnp.where(b, lv[4], lv[1])) >= cand
        per_lane = (jnp.where(a, 6.0, 0.0) + jnp.where(b, 3.0, 0.0) + jnp.where(c, 1.0, 0.0)
                    + jnp.where(d, 1.0, 0.0) + jnp.where(a & b & c, 1.0, 0.0))
        return jnp.sum(per_lane, axis=-1, keepdims=True)

    def bit_step(it):
        for rs in blocks:
            pre = key_ref[rs, :]
            cand = pre + lax.shift_left(jnp.int32(1), nbits - 1 - it)
            cnt = sorted_count(rs, _key_to_float(cand))
            key_ref[rs, :] = jnp.where(cnt >= float(topk), cand, pre)

    half = tri_ref.shape[0]

    def attn_chunk(c):
        k0 = pl.multiple_of(c * kc, kc)
        akc = ak_ref[0, pl.ds(k0, kc), :]
        avc = av_ref[0, pl.ds(k0, kc), :]
        thr_b = thr_ref[pslot]
        need_b = need_ref[pslot]
        bias = []
        off = off_ref[...]
        for hh in range(kc // half):
            eq_parts = []
            for s in range(half // LANES):
                v = sc_ref[pslot, c, :, hh * half + s * LANES: hh * half + (s + 1) * LANES]
                eq_parts.append(jnp.where(v == thr_b, 1.0, 0.0).astype(jnp.bfloat16))
            eqf = jnp.concatenate(eq_parts, axis=1)
            pre = jnp.dot(eqf, tri_ref[...], preferred_element_type=jnp.float32)
            for s in range(half // LANES):
                v = sc_ref[pslot, c, :, hh * half + s * LANES: hh * half + (s + 1) * LANES]
                rank = pre[:, s * LANES:(s + 1) * LANES] + off
                tie_ok = (v == thr_b) & (rank <= need_b)
                sel = (v > thr_b) | tie_ok
                bias.append(jnp.where(sel, 0.0, NEG_BIG))
            off = off + jnp.sum(eqf.astype(jnp.float32), axis=-1, keepdims=True)
        off_ref[...] = off

        av_ones = jnp.concatenate([avc, jnp.ones_like(avc)], axis=1)
        for h in range(A_HEADS):
            logits = lax.dot_general(aq_ref[0, :, h * LANES:(h + 1) * LANES], akc,
                                     (((1,), (1,)), ((), ())), preferred_element_type=jnp.float32)
            lg = [logits[:, s * LANES:(s + 1) * LANES] + bias[s] for s in range(nslab)]
            m_old = m_ref[h]
            m_cur = functools.reduce(jnp.maximum, lg)
            m_new = jnp.maximum(m_old, jnp.max(m_cur, axis=-1, keepdims=True))
            alpha = jnp.exp2(m_old - m_new)
            p = jnp.concatenate([jnp.exp2((x - m_new).astype(jnp.bfloat16)) for x in lg], axis=1)
            m_ref[h] = m_new
            pv = jnp.dot(p, av_ones, preferred_element_type=jnp.float32)
            acc_ref[h] = jnp.concatenate([alpha, alpha], axis=1) * acc_ref[h] + pv

    nprev = chunks_of(i - 1)
    nride = jnp.minimum(nprev, nbits // steps_per_chunk)

    @pl.when(i >= 1)
    def _():
        off_ref[...] = jnp.zeros_like(off_ref)
        m_ref[...] = jnp.full(m_ref.shape, NEG_BIG, jnp.float32)
        acc_ref[...] = jnp.zeros_like(acc_ref)

        def chunk_with_steps(c, carry):
            attn_chunk(c)
            for j in range(steps_per_chunk):
                bit_step(c * steps_per_chunk + j)
            return carry

        def chunk_plain(c, carry):
            attn_chunk(c)
            return carry

        lax.fori_loop(0, nride, chunk_with_steps, 0)
        lax.fori_loop(nride, nprev, chunk_plain, 0)

        low = lax.broadcasted_iota(jnp.int32, (tq, LANES), 1) < HEAD_DIM
        for p in range(group):
            outs = [acc_ref[h, :, :LANES] / acc_ref[h, :, LANES:] for h in (p, group + p)]
            o_ref[0, :, p * LANES:(p + 1) * LANES] = jnp.where(low, outs[0], outs[1]).astype(o_ref.dtype)

    @pl.when(i < nt)
    def _():
        nchunks = chunks_of(i)
        done = jnp.where(i >= 1, nride * steps_per_chunk, 0)

        def rest(it, carry):
            bit_step(it)
            return carry

        lax.fori_loop(done, nbits, rest, 0)

        def finish(rs, thr, c_gt):
            thr_ref[slot, rs, :] = thr
            need_ref[slot, rs, :] = jnp.where(thr == -jnp.inf, 0.0, float(topk) - c_gt)

        def full_count(rs, cand, strict):
            def body(c, acc):
                for s in range(nslab):
                    v = sc_ref[slot, c, rs, s * LANES:(s + 1) * LANES]
                    acc = acc + jnp.where((v > cand) if strict else (v >= cand), 1.0, 0.0)
                return acc
            acc = lax.fori_loop(0, nchunks, body, jnp.zeros((bis_rows, LANES), jnp.float32))
            return jnp.sum(acc, axis=-1, keepdims=True)

        def redo_block(rs):
            def step(it, prefix):
                cand = prefix + lax.shift_left(jnp.int32(1), nbits - 1 - it)
                cnt = full_count(rs, _key_to_float(cand), False)
                return jnp.where(cnt >= float(topk), cand, prefix)
            key = lax.fori_loop(0, nbits, step, jnp.full((bis_rows, LANES), INT_MIN, jnp.int32))
            thr = _key_to_float(key)
            finish(rs, thr, full_count(rs, thr, True))

        overflows = []
        for rs in blocks:
            thr = _key_to_float(key_ref[rs, :])
            finish(rs, thr, list_count(rs, thr, True))
            last = lists_ref[rs, (depth - 1) * LANES:depth * LANES]
            overflows.append(jnp.where(last > thr, 1.0, 0.0))

        @pl.when(jnp.max(functools.reduce(jnp.maximum, overflows)) > 0.0)
        def _():
            for rs, over in zip(blocks, overflows):
                pl.when(jnp.max(over) > 0.0)(functools.partial(redo_block, rs))


def _dsa(p3, tri):
    b, s, _ = p3.shape
    tq = min(256, s)
    kc = min(512, s)
    nt = s // tq
    topk = min(TOPK_MAX, s // 4)
    depth = 12
    kern = functools.partial(_dsa_kernel, tq=tq, kc=kc, topk=topk, depth=depth,
                             ins_rows=16, bis_rows=min(64, tq), steps_per_chunk=4)
    wide = A_HEADS * LANES
    prev = lambda i: jnp.maximum(i - 1, 0)
    cur = lambda i: jnp.minimum(i, nt - 1)
    return pl.pallas_call(
        kern,
        grid=(b, nt + 1),
        in_specs=[
            pl.BlockSpec((1, tq, wide), lambda bi, i: (bi, prev(i), T_AQ)),
            pl.BlockSpec((1, tq, PROJ_TILE), lambda bi, i: (bi, cur(i), T_IQ)),
            pl.BlockSpec((1, tq, LANES), lambda bi, i: (bi, cur(i), SLAB_IW)),
            pl.BlockSpec((1, s, LANES), lambda bi, i: (bi, 0, SLAB_IK)),
            pl.BlockSpec((1, s, LANES), lambda bi, i: (bi, 0, SLAB_AK)),
            pl.BlockSpec((1, s, LANES), lambda bi, i: (bi, 0, SLAB_AV)),
            pl.BlockSpec(tri.shape, lambda bi, i: (0, 0)),
        ],
        out_specs=pl.BlockSpec((1, tq, wide // 2), lambda bi, i: (bi, prev(i), 0)),
        out_shape=jax.ShapeDtypeStruct((b, s, wide // 2), jnp.bfloat16),
        scratch_shapes=[
            pltpu.VMEM((2, s // kc, tq, kc), jnp.float32),
            pltpu.VMEM((tq, depth * LANES), jnp.float32),
            pltpu.VMEM((tq, LANES), jnp.int32),
            pltpu.VMEM((2, tq, LANES), jnp.float32),
            pltpu.VMEM((2, tq, LANES), jnp.float32),
            pltpu.VMEM((tq, LANES), jnp.float32),
            pltpu.VMEM((A_HEADS, tq, LANES), jnp.float32),
            pltpu.VMEM((A_HEADS, tq, 2 * LANES), jnp.float32),
        ],
        compiler_params=_cparams(("parallel", "arbitrary")),
        name="dsa",
    )(p3, p3, p3, p3, p3, p3, tri)


def _dil_kernel(q_ref, kp_ref, kc_ref, vp_ref, vc_ref, o_ref, lse_ref, *, tu, w):
    u = pl.program_id(2)
    qq = lax.broadcasted_iota(jnp.int32, (2 * w, 2 * w), 0) % w
    kk = lax.broadcasted_iota(jnp.int32, (2 * w, 2 * w), 1)
    band = (kk >= qq) & (kk <= qq + w)
    band_first = band & (kk >= jnp.where(u > 0, 0, w))
    lane = lax.broadcasted_iota(jnp.int32, (w, LANES), 1)
    low = lane < HEAD_DIM
    for sb in range(tu // w):
        rows = slice(sb * w, (sb + 1) * w)
        lse_tile = jnp.zeros((w, LANES), jnp.float32)
        for pr in range(B_HEADS // 2):
            sl = slice(pr * LANES, (pr + 1) * LANES)
            q = q_ref[0, 0, rows, sl]
            if sb == 0:
                k = jnp.concatenate([kp_ref[0, 0, :, sl], kc_ref[0, 0, :w, sl]], axis=0)
                v = jnp.concatenate([vp_ref[0, 0, :, sl], vc_ref[0, 0, :w, sl]], axis=0)
            else:
                k = kc_ref[0, 0, (sb - 1) * w:(sb + 1) * w, sl]
                v = vc_ref[0, 0, (sb - 1) * w:(sb + 1) * w, sl]
            zero = jnp.zeros_like(q)
            qm = jnp.concatenate([jnp.where(low, q, zero), jnp.where(low, zero, q)], axis=0)
            lg = lax.dot_general(qm, k, (((1,), (1,)), ((), ())), preferred_element_type=jnp.float32)
            lg = jnp.where(band_first if sb == 0 else band, lg, NEG_BIG)
            m = jnp.max(lg, axis=-1, keepdims=True)
            p = jnp.exp2(lg - m).astype(jnp.bfloat16)
            nd = jnp.dot(p, jnp.concatenate([v, jnp.ones_like(v)], axis=1),
                         preferred_element_type=jnp.float32)
            l = nd[:, LANES:]
            o = nd[:, :LANES] / l
            o_ref[0, 0, rows, sl] = jnp.where(low, o[:w], o[w:]).astype(o_ref.dtype)
            lse = (m + jnp.log2(l)) * math.log(2.0)
            lse_tile = jnp.where(lane == 2 * pr, lse[:w], lse_tile)
            lse_tile = jnp.where(lane == 2 * pr + 1, lse[w:], lse_tile)
        lse_ref[0, 0, rows, :] = lse_tile


def _dilated(src, tq_tile, tk_tile, tv_tile, gi):
    b, dil, su, _ = src.shape
    window, d2 = B_PATTERNS[gi]
    assert d2 == dil
    w = window // dil
    tu = min(512, su)
    ratio = tu // w
    width = B_HEADS * HEAD_DIM
    cur = lambda t: pl.BlockSpec((1, 1, tu, width), lambda bi, r, u: (bi, r, u, t))
    prev = lambda t: pl.BlockSpec((1, 1, w, width),
                                  lambda bi, r, u: (bi, r, jnp.maximum(u * ratio - 1, 0), t))
    return pl.pallas_call(
        functools.partial(_dil_kernel, tu=tu, w=w),
        grid=(b, dil, su // tu),
        in_specs=[cur(tq_tile), prev(tk_tile), cur(tk_tile), prev(tv_tile), cur(tv_tile)],
        out_specs=[pl.BlockSpec((1, 1, tu, width), lambda bi, r, u: (bi, r, u, 0)),
                   pl.BlockSpec((1, 1, tu, LANES), lambda bi, r, u: (bi, r, u, 0))],
        out_shape=[jax.ShapeDtypeStruct((b, dil, su, width), jnp.bfloat16),
                   jax.ShapeDtypeStruct((b, dil, su, LANES), jnp.float32)],
        compiler_params=_cparams(("parallel", "parallel", "arbitrary")),
        name=f"dilated{gi}",
    )(src, src, src, src, src)


def _mem_attn_kernel(q_ref, k_ref, v_ref, o_ref):
    scale = M_HEAD_DIM ** -0.5
    for h in range(M_HEADS):
        sl = slice(h * M_HEAD_DIM, (h + 1) * M_HEAD_DIM)
        lg = lax.dot_general(q_ref[0, :, sl], k_ref[0, :, sl], (((1,), (1,)), ((), ())),
                             preferred_element_type=jnp.float32) * scale
        m = jnp.max(lg, axis=-1, keepdims=True)
        p = jnp.exp(lg - m)
        l = jnp.sum(p, axis=-1, keepdims=True)
        o = jnp.dot(p.astype(jnp.bfloat16), v_ref[0, :, sl], preferred_element_type=jnp.float32)
        o_ref[0, :, sl] = (o / l).astype(o_ref.dtype)


def _mem_attn(p3, kv):
    b, s, _ = p3.shape
    m = kv.shape[1]
    width = M_HEADS * M_HEAD_DIM
    tq = min(512, s)
    return pl.pallas_call(
        _mem_attn_kernel,
        grid=(b, s // tq),
        in_specs=[
            pl.BlockSpec((1, tq, width), lambda bi, i: (bi, i, T_MQ)),
            pl.BlockSpec((1, m, width), lambda bi, i: (bi, 0, 0)),
            pl.BlockSpec((1, m, width), lambda bi, i: (bi, 0, 1)),
        ],
        out_specs=pl.BlockSpec((1, tq, width), lambda bi, i: (bi, i, 0)),
        out_shape=jax.ShapeDtypeStruct((b, s, width), jnp.bfloat16),
        compiler_params=_cparams(("parallel", "parallel")),
        name="mem_attn",
    )(p3, kv, kv)


def _split_bf16(v):
    hi = v.astype(jnp.bfloat16)
    lo = (v - hi.astype(jnp.float32)).astype(jnp.bfloat16)
    return hi, lo


def _merge_kernel(x_ref, gmix_ref, wg_ref, bg_ref, ya_ref, wa_ref, ob0_ref, ob1_ref, ob2_ref,
                  ls0_ref, ls1_ref, ls2_ref, wb_ref, ym_ref, wm_ref, wo_ref, gffn_ref,
                  wr2_ref, h_ref, xn_ref, rl_ref, ls_scr, ob_scr):
    d = x_ref.shape[2]
    tm = x_ref.shape[1]
    x = x_ref[0]
    n = _rms(x, gmix_ref[...]).astype(jnp.bfloat16)

    def gate(k):
        z = jnp.dot(n, wg_ref[:, k * d:(k + 1) * d], preferred_element_type=jnp.float32)
        return jax.nn.sigmoid(z + bg_ref[:, k * d:(k + 1) * d])

    merged = gate(0) * jnp.dot(ya_ref[0], wa_ref[...], preferred_element_type=jnp.float32)

    ob_refs = (ob0_ref, ob1_ref, ob2_ref)
    ls = []
    for gi, ls_ref in enumerate((ls0_ref, ls1_ref, ls2_ref)):
        dil = ls_ref.shape[1]
        if dil == 1:
            ls.append(ls_ref[0, 0])
        else:
            for r in range(dil):
                ls_scr[gi - 1, pl.ds(r, tm // dil, stride=dil), :] = ls_ref[0, r]
            ls.append(ls_scr[gi - 1])
    mx = jnp.maximum(jnp.maximum(ls[0], ls[1]), ls[2])
    es = [jnp.exp(v - mx) for v in ls]
    inv = 1.0 / (es[0] + es[1] + es[2])
    yb = [jnp.zeros((tm, LANES), jnp.float32) for _ in range(SLABS)]
    low = lax.broadcasted_iota(jnp.int32, (tm, LANES), 1) < HEAD_DIM
    for e, ob in zip(es, ob_refs):
        dil = ob.shape[1]
        a = e * inv
        for s in range(SLABS):
            sl = slice(s * LANES, (s + 1) * LANES)
            if dil == 1:
                o = ob[0, 0, :, sl].astype(jnp.float32)
            else:
                for r in range(dil):
                    ob_scr[s, pl.ds(r, tm // dil, stride=dil), :] = ob[0, r, :, sl].astype(jnp.float32)
                o = ob_scr[s]
            a_s = jnp.where(low, jnp.broadcast_to(a[:, 2 * s:2 * s + 1], (tm, LANES)),
                            jnp.broadcast_to(a[:, 2 * s + 1:2 * s + 2], (tm, LANES)))
            yb[s] = yb[s] + a_s * o
    yb = jnp.concatenate(yb, axis=1)
    merged = merged + gate(1) * jnp.dot(yb.astype(jnp.bfloat16), wb_ref[...],
                                        preferred_element_type=jnp.float32)
    merged = merged + gate(2) * jnp.dot(ym_ref[0], wm_ref[...], preferred_element_type=jnp.float32)

    h = x + jnp.dot(merged.astype(jnp.bfloat16), wo_ref[...], preferred_element_type=jnp.float32)
    h_ref[0] = h
    xn = _rms(h, gffn_ref[...])
    xn_ref[0] = xn.astype(jnp.bfloat16)
    hi, lo = _split_bf16(xn)
    rh = jnp.dot(hi, wr2_ref[...], preferred_element_type=jnp.float32)
    rlo = jnp.dot(lo, wr2_ref[...], preferred_element_type=jnp.float32)
    rl_ref[0] = rh[:, :LANES] + rh[:, LANES:] + rlo[:, :LANES]


def _merge(x3, g_mix, wg, bg, ya, wa, obs, lss, wb, ym, wm, wo, g_ffn, wr2):
    b, s, d = x3.shape
    tm = min(512, s)
    row = lambda w: pl.BlockSpec((1, tm, w), lambda bi, i: (bi, i, 0))
    full = lambda a: pl.BlockSpec(a.shape, lambda bi, i: (0,) * a.ndim)

    def dil_spec(a):
        dil, width = a.shape[1], a.shape[3]
        return pl.BlockSpec((1, dil, tm // dil, width), lambda bi, i: (bi, 0, i, 0))

    return pl.pallas_call(
        _merge_kernel,
        grid=(b, s // tm),
        in_specs=[row(d), full(g_mix), full(wg), full(bg), row(ya.shape[2]), full(wa),
                  dil_spec(obs[0]), dil_spec(obs[1]), dil_spec(obs[2]),
                  dil_spec(lss[0]), dil_spec(lss[1]), dil_spec(lss[2]), full(wb),
                  row(ym.shape[2]), full(wm), full(wo), full(g_ffn), full(wr2)],
        out_specs=[row(d), row(d), row(LANES)],
        out_shape=[jax.ShapeDtypeStruct((b, s, d), jnp.float32),
                   jax.ShapeDtypeStruct((b, s, d), jnp.bfloat16),
                   jax.ShapeDtypeStruct((b, s, LANES), jnp.float32)],
        scratch_shapes=[pltpu.VMEM((2, tm, LANES), jnp.float32),
                        pltpu.VMEM((SLABS, tm, LANES), jnp.float32)],
        compiler_params=_cparams(("parallel", "parallel")),
        name="merge",
    )(x3, g_mix, wg, bg, ya, wa, obs[0], obs[1], obs[2], lss[0], lss[1], lss[2], wb,
      ym, wm, wo, g_ffn, wr2)


def _first_lane_where(cond, lane):
    return jnp.min(jnp.where(cond, lane, float(LANES)), axis=-1, keepdims=True)


def _route(logits, rb):
    z = logits + rb
    lane = lax.broadcasted_iota(jnp.int32, z.shape, 1).astype(jnp.float32)
    is_g = lane < MOE_GROUPS
    zg = jnp.where(is_g, z, -jnp.inf)
    mg = jnp.max(zg, axis=-1, keepdims=True)
    eg = jnp.exp(zg - mg)
    gp = eg / jnp.sum(eg, axis=-1, keepdims=True)
    g_w = jnp.max(gp, axis=-1, keepdims=True)
    g_sel = _first_lane_where(is_g & (gp == g_w), lane)
    lo = R_SUB0 + g_sel * EXPERTS_PER_GROUP
    in_grp = (lane >= lo) & (lane < lo + EXPERTS_PER_GROUP)
    zs = jnp.where(in_grp, z, -jnp.inf)
    ms = jnp.max(zs, axis=-1, keepdims=True)
    es = jnp.exp(zs - ms)
    sp = es / jnp.sum(es, axis=-1, keepdims=True)
    p1 = jnp.max(sp, axis=-1, keepdims=True)
    i1 = _first_lane_where(in_grp & (sp == p1), lane)
    rest = in_grp & (lane != i1)
    sp2 = jnp.where(rest, sp, -1.0)
    p2 = jnp.max(sp2, axis=-1, keepdims=True)
    i2 = _first_lane_where(rest & (sp2 == p2), lane)
    tot = p1 + p2
    comb = jnp.where(lane == i1, g_w * (p1 / tot), jnp.where(lane == i2, g_w * (p2 / tot), 0.0))
    return comb, g_sel


def _moe_kernel(xn_ref, rl_ref, rb_ref, ltri_ref, w1_ref, w3_ref, w2_ref, h_ref, gf_ref, o_ref,
                xs_ref, cs_ref, acc_ref, perm_ref, permt_ref, blk_ref, moe_ref):
    i = pl.program_id(0)
    nt = pl.num_programs(0) - 1
    pr = pl.program_id(1)
    tm, d = xn_ref.shape
    qrows = o_ref.shape[0]

    @pl.when(i >= 1)
    def _():
        prev = moe_ref[pl.ds(pl.multiple_of(pr * qrows, qrows), qrows), :].astype(jnp.float32)
        o_ref[...] = _rms(h_ref[...] + prev, gf_ref[...])

    @pl.when(i < nt)
    def _():
        _moe_tile(xn_ref, rl_ref, rb_ref, ltri_ref, w1_ref, w3_ref, w2_ref, moe_ref,
                  xs_ref, cs_ref, acc_ref, perm_ref, permt_ref, blk_ref, pr)


def _moe_tile(xn_ref, rl_ref, rb_ref, ltri_ref, w1_ref, w3_ref, w2_ref, moe_ref,
              xs_ref, cs_ref, acc_ref, perm_ref, permt_ref, blk_ref, pr):
    tm, d = xn_ref.shape

    @pl.when(pr == 0)
    def _():
        comb, g_sel = _route(rl_ref[...], rb_ref[...])
        lane = lax.broadcasted_iota(jnp.int32, (tm, LANES), 1).astype(jnp.float32)
        onehot = jnp.where(lane == g_sel, 1.0, 0.0)
        seen = jnp.dot(ltri_ref[...], onehot.astype(jnp.bfloat16), preferred_element_type=jnp.float32)
        counts = seen[tm - 1:tm, :]
        lane_row = lane[0:1, :]
        off = 0.0
        off_row = jnp.zeros((1, LANES), jnp.float32)
        for g in range(MOE_GROUPS):
            n_g = jnp.sum(jnp.where(lane_row == g, counts, 0.0))
            off_row = jnp.where(lane_row == g, off, off_row)
            first = off.astype(jnp.int32) if g else jnp.int32(0)
            start = (first // 16) * 16
            span = first - start + n_g.astype(jnp.int32)
            tail = span % MOE_BLK
            blk_ref[g] = start
            blk_ref[MOE_GROUPS + g] = span // MOE_BLK + (tail > MOE_BLK // 2).astype(jnp.int32)
            blk_ref[2 * MOE_GROUPS + g] = ((tail > 0) & (tail <= MOE_BLK // 2)).astype(jnp.int32)
            off = off + n_g
        dest = jnp.sum(onehot * (off_row + seen - 1.0), axis=-1, keepdims=True)
        dest_row = jnp.transpose(jnp.broadcast_to(dest, (tm, LANES)))[0:1, :]
        row = lax.broadcasted_iota(jnp.int32, (tm, LANES), 0).astype(jnp.float32)
        for s in range(tm // LANES):
            sl = slice(s * LANES, (s + 1) * LANES)
            permt_ref[:, sl] = jnp.where(lane + float(s * LANES) == dest, 1.0, 0.0).astype(jnp.bfloat16)
            perm_ref[:, sl] = jnp.where(row == dest_row[:, sl], 1.0, 0.0).astype(jnp.bfloat16)
        perm = perm_ref[...]
        xs_ref[0:tm, :] = jnp.dot(perm, xn_ref[...], preferred_element_type=jnp.float32).astype(jnp.bfloat16)
        hi, lo = _split_bf16(comb)
        cs_ref[0:tm, :] = (jnp.dot(perm, hi, preferred_element_type=jnp.float32)
                           + jnp.dot(perm, lo, preferred_element_type=jnp.float32))
        xs_ref[tm:, :] = jnp.zeros((MOE_BLK, xs_ref.shape[1]), jnp.bfloat16)
        cs_ref[tm:, :] = jnp.zeros((MOE_BLK, LANES), jnp.float32)
        acc_ref[...] = jnp.zeros_like(acc_ref)

    g = pr // (EXPERTS_PER_GROUP // MOE_STEP)
    start = blk_ref[g]
    nfull = blk_ref[MOE_GROUPS + g]

    def run_experts(r0, nrows):
        rows = pl.ds(pl.multiple_of(r0, 16), nrows)
        xb = xs_ref[rows, :]
        cb = cs_ref[rows, :]
        lane_i = lax.broadcasted_iota(jnp.int32, (nrows, LANES), 1)
        y = jnp.zeros((nrows, d), jnp.float32)
        for e2 in range(MOE_STEP):
            c = jnp.sum(jnp.where(lane_i == R_SUB0 + MOE_STEP * pr + e2, cb, 0.0), axis=-1, keepdims=True)
            a = jnp.dot(xb, w1_ref[e2], preferred_element_type=jnp.float32)
            u = jnp.dot(xb, w3_ref[e2], preferred_element_type=jnp.float32)
            hid = (jax.nn.silu(a) * u * c).astype(jnp.bfloat16)
            y = y + jnp.dot(hid, w2_ref[e2], preferred_element_type=jnp.float32)
        acc_ref[rows, :] += y

    def block(b, carry):
        run_experts(start + b * MOE_BLK, MOE_BLK)
        return carry

    lax.fori_loop(0, nfull, block, 0)

    @pl.when(blk_ref[2 * MOE_GROUPS + g] > 0)
    def _():
        run_experts(start + nfull * MOE_BLK, MOE_BLK // 2)

    @pl.when(pr == N_EXPERTS // MOE_STEP - 1)
    def _():
        moe_ref[...] = jnp.dot(permt_ref[...], acc_ref[0:tm, :].astype(jnp.bfloat16),
                               preferred_element_type=jnp.float32).astype(moe_ref.dtype)


def _moe(xn, rl, rb, h, w1, w3, w2, g_final):
    t, d = xn.shape
    hid = w1.shape[2]
    tm = min(1024, t)
    nt = t // tm
    nq = N_EXPERTS // MOE_STEP
    qrows = tm // nq
    ltri = (jnp.arange(tm)[:, None] >= jnp.arange(tm)[None, :]).astype(jnp.bfloat16)
    tile = lambda i, p: (jnp.minimum(i, nt - 1), 0)
    wblk = lambda i, p: (jnp.where(i < nt, p, nq - 1), 0, 0)
    lagged = lambda i, p: (jnp.where(i == 0, 0, (i - 1) * nq + p), 0)
    return pl.pallas_call(
        _moe_kernel,
        grid=(nt + 1, nq),
        in_specs=[
            pl.BlockSpec((tm, d), tile),
            pl.BlockSpec((tm, LANES), tile),
            pl.BlockSpec((1, LANES), lambda i, p: (0, 0)),
            pl.BlockSpec((tm, tm), lambda i, p: (0, 0)),
            pl.BlockSpec((MOE_STEP, d, hid), wblk),
            pl.BlockSpec((MOE_STEP, d, hid), wblk),
            pl.BlockSpec((MOE_STEP, hid, d), wblk),
            pl.BlockSpec((qrows, d), lagged),
            pl.BlockSpec((1, d), lambda i, p: (0, 0)),
        ],
        out_specs=pl.BlockSpec((qrows, d), lagged),
        out_shape=jax.ShapeDtypeStruct((t, d), jnp.float32),
        scratch_shapes=[
            pltpu.VMEM((tm + MOE_BLK, d), jnp.bfloat16),
            pltpu.VMEM((tm + MOE_BLK, LANES), jnp.float32),
            pltpu.VMEM((tm + MOE_BLK, d), jnp.float32),
            pltpu.VMEM((tm, tm), jnp.bfloat16),
            pltpu.VMEM((tm, tm), jnp.bfloat16),
            pltpu.SMEM((3 * MOE_GROUPS,), jnp.int32),
            pltpu.VMEM((tm, d), jnp.bfloat16),
        ],
        compiler_params=_cparams(("arbitrary", "arbitrary")),
        name="moe",
    )(xn, rl, rb, ltri, w1, w3, w2, h, g_final)


def _pad_heads_kv(w):
    d = w.shape[0]
    group = A_HEADS // A_KV_HEADS
    w = w.reshape(d, A_KV_HEADS, group, HEAD_DIM)
    parts = [jnp.pad(w[:, j], ((0, 0), (0, 0), (j * HEAD_DIM, LANES - (j + 1) * HEAD_DIM)))
             for j in range(A_KV_HEADS)]
    return jnp.concatenate(parts, axis=1).reshape(d, A_HEADS * LANES)


def _pack_w_in(w_in):
    parts, off = [], 0
    for n in IN_SPLITS:
        parts.append(w_in[:, off:off + n])
        off += n
    aq, ak, av, iq, ik, iw, bq, bk, bv, mq = parts
    d = w_in.shape[0]
    qscale = HEAD_DIM ** -0.5
    aq_x = _pad_heads_kv(aq * (qscale * math.log2(math.e)))
    iq_x = jnp.pad((iq * qscale).reshape(d, IDX_HEADS, HEAD_DIM),
                   ((0, 0), (0, 0), (0, LANES - HEAD_DIM))).reshape(d, IDX_HEADS * LANES)
    ik_x = jnp.pad(ik, ((0, 0), (0, LANES - ik.shape[1])))
    iw2 = jnp.concatenate([iw, iw], axis=1) * (IDX_HEADS ** -0.5)
    iw_x = jnp.pad(iw2, ((0, 0), (0, LANES - iw2.shape[1])))
    width = B_HEADS * HEAD_DIM
    bqs = bq * (qscale * math.log2(math.e))
    grp = lambda a, gi: a[:, gi * width:(gi + 1) * width]
    nat = jnp.concatenate([aq_x, iq_x, grp(bqs, 0), grp(bk, 0), ak, ik_x, av, iw_x, grp(bv, 0), mq], axis=1)
    dil = [jnp.concatenate([grp(bqs, gi), grp(bk, gi), grp(bv, gi)], axis=1) for gi in (1, 2)]
    return nat, dil


def _rope_tables(seq):
    half = HEAD_DIM // 2
    inv = ROPE_THETA ** (-jnp.arange(half, dtype=jnp.float32) / half)
    ang = jnp.arange(seq, dtype=jnp.float32)[:, None] * inv[None, :]
    cos = jnp.tile(jnp.cos(ang), (1, LANES // half))
    sign = jnp.tile(jnp.concatenate([-jnp.ones((half,), jnp.float32), jnp.ones((half,), jnp.float32)]),
                    LANES // HEAD_DIM)
    sin = jnp.tile(jnp.sin(ang), (1, LANES // half)) * sign[None, :]
    return cos, sin


def _layer(x3, mem, g_mix, g_mem, w_in, w_mem_kv, w_gate, b_gate, w_branch, w_out, g_ffn,
           w_group, b_group, w_sub, b_sub, w1, w3, w2, g_out):
    b, s, d = x3.shape
    bf = jnp.bfloat16
    cos_t, sin_t = _rope_tables(s)
    w_nat, w_dil = _pack_w_in(w_in)
    gm = g_mix[None, :]

    p4, n3 = _in_proj(x3, gm, w_nat.astype(bf), cos_t, sin_t, P_MODES, 1, emit_n=True)
    p3 = p4.reshape(b, s, p4.shape[3])
    srcs = [p4] + [_in_proj(n3, gm, w.astype(bf), cos_t, sin_t, D_MODES, B_PATTERNS[gi + 1][1])
                   for gi, w in enumerate(w_dil)]

    kv = _mem_kv(mem, g_mem[None, :], w_mem_kv.astype(bf))

    tri = (jnp.arange(MXU_WIDTH)[:, None] <= jnp.arange(MXU_WIDTH)[None, :]).astype(bf)
    ya = _dsa(p3, tri)

    obs, lss = [], []
    for gi, src in enumerate(srcs):
        tiles = (T_BQ, T_BK, T_BV) if gi == 0 else (0, 1, 2)
        o, l = _dilated(src, *tiles, gi)
        obs.append(o)
        lss.append(l)

    ym = _mem_attn(p3, kv)

    group = A_HEADS // A_KV_HEADS
    wa = jnp.swapaxes(w_branch[0].reshape(A_KV_HEADS, group, HEAD_DIM, d), 0, 1).reshape(-1, d).astype(bf)
    w_route = jnp.concatenate([w_group, jnp.moveaxis(w_sub, 0, 1).reshape(d, N_EXPERTS)], axis=1)
    w_route = jnp.pad(w_route, ((0, 0), (0, LANES - w_route.shape[1])))
    wrh = w_route.astype(bf)
    wrl = (w_route - wrh.astype(jnp.float32)).astype(bf)
    wr2 = jnp.concatenate([wrh, wrl], axis=1)
    r_bias = jnp.pad(jnp.concatenate([b_group, b_sub.reshape(-1)]), (0, LANES - MOE_GROUPS - N_EXPERTS))

    h, xn, rl = _merge(x3, gm, w_gate.astype(bf), b_gate[None, :], ya, wa, obs, lss,
                       w_branch[1].astype(bf), ym, w_branch[2].astype(bf), w_out.astype(bf),
                       g_ffn[None, :], wr2)
    t = b * s
    out = _moe(xn.reshape(t, d), rl.reshape(t, LANES), r_bias[None, :], h.reshape(t, d),
               w1.astype(bf), w3.astype(bf), w2.astype(bf), g_out[None, :])
    return out.reshape(b, s, d)


def kernel(x, mem, g_mix, g_mem, w_in, w_mem_kv, w_gate, b_gate, w_branch, w_out, g_ffn,
           w_group, b_group, w_sub, b_sub, w1, w3, w2, g_final):
    depth = g_mix.shape[0]
    assert depth == 1, "the final rmsnorm is applied right after the single layer"
    return _layer(x, mem, g_mix[0], g_mem[0], w_in[0], w_mem_kv[0], w_gate[0], b_gate[0], w_branch[0],
                  w_out[0], g_ffn[0], w_group[0], b_group[0], w_sub[0], b_sub[0], w1[0], w3[0], w2[0],
                  g_final)
```
